```python
import math
import jax, jax.numpy as jnp
from jax import lax
import numpy as np

D_MODEL = 2048
BATCH = 1
SEQ = 8192
DEPTH = 2

GRID_W = 64
CTX_LEN = 256
EPS = 1e-6
NEG_INF = -1e30

MIX_W = D_MODEL
ATTN_HEADS = 12
ATTN_KV_HEADS = 4
ATTN_HEAD_DIM = 64
ATTN_GROUP = ATTN_HEADS // ATTN_KV_HEADS
ATTN_W = ATTN_HEADS * ATTN_HEAD_DIM
ATTN_KV_W = ATTN_KV_HEADS * ATTN_HEAD_DIM
WINDOW = 128
ATTN_BLOCK = 128
ROPE_BASE = 10000.0
DN_HEADS = 6
DN_HEAD_DIM = 128
DN_W = DN_HEADS * DN_HEAD_DIM
DN_CONV = 5
DN_CHUNK = 64
S5_W = MIX_W - ATTN_W - DN_W
S5_GROUP_CH = 16
S5_GROUPS = S5_W // S5_GROUP_CH
S5_STATE = 64
IN_SPLITS = (ATTN_W, ATTN_KV_W, ATTN_KV_W, DN_W, DN_W, DN_W, DN_W, 2 * DN_HEADS, 2 * DN_HEADS, S5_W)
IN_COLS = ATTN_W + 2 * ATTN_KV_W + 4 * DN_W + 4 * DN_HEADS + S5_W
N_GROUPS = 4
EXPERTS_PER_GROUP = 8
N_EXPERTS = N_GROUPS * EXPERTS_PER_GROUP
TOP_K = 2
EXPERT_HIDDEN = 512

kernel_name = "hybrid_parallel_heads_dit_block"

F32 = jnp.float32


def rms_norm(x, w):
    x32 = x.astype(F32)
    return x32 * lax.rsqrt(jnp.mean(x32 * x32, axis=-1, keepdims=True) + EPS) * w.astype(F32)


def l2_norm(x):
    return x * lax.rsqrt(jnp.sum(x * x, axis=-1, keepdims=True) + EPS)


def modulate(h32, shift, scale):
    return h32 * (1.0 + scale.astype(F32)) + shift.astype(F32)


def split_cols(z):
    out, start = [], 0
    for w in IN_SPLITS:
        out.append(z[..., start:start + w])
        start += w
    return out


def flip_seq(t, rev):
    return jnp.flip(t, axis=1) if rev else t


def axial_angles(n):
    rows = n // GRID_W
    row = jnp.repeat(jnp.arange(rows), GRID_W).astype(F32)
    col = jnp.tile(jnp.arange(GRID_W), rows).astype(F32)
    nf = ATTN_HEAD_DIM // 4
    inv = ROPE_BASE ** (-jnp.arange(nf, dtype=F32) / nf)
    return row[:, None] * inv, col[:, None] * inv


def rotate_half(x, ang):
    f = ang.shape[-1]
    cos = jnp.cos(ang)[None, :, None, :]
    sin = jnp.sin(ang)[None, :, None, :]
    x1, x2 = x[..., :f], x[..., f:]
    return jnp.concatenate([x1 * cos - x2 * sin, x2 * cos + x1 * sin], axis=-1)


def apply_axial_rope(x, ang_row, ang_col):
    half = ATTN_HEAD_DIM // 2
    return jnp.concatenate([rotate_half(x[..., :half], ang_row), rotate_half(x[..., half:], ang_col)], axis=-1)


def band_mask(n):
    nb = n // ATTN_BLOCK
    blk = jnp.arange(nb)[:, None, None]
    qpos = blk * ATTN_BLOCK + jnp.arange(ATTN_BLOCK)[None, :, None]
    kpos = (blk - 1) * ATTN_BLOCK + jnp.arange(3 * ATTN_BLOCK)[None, None, :]
    return (jnp.abs(kpos - qpos) <= WINDOW) & (kpos >= 0) & (kpos < n)


def window_attention(qx, kx, vx, qc, kc, vc, ang_row, ang_col, mask, q_norm_w, k_norm_w, sink, ctx_out):
    b, n, _ = qx.shape
    lc = qc.shape[1]
    nb = n // ATTN_BLOCK
    dtype = qx.dtype
    scale = ATTN_HEAD_DIM ** -0.5
    q = apply_axial_rope(rms_norm(qx.reshape(b, n, ATTN_HEADS, ATTN_HEAD_DIM), q_norm_w), ang_row, ang_col) * scale
    k = apply_axial_rope(rms_norm(kx.reshape(b, n, ATTN_KV_HEADS, ATTN_HEAD_DIM), k_norm_w), ang_row, ang_col)
    v = vx.reshape(b, n, ATTN_KV_HEADS, ATTN_HEAD_DIM).astype(F32)
    q_c = rms_norm(qc.reshape(b, lc, ATTN_HEADS, ATTN_HEAD_DIM), q_norm_w) * scale
    k_c = rms_norm(kc.reshape(b, lc, ATTN_KV_HEADS, ATTN_HEAD_DIM), k_norm_w)
    v_c = vc.reshape(b, lc, ATTN_KV_HEADS, ATTN_HEAD_DIM).astype(F32)
    sink_logit = sink.astype(F32).reshape(ATTN_KV_HEADS, ATTN_GROUP)

    qb = q.reshape(b, nb, ATTN_BLOCK, ATTN_KV_HEADS, ATTN_GROUP, ATTN_HEAD_DIM)
    pad = ((0, 0), (ATTN_BLOCK, ATTN_BLOCK), (0, 0), (0, 0))
    kp = jnp.pad(k, pad).reshape(b, nb + 2, ATTN_BLOCK, ATTN_KV_HEADS, ATTN_HEAD_DIM)
    vp = jnp.pad(v, pad).reshape(b, nb + 2, ATTN_BLOCK, ATTN_KV_HEADS, ATTN_HEAD_DIM)
    k_band = jnp.concatenate([kp[:, :-2], kp[:, 1:-1], kp[:, 2:]], axis=2)
    v_band = jnp.concatenate([vp[:, :-2], vp[:, 1:-1], vp[:, 2:]], axis=2)
    s_lat = jnp.einsum('bnqkgd,bnjkd->bnkgqj', qb, k_band)
    s_lat = jnp.where(mask[None, :, None, None], s_lat, NEG_INF)
    s_ctx = jnp.einsum('bnqkgd,bjkd->bnkgqj', qb, k_c)
    s_sink = jnp.broadcast_to(sink_logit[None, None, :, :, None, None], s_lat.shape[:-1] + (1,))
    p = jax.nn.softmax(jnp.concatenate([s_lat, s_ctx, s_sink], axis=-1), axis=-1)
    nband = 3 * ATTN_BLOCK
    o = (jnp.einsum('bnkgqj,bnjkd->bnqkgd', p[..., :nband], v_band)
         + jnp.einsum('bnkgqj,bjkd->bnqkgd', p[..., nband:nband + lc], v_c))
    out_x = o.reshape(b, n, ATTN_W).astype(dtype)

    out_c = None
    if ctx_out:
        qcg = q_c.reshape(b, lc, ATTN_KV_HEADS, ATTN_GROUP, ATTN_HEAD_DIM)
        s = jnp.einsum('bqkgd,bjkd->bkgqj', qcg, k_c)
        s_sink_c = jnp.broadcast_to(sink_logit[None, :, :, None, None], s.shape[:-1] + (1,))
        pc = jax.nn.softmax(jnp.concatenate([s, s_sink_c], axis=-1), axis=-1)
        out_c = jnp.einsum('bkgqj,bjkd->bqkgd', pc[..., :lc], v_c).reshape(b, lc, ATTN_W).astype(dtype)
    return out_x, out_c


def centred_dwconv(x, w):
    return lax.conv_general_dilated(x, w[:, None, :], window_strides=(1,), padding='SAME',
                                    dimension_numbers=('NWC', 'WIO', 'NWC'), feature_group_count=x.shape[-1])


def gated_delta_chunked(q, k, v, beta, g, s0):
    b, n, h, dk = q.shape
    dv = v.shape[-1]
    nc = n // DN_CHUNK

    def chunks(t):
        return jnp.moveaxis(t.reshape(b, nc, DN_CHUNK, h, -1), 3, 1)

    qc, kc, vc = chunks(q), chunks(k), chunks(v)
    bc = chunks(beta[..., None])[..., 0]
    gcum = jnp.cumsum(chunks(g[..., None])[..., 0], axis=-1)
    idx = jnp.arange(DN_CHUNK)
    incl = idx[:, None] >= idx[None, :]
    strict = idx[:, None] > idx[None, :]
    diff = gcum[..., :, None] - gcum[..., None, :]
    decay = jnp.where(incl, jnp.exp(jnp.where(incl, diff, 0.0)), 0.0)
    kb = kc * bc[..., None]
    m = jnp.where(strict, jnp.einsum('bhnid,bhnjd->bhnij', kb, kc) * decay, 0.0)
    rhs = jnp.concatenate([vc * bc[..., None], kb * jnp.exp(gcum)[..., None]], axis=-1)
    sol = lax.linalg.triangular_solve(m, rhs, left_side=True, lower=True, unit_diagonal=True)
    u, w = sol[..., :dv], sol[..., dv:]
    aqk = jnp.where(incl, jnp.einsum('bhnid,bhnjd->bhnij', qc, kc) * decay, 0.0)
    qg = qc * jnp.exp(gcum)[..., None]
    kg = kc * jnp.exp(gcum[..., -1:] - gcum)[..., None]
    glast = jnp.exp(gcum[..., -1])
    xs = tuple(jnp.moveaxis(t, 2, 0) for t in (qg, kg, u, w, aqk, glast))

    def step(s, inp):
        qg_i, kg_i, u_i, w_i, aqk_i, gl_i = inp
        v_new = u_i - jnp.einsum('bhcd,bhde->bhce', w_i, s)
        o_i = jnp.einsum('bhcd,bhde->bhce', qg_i, s) + jnp.einsum('bhcj,bhje->bhce', aqk_i, v_new)
        s = s * gl_i[..., None, None] + jnp.einsum('bhcd,bhce->bhde', kg_i, v_new)
        return s, o_i

    s_fin, o = lax.scan(step, s0, xs)
    o = jnp.moveaxis(jnp.moveaxis(o, 0, 2), 1, 3).reshape(b, n, h, dv)
    return o, s_fin


def gated_deltanet(lat_in, ctx_in, conv_w, a_log, dt_bias, o_norm_w, ctx_out):
    def prep(q, k, v, beta, alpha):
        b, n, _ = q.shape
        qkv = jax.nn.silu(centred_dwconv(jnp.concatenate([q, k, v], axis=-1), conv_w)).astype(F32)
        q, k, v = jnp.split(qkv, 3, axis=-1)
        q = l2_norm(q.reshape(b, n, DN_HEADS, DN_HEAD_DIM)) * (DN_HEAD_DIM ** -0.5)
        k = l2_norm(k.reshape(b, n, DN_HEADS, DN_HEAD_DIM))
        v = v.reshape(b, n, DN_HEADS, DN_HEAD_DIM)
        bt = jax.nn.sigmoid(beta.astype(F32)).reshape(b, n, 2, DN_HEADS)
        gl = -jnp.exp(a_log.astype(F32)) * jax.nn.softplus(
            alpha.astype(F32).reshape(b, n, 2, DN_HEADS) + dt_bias.astype(F32))
        return q, k, v, bt, gl

    qx, kx, vx, gx, betax, alphax = lat_in
    qc, kc, vc, gc, betac, alphac = ctx_in
    lat = prep(qx, kx, vx, betax, alphax)
    ctx = prep(qc, kc, vc, betac, alphac)
    b = qx.shape[0]
    o_x, o_c = 0.0, 0.0
    for d, rev in enumerate((False, True)):
        s0 = jnp.zeros((b, DN_HEADS, DN_HEAD_DIM, DN_HEAD_DIM), F32)
        oc_d, s_ctx = gated_delta_chunked(flip_seq(ctx[0], rev), flip_seq(ctx[1], rev), flip_seq(ctx[2], rev),
                                          flip_seq(ctx[3][:, :, d], rev), flip_seq(ctx[4][:, :, d], rev), s0)
        ox_d, _ = gated_delta_chunked(flip_seq(lat[0], rev), flip_seq(lat[1], rev), flip_seq(lat[2], rev),
                                      flip_seq(lat[3][:, :, d], rev), flip_seq(lat[4][:, :, d], rev), s_ctx)
        o_x = o_x + flip_seq(ox_d, rev)
        o_c = o_c + flip_seq(oc_d, rev)

    def gated_out(o, gate):
        bb, n = gate.shape[:2]
        gt = jax.nn.silu(gate.astype(F32).reshape(bb, n, DN_HEADS, DN_HEAD_DIM))
        return (rms_norm(o, o_norm_w) * gt).reshape(bb, n, DN_W).astype(gate.dtype)

    return gated_out(o_x, gx), (gated_out(o_c, gc) if ctx_out else None)


def s5_discretize(lam_re, lam_im, log_step, b_re, b_im):
    lr = jnp.minimum(lam_re, -1e-4)
    dt = jnp.exp(log_step)[:, None]
    mag = jnp.exp(lr * dt)
    ar, ai = mag * jnp.cos(lam_im * dt), mag * jnp.sin(lam_im * dt)
    den = lr * lr + lam_im * lam_im
    nr, ni = ar - 1.0, ai
    fr = (nr * lr + ni * lam_im) / den
    fi = (ni * lr - nr * lam_im) / den
    bbr = fr[..., None] * b_re - fi[..., None] * b_im
    bbi = fr[..., None] * b_im + fi[..., None] * b_re
    return ar, ai, bbr, bbi


def s5_scan(bu_r, bu_i, ar, ai, h0_r, h0_i, reverse):
    pos = -1 if reverse else 0
    bu_r = bu_r.at[:, pos].add(ar * h0_r - ai * h0_i)
    bu_i = bu_i.at[:, pos].add(ar * h0_i + ai * h0_r)
    a_r = jnp.broadcast_to(ar, bu_r.shape)
    a_i = jnp.broadcast_to(ai, bu_i.shape)

    def combine(e1, e2):
        a1r, a1i, b1r, b1i = e1
        a2r, a2i, b2r, b2i = e2
        return (a2r * a1r - a2i * a1i, a2r * a1i + a2i * a1r,
                a2r * b1r - a2i * b1i + b2r, a2r * b1i + a2i * b1r + b2i)

    _, _, hr, hi = lax.associative_scan(combine, (a_r, a_i, bu_r, bu_i), reverse=reverse, axis=1)
    return hr, hi


def s5_readout(hr, hi, c_re, c_im):
    return jnp.einsum('blgp,ghp->blgh', hr, c_re) - jnp.einsum('blgp,ghp->blgh', hi, c_im)


def s5_mixer(u_x, u_c, lam_re, lam_im, log_step, b_re, b_im, c_re, c_im, d_skip, w_glu, ctx_out):
    b, n, _ = u_x.shape
    lc = u_c.shape[1]
    dtype = u_x.dtype
    ux = u_x.astype(F32).reshape(b, n, S5_GROUPS, S5_GROUP_CH)
    uc = u_c.astype(F32).reshape(b, lc, S5_GROUPS, S5_GROUP_CH)
    dsk = d_skip.astype(F32).reshape(S5_GROUPS, S5_GROUP_CH)
    yx, yc = ux * dsk, uc * dsk
    zero = jnp.zeros((b, S5_GROUPS, S5_STATE), F32)
    for d, rev in enumerate((False, True)):
        ar, ai, bbr, bbi = s5_discretize(lam_re[d].astype(F32), lam_im[d].astype(F32), log_step[d].astype(F32),
                                         b_re[d].astype(F32), b_im[d].astype(F32))
        cr, ci = c_re[d].astype(F32), c_im[d].astype(F32)
        hcr, hci = s5_scan(jnp.einsum('blgh,gph->blgp', uc, bbr), jnp.einsum('blgh,gph->blgp', uc, bbi),
                           ar, ai, zero, zero, rev)
        end = 0 if rev else -1
        hxr, hxi = s5_scan(jnp.einsum('blgh,gph->blgp', ux, bbr), jnp.einsum('blgh,gph->blgp', ux, bbi),
                           ar, ai, hcr[:, end], hci[:, end], rev)
        yx = yx + s5_readout(hxr, hxi, cr, ci)
        if ctx_out:
            yc = yc + s5_readout(hcr, hci, cr, ci)

    def glu(y, m):
        y = jax.nn.gelu(y.reshape(b, m, S5_W))
        return (y * jax.nn.sigmoid(y @ w_glu.astype(F32))).astype(dtype)

    return glu(yx, n), (glu(yc, lc) if ctx_out else None)


def hier_moe(h, w_grp, b_grp, w_rt, b_rt, w1, w3, w2):
    b, n, d = h.shape
    t = h.reshape(b * n, d)
    g_logits = (t @ w_grp).astype(F32) + b_grp.astype(F32)
    g_prob = jax.nn.softmax(g_logits, axis=-1)
    g_onehot = jax.nn.one_hot(jnp.argmax(g_logits, axis=-1), N_GROUPS, dtype=F32)
    g_w = jnp.max(g_prob, axis=-1, keepdims=True)
    e_logits = ((t @ w_rt).astype(F32) + b_rt.astype(F32)).reshape(-1, N_GROUPS, EXPERTS_PER_GROUP)
    e_sel = jnp.sum(e_logits * g_onehot[:, :, None], axis=1)
    top_v, top_i = lax.top_k(e_sel, TOP_K)
    top_w = jax.nn.softmax(top_v, axis=-1) * g_w
    within = jnp.sum(jax.nn.one_hot(top_i, EXPERTS_PER_GROUP, dtype=F32) * top_w[..., None], axis=1)
    gate = g_onehot[:, :, None] * within[:, None, :]
    out = jnp.zeros((b * n, d), F32)
    for gi in range(N_GROUPS):
        a = jnp.einsum('td,edf->tef', t, w1[gi]).astype(F32)
        u = jnp.einsum('td,edf->tef', t, w3[gi]).astype(F32)
        act = jax.nn.silu(a) * u * gate[:, gi, :, None]
        out = out + jnp.einsum('tef,efd->td', act.astype(t.dtype), w2[gi]).astype(F32)
    return out.reshape(b, n, d).astype(h.dtype)


def setup_inputs(seed: int = 0) -> dict:
    key = jax.random.key(seed)
    k = jax.random.split(key, 40)
    L = DEPTH

    def nrm(i, shape, s):
        return jax.random.normal(k[i], shape, F32) * s

    lam_im = (jnp.pi * jnp.arange(S5_STATE, dtype=F32))[None, None, None, :] + nrm(21, (L, 2, S5_GROUPS, S5_STATE), 0.01)
    dn_dt = jnp.exp(jax.random.uniform(k[13], (L, 2, DN_HEADS), F32, math.log(1e-3), math.log(1e-1)))
    return {
        "x": nrm(0, (BATCH, SEQ, D_MODEL), 1.0),
        "c": nrm(1, (BATCH, D_MODEL), 1.0),
        "ctx": nrm(2, (BATCH, CTX_LEN, D_MODEL), 1.0),
        "c_ctx": nrm(3, (D_MODEL,), 1.0),
        "w_ada": nrm(4, (L, D_MODEL, 6 * D_MODEL), 0.5 * D_MODEL ** -0.5),
        "b_ada": nrm(5, (L, 6 * D_MODEL), 0.02),
        "norm1_w": 1.0 + nrm(6, (L, D_MODEL), 0.02),
        "norm2_w": 1.0 + nrm(7, (L, D_MODEL), 0.02),
        "w_in": nrm(8, (L, D_MODEL, IN_COLS), D_MODEL ** -0.5),
        "w_out": nrm(9, (L, MIX_W, D_MODEL), MIX_W ** -0.5),
        "attn_q_norm": 1.0 + nrm(10, (L, ATTN_HEAD_DIM), 0.02),
        "attn_k_norm": 1.0 + nrm(11, (L, ATTN_HEAD_DIM), 0.02),
        "attn_sink": nrm(12, (L, ATTN_HEADS), 0.5),
        "dn_conv": nrm(14, (L, DN_CONV, 3 * DN_W), DN_CONV ** -0.5),
        "dn_a_log": jnp.log(jax.random.uniform(k[15], (L, 2, DN_HEADS), F32, 1.0, 16.0)),
        "dn_dt_bias": dn_dt + jnp.log(-jnp.expm1(-dn_dt)),
        "dn_o_norm": 1.0 + nrm(16, (L, DN_HEAD_DIM), 0.02),
        "s5_lam_re": -0.5 + nrm(20, (L, 2, S5_GROUPS, S5_STATE), 0.01),
        "s5_lam_im": lam_im,
        "s5_log_step": jax.random.uniform(k[22], (L, 2, S5_GROUPS), F32, math.log(1e-3), math.log(1e-1)),
        "s5_b_re": nrm(23, (L, 2, S5_GROUPS, S5_STATE, S5_GROUP_CH), S5_GROUP_CH ** -0.5),
        "s5_b_im": nrm(24, (L, 2, S5_GROUPS, S5_STATE, S5_GROUP_CH), S5_GROUP_CH ** -0.5),
        "s5_c_re": nrm(25, (L, 2, S5_GROUPS, S5_GROUP_CH, S5_STATE), S5_STATE ** -0.5),
        "s5_c_im": nrm(26, (L, 2, S5_GROUPS, S5_GROUP_CH, S5_STATE), S5_STATE ** -0.5),
        "s5_d": nrm(27, (L, S5_W), 0.5),
        "s5_w_glu": nrm(28, (L, S5_W, S5_W), S5_W ** -0.5),
        "moe_w_grp": nrm(30, (L, D_MODEL, N_GROUPS), D_MODEL ** -0.5),
        "moe_b_grp": nrm(31, (L, N_GROUPS), 0.01),
        "moe_w_rt": nrm(32, (L, D_MODEL, N_EXPERTS), D_MODEL ** -0.5),
        "moe_b_rt": nrm(33, (L, N_EXPERTS), 0.01),
        "moe_w1": nrm(34, (L, N_GROUPS, EXPERTS_PER_GROUP, D_MODEL, EXPERT_HIDDEN), D_MODEL ** -0.5),
        "moe_w3": nrm(35, (L, N_GROUPS, EXPERTS_PER_GROUP, D_MODEL, EXPERT_HIDDEN), D_MODEL ** -0.5),
        "moe_w2": nrm(36, (L, N_GROUPS, EXPERTS_PER_GROUP, EXPERT_HIDDEN, D_MODEL), EXPERT_HIDDEN ** -0.5),
    }


def reference(x, c, ctx, c_ctx, w_ada, b_ada, norm1_w, norm2_w, w_in, w_out, attn_q_norm, attn_k_norm, attn_sink,
              dn_conv, dn_a_log, dn_dt_bias, dn_o_norm, s5_lam_re, s5_lam_im, s5_log_step, s5_b_re, s5_b_im,
              s5_c_re, s5_c_im, s5_d, s5_w_glu, moe_w_grp, moe_b_grp, moe_w_rt, moe_b_rt, moe_w1, moe_w3, moe_w2):
    b, n, d = x.shape
    ang_row, ang_col = axial_angles(n)
    mask = band_mask(n)
    cx = ctx
    for layer in range(DEPTH):
        ctx_out = layer < DEPTH - 1
        mod_x = (jax.nn.silu(c) @ w_ada[layer] + b_ada[layer]).reshape(b, 6, 1, d)
        mod_c = (jax.nn.silu(c_ctx) @ w_ada[layer] + b_ada[layer]).reshape(6, d)

        hx = modulate(rms_norm(x, norm1_w[layer]), mod_x[:, 0], mod_x[:, 1]).astype(x.dtype)
        hc = modulate(rms_norm(cx, norm1_w[layer]), mod_c[0], mod_c[1]).astype(cx.dtype)
        zx = split_cols(hx @ w_in[layer])
        zc = split_cols(hc @ w_in[layer])
        ax, ac = window_attention(zx[0], zx[1], zx[2], zc[0], zc[1], zc[2], ang_row, ang_col, mask,
                                  attn_q_norm[layer], attn_k_norm[layer], attn_sink[layer], ctx_out)
        bx, bc = gated_deltanet(zx[3:9], zc[3:9], dn_conv[layer], dn_a_log[layer], dn_dt_bias[layer],
                                dn_o_norm[layer], ctx_out)
        sx, sc = s5_mixer(zx[9], zc[9], s5_lam_re[layer], s5_lam_im[layer], s5_log_step[layer], s5_b_re[layer],
                          s5_b_im[layer], s5_c_re[layer], s5_c_im[layer], s5_d[layer], s5_w_glu[layer], ctx_out)
        x = x + mod_x[:, 2] * (jnp.concatenate([ax, bx, sx], axis=-1) @ w_out[layer])

        hx = modulate(rms_norm(x, norm2_w[layer]), mod_x[:, 3], mod_x[:, 4]).astype(x.dtype)
        x = x + mod_x[:, 5] * hier_moe(hx, moe_w_grp[layer], moe_b_grp[layer], moe_w_rt[layer], moe_b_rt[layer],
                                       moe_w1[layer], moe_w3[layer], moe_w2[layer])
        if ctx_out:
            cx = cx + mod_c[2] * (jnp.concatenate([ac, bc, sc], axis=-1) @ w_out[layer])
            hc = modulate(rms_norm(cx, norm2_w[layer]), mod_c[3], mod_c[4]).astype(cx.dtype)
            cx = cx + mod_c[5] * hier_moe(hc, moe_w_grp[layer], moe_b_grp[layer], moe_w_rt[layer], moe_b_rt[layer],
                                          moe_w1[layer], moe_w3[layer], moe_w2[layer])
    return x
```

```python
import functools
import math

import jax
import jax.numpy as jnp
import numpy as np
from jax import lax
from jax.experimental import pallas as pl
from jax.experimental.pallas import tpu as pltpu

F32 = jnp.float32
BF16 = jnp.bfloat16
HIGHEST = lax.Precision.HIGHEST

D_MODEL = 2048
SEQ = 8192
DEPTH = 2
GRID_W = 64
CTX_LEN = 256
N_ALL = CTX_LEN + SEQ
EPS = 1e-6
NEG_INF = -1e30

ATTN_HEADS = 12
ATTN_KV_HEADS = 4
ATTN_HEAD_DIM = 64
ATTN_GROUP = ATTN_HEADS // ATTN_KV_HEADS
ATTN_W = ATTN_HEADS * ATTN_HEAD_DIM
ATTN_KV_W = ATTN_KV_HEADS * ATTN_HEAD_DIM
WINDOW = 128
ATTN_BLOCK = 128
ROPE_BASE = 10000.0
DN_HEADS = 6
DN_HEAD_DIM = 128
DN_W = DN_HEADS * DN_HEAD_DIM
DN_CONV = 5
DN_CHUNK = 64
S5_W = D_MODEL - ATTN_W - DN_W
S5_GROUP_CH = 16
S5_GROUPS = S5_W // S5_GROUP_CH
S5_STATE = 64
N_GROUPS = 4
EXPERTS_PER_GROUP = 8
N_EXPERTS = N_GROUPS * EXPERTS_PER_GROUP
TOP_K = 2
EXPERT_HIDDEN = 512

Z_Q, Z_K, Z_V = 0, 768, 1024
Z_DNQ, Z_DNK, Z_DNV, Z_DNG = 1280, 2048, 2816, 3584
Z_S5 = 4352
Z_BETA, Z_ALPHA = 4864, 4876
Z_USED = 4888
Z_W = 5120

VMEM_LIMIT = 56 * 1024 * 1024


def _cparams(sem, vmem=VMEM_LIMIT):
    return pltpu.CompilerParams(dimension_semantics=sem, vmem_limit_bytes=vmem)


def _adaln_kernel(c_ref, w_ref, b_ref, o_ref):
    cv = c_ref[...]
    s = cv * jax.nn.sigmoid(cv)
    o_ref[0] = jnp.dot(s, w_ref[0], preferred_element_type=F32, precision=HIGHEST) + b_ref[0]


def adaln(cvec, w_ada, b_ada):
    L = w_ada.shape[0]
    tn = 1024
    n6 = 6 * D_MODEL
    return pl.pallas_call(
        _adaln_kernel,
        grid=(L, n6 // tn),
        in_specs=[
            pl.BlockSpec((8, D_MODEL), lambda l, j: (0, 0)),
            pl.BlockSpec((1, D_MODEL, tn), lambda l, j: (l, 0, j)),
            pl.BlockSpec((1, 1, tn), lambda l, j: (l, 0, j)),
        ],
        out_specs=pl.BlockSpec((1, 8, tn), lambda l, j: (l, 0, j)),
        out_shape=jax.ShapeDtypeStruct((L, 8, n6), F32),
        compiler_params=_cparams(("parallel", "parallel")),
        name="adaln",
    )(cvec, w_ada, b_ada.reshape(L, 1, n6))


def _row_is_ctx(tm):
    row = pl.program_id(0) * tm + lax.broadcasted_iota(jnp.int32, (tm, 1), 0)
    return row < CTX_LEN


def _norm_mod(x, nw, mod_ref, shift_i, scale_i, is_ctx):
    ms = jnp.mean(x * x, axis=-1, keepdims=True)
    h = x * lax.rsqrt(ms + EPS) * nw
    sc = jnp.where(is_ctx, mod_ref[0, scale_i:scale_i + 1, :], mod_ref[1, scale_i:scale_i + 1, :])
    sh = jnp.where(is_ctx, mod_ref[0, shift_i:shift_i + 1, :], mod_ref[1, shift_i:shift_i + 1, :])
    return h * (1.0 + sc) + sh


def _in_proj_kernel(x_ref, nw_ref, mod_ref, w_ref, o_ref, h_scr, *, tm):
    @pl.when(pl.program_id(1) == 0)
    def _():
        h = _norm_mod(x_ref[...], nw_ref[...], mod_ref, 0, 1, _row_is_ctx(tm))
        h_scr[...] = h.astype(BF16)

    o_ref[...] = jnp.dot(h_scr[...], w_ref[...], preferred_element_type=F32)


def in_proj(xa, norm_w, mod, w_in_p):
    tm, tn = 512, 1024
    n = xa.shape[0]
    return pl.pallas_call(
        functools.partial(_in_proj_kernel, tm=tm),
        grid=(pl.cdiv(n, tm), Z_W // tn),
        in_specs=[
            pl.BlockSpec((tm, D_MODEL), lambda i, j: (i, 0)),
            pl.BlockSpec((1, D_MODEL), lambda i, j: (0, 0)),
            pl.BlockSpec((2, 8, D_MODEL), lambda i, j: (0, 0, 0)),
            pl.BlockSpec((D_MODEL, tn), lambda i, j: (0, j)),
        ],
        out_specs=pl.BlockSpec((tm, tn), lambda i, j: (i, j)),
        out_shape=jax.ShapeDtypeStruct((n, Z_W), F32),
        scratch_shapes=[pltpu.VMEM((tm, D_MODEL), BF16)],
        compiler_params=_cparams(("parallel", "arbitrary")),
        name="in_proj",
    )(xa, norm_w.reshape(1, D_MODEL), mod, w_in_p)


def _out_proj_kernel(mix_ref, x_ref, mod_ref, w_ref, nw_ref, wr_ref, br_ref, xo_ref, h_ref, lg_ref, *, tm):
    is_ctx = _row_is_ctx(tm)
    y = jnp.dot(mix_ref[...], w_ref[...], preferred_element_type=F32)
    gate = jnp.where(is_ctx, mod_ref[0, 2:3, :], mod_ref[1, 2:3, :])
    xn = x_ref[...] + gate * y
    xo_ref[...] = xn
    h = _norm_mod(xn, nw_ref[...], mod_ref, 3, 4, is_ctx)
    h_ref[...] = h
    lg_ref[...] = jnp.dot(h, wr_ref[...], preferred_element_type=F32, precision=HIGHEST) + br_ref[...]


def out_proj(mix, xa, mod, w_out_b, norm2_w, w_router, b_router):
    tm = 256
    n = xa.shape[0]
    return pl.pallas_call(
        functools.partial(_out_proj_kernel, tm=tm),
        grid=(n // tm,),
        in_specs=[
            pl.BlockSpec((tm, D_MODEL), lambda i: (i, 0)),
            pl.BlockSpec((tm, D_MODEL), lambda i: (i, 0)),
            pl.BlockSpec((2, 8, D_MODEL), lambda i: (0, 0, 0)),
            pl.BlockSpec((D_MODEL, D_MODEL), lambda i: (0, 0)),
            pl.BlockSpec((1, D_MODEL), lambda i: (0, 0)),
            pl.BlockSpec((D_MODEL, 128), lambda i: (0, 0)),
            pl.BlockSpec((1, 128), lambda i: (0, 0)),
        ],
        out_specs=[
            pl.BlockSpec((tm, D_MODEL), lambda i: (i, 0)),
            pl.BlockSpec((tm, D_MODEL), lambda i: (i, 0)),
            pl.BlockSpec((tm, 128), lambda i: (i, 0)),
        ],
        out_shape=[
            jax.ShapeDtypeStruct((n, D_MODEL), F32),
            jax.ShapeDtypeStruct((n, D_MODEL), F32),
            jax.ShapeDtypeStruct((n, 128), F32),
        ],
        compiler_params=_cparams(("parallel",)),
        name="out_proj",
    )(mix, xa, mod, w_out_b, norm2_w.reshape(1, D_MODEL), w_router, b_router)


def _router_kernel(lg_ref, o_ref):
    lg = lg_ref[...]
    tm = lg.shape[0]
    lane = lax.broadcasted_iota(jnp.int32, (tm, 128), 1)
    is_g = lane < N_GROUPS
    gl = jnp.where(is_g, lg, NEG_INF)
    gmax = jnp.max(gl, axis=-1, keepdims=True)
    gidx = jnp.min(jnp.where((gl == gmax) & is_g, lane, 128), axis=-1, keepdims=True)
    gsum = jnp.sum(jnp.where(is_g, jnp.exp(gl - gmax), 0.0), axis=-1, keepdims=True)
    g_w = 1.0 / gsum
    e_lane = lane - N_GROUPS
    in_grp = (e_lane >= gidx * EXPERTS_PER_GROUP) & (e_lane < (gidx + 1) * EXPERTS_PER_GROUP)
    el = jnp.where(in_grp, lg, NEG_INF)
    v1 = jnp.max(el, axis=-1, keepdims=True)
    i1 = jnp.min(jnp.where((el == v1) & in_grp, e_lane, 128), axis=-1, keepdims=True)
    el2 = jnp.where(e_lane == i1, NEG_INF, el)
    in2 = in_grp & (e_lane != i1)
    v2 = jnp.max(el2, axis=-1, keepdims=True)
    i2 = jnp.min(jnp.where((el2 == v2) & in2, e_lane, 128), axis=-1, keepdims=True)
    e2 = jnp.exp(v2 - v1)
    w1 = g_w / (1.0 + e2)
    w2 = g_w * e2 / (1.0 + e2)
    out = jnp.where(lane == 0, i1.astype(F32), 0.0)
    out = jnp.where(lane == 1, i2.astype(F32), out)
    out = jnp.where(lane == 2, w1, out)
    out = jnp.where(lane == 3, w2, out)
    o_ref[...] = out


def router(logits):
    tm = 256
    n = logits.shape[0]
    return pl.pallas_call(
        _router_kernel,
        grid=(n // tm,),
        in_specs=[pl.BlockSpec((tm, 128), lambda i: (i, 0))],
        out_specs=pl.BlockSpec((tm, 128), lambda i: (i, 0)),
        out_shape=jax.ShapeDtypeStruct((n, 128), F32),
        compiler_params=_cparams(("parallel",)),
        name="router",
    )(logits)


MOE_TILE = 256


def _moe_tiles(n):
    return TOP_K * n // MOE_TILE + N_EXPERTS


def _expert_kernel(te_ref, src_ref, nt_ref, h_hbm, wrow_ref, w1_ref, w3_ref, w2_ref, y_ref,
                   xbuf, sem, w1b, w3b, w2b):
    i = pl.program_id(0)
    n_tiles = nt_ref[0]

    def gather(tile, slot, start):
        def body(r, c):
            tok = src_ref[tile * MOE_TILE + r]
            cp = pltpu.make_async_copy(h_hbm.at[pl.ds(tok, 1)], xbuf.at[slot, pl.ds(r, 1)], sem.at[slot])
            if start:
                cp.start()
            else:
                cp.wait()
            return c
        lax.fori_loop(0, MOE_TILE, body, 0)

    @pl.when(i == 0)
    def _():
        gather(0, 0, True)

    @pl.when(i + 1 < n_tiles)
    def _():
        gather(i + 1, (i + 1) % 2, True)

    @pl.when(i < n_tiles)
    def _():
        slot = i % 2
        gather(i, slot, False)
        prev_e = te_ref[jnp.maximum(i - 1, 0)]
        new_e = (i == 0) | (te_ref[i] != prev_e)

        @pl.when(new_e)
        def _():
            w1b[...] = w1_ref[0].astype(BF16)
            w3b[...] = w3_ref[0].astype(BF16)
            w2b[...] = w2_ref[0].astype(BF16)

        xt = xbuf[slot].astype(BF16)
        a = jnp.dot(xt, w1b[...], preferred_element_type=F32)
        u = jnp.dot(xt, w3b[...], preferred_element_type=F32)
        act = (a * jax.nn.sigmoid(a)) * u * wrow_ref[...]
        y_ref[...] = jnp.dot(act.astype(BF16), w2b[...], preferred_element_type=F32)

    @pl.when(i >= n_tiles)
    def _():
        y_ref[...] = jnp.zeros_like(y_ref)


def expert_mlp(tile_expert, src_tok, n_tiles, h2, wrow, w1, w3, w2):
    e_map = lambda i, te, src, nt: (te[i], 0, 0)
    moe_tiles = tile_expert.shape[0]
    grid_spec = pltpu.PrefetchScalarGridSpec(
        num_scalar_prefetch=3,
        grid=(moe_tiles,),
        in_specs=[
            pl.BlockSpec(memory_space=pl.ANY),
            pl.BlockSpec((MOE_TILE, 1), lambda i, te, src, nt: (i, 0)),
            pl.BlockSpec((1, D_MODEL, EXPERT_HIDDEN), e_map),
            pl.BlockSpec((1, D_MODEL, EXPERT_HIDDEN), e_map),
            pl.BlockSpec((1, EXPERT_HIDDEN, D_MODEL), e_map),
        ],
        out_specs=pl.BlockSpec((MOE_TILE, D_MODEL), lambda i, te, src, nt: (i, 0)),
        scratch_shapes=[
            pltpu.VMEM((2, MOE_TILE, D_MODEL), F32),
            pltpu.SemaphoreType.DMA((2,)),
            pltpu.VMEM((D_MODEL, EXPERT_HIDDEN), BF16),
            pltpu.VMEM((D_MODEL, EXPERT_HIDDEN), BF16),
            pltpu.VMEM((EXPERT_HIDDEN, D_MODEL), BF16),
        ],
    )
    return pl.pallas_call(
        _expert_kernel,
        grid_spec=grid_spec,
        out_shape=jax.ShapeDtypeStruct((moe_tiles * MOE_TILE, D_MODEL), F32),
        compiler_params=_cparams(("arbitrary",)),
        name="expert_mlp",
    )(tile_expert, src_tok, n_tiles, h2, wrow, w1, w3, w2)


CMB_TILE = 256


def _combine_kernel(pos_ref, y_hbm, x_ref, mod_ref, o_ref, ybuf, sem):
    i = pl.program_id(0)
    nt = pl.num_programs(0)

    def gather(tile, slot, start):
        def body(r, c):
            for k in range(TOP_K):
                p = pos_ref[(tile * CMB_TILE + r) * TOP_K + k]
                cp = pltpu.make_async_copy(y_hbm.at[pl.ds(p, 1)], ybuf.at[slot, k, pl.ds(r, 1)], sem.at[slot])
                if start:
                    cp.start()
                else:
                    cp.wait()
            return c
        lax.fori_loop(0, CMB_TILE, body, 0)

    @pl.when(i == 0)
    def _():
        gather(0, 0, True)

    @pl.when(i + 1 < nt)
    def _():
        gather(i + 1, (i + 1) % 2, True)

    slot = i % 2
    gather(i, slot, False)
    gate = jnp.where(_row_is_ctx(CMB_TILE), mod_ref[0, 5:6, :], mod_ref[1, 5:6, :])
    o_ref[...] = x_ref[...] + gate * (ybuf[slot, 0] + ybuf[slot, 1])


def moe_combine(pos, y_sorted, xa, mod):
    n = xa.shape[0]
    grid_spec = pltpu.PrefetchScalarGridSpec(
        num_scalar_prefetch=1,
        grid=(n // CMB_TILE,),
        in_specs=[
            pl.BlockSpec(memory_space=pl.ANY),
            pl.BlockSpec((CMB_TILE, D_MODEL), lambda i, pos: (i, 0)),
            pl.BlockSpec((2, 8, D_MODEL), lambda i, pos: (0, 0, 0)),
        ],
        out_specs=pl.BlockSpec((CMB_TILE, D_MODEL), lambda i, pos: (i, 0)),
        scratch_shapes=[
            pltpu.VMEM((2, TOP_K, CMB_TILE, D_MODEL), F32),
            pltpu.SemaphoreType.DMA((2,)),
        ],
    )
    return pl.pallas_call(
        _combine_kernel,
        grid_spec=grid_spec,
        out_shape=jax.ShapeDtypeStruct((n, D_MODEL), F32),
        compiler_params=_cparams(("arbitrary",)),
        name="moe_combine",
    )(pos, y_sorted, xa, mod)


def moe_dispatch_plan(route):
    n = route.shape[0]
    eid = route[:, 0:TOP_K].astype(jnp.int32).reshape(-1)
    wgt = route[:, TOP_K:2 * TOP_K].reshape(-1)
    p_total = eid.shape[0]
    order = jnp.argsort(eid, stable=True).astype(jnp.int32)
    e_sorted = eid[order]
    counts = jnp.zeros((N_EXPERTS,), jnp.int32).at[eid].add(1)
    tiles_per = (counts + MOE_TILE - 1) // MOE_TILE
    tile_off = jnp.cumsum(tiles_per) - tiles_per
    off = jnp.cumsum(counts) - counts
    rank = jnp.arange(p_total, dtype=jnp.int32) - off[e_sorted]
    dest = tile_off[e_sorted] * MOE_TILE + rank
    moe_tiles = _moe_tiles(n)
    n_rows = moe_tiles * MOE_TILE
    src_tok = jnp.zeros((n_rows,), jnp.int32).at[dest].set(order // TOP_K)
    wrow = jnp.zeros((n_rows,), F32).at[dest].set(wgt[order])
    pos = jnp.zeros((p_total,), jnp.int32).at[order].set(dest)
    n_tiles = jnp.sum(tiles_per).astype(jnp.int32)
    tile_ids = jnp.arange(moe_tiles, dtype=jnp.int32)
    tile_expert = jnp.sum((tile_ids[:, None] >= (tile_off + tiles_per)[None, :]).astype(jnp.int32), axis=1)
    tile_expert = jnp.minimum(tile_expert, N_EXPERTS - 1)
    return tile_expert, src_tok, n_tiles.reshape(1), wrow.reshape(n_rows, 1), pos


def moe_block(xa, h2, logits, mod, w1, w3, w2):
    route = router(logits)
    tile_expert, src_tok, n_tiles, wrow, pos = moe_dispatch_plan(route)
    y_sorted = expert_mlp(tile_expert, src_tok, n_tiles, h2, wrow, w1, w3, w2)
    return moe_combine(pos, y_sorted, xa, mod)


def _rms_norm(x, w):
    x32 = x.astype(F32)
    return x32 * lax.rsqrt(jnp.mean(x32 * x32, axis=-1, keepdims=True) + EPS) * w.astype(F32)


def _l2_norm(x):
    return x * lax.rsqrt(jnp.sum(x * x, axis=-1, keepdims=True) + EPS)


def _flip_seq(t, rev):
    return jnp.flip(t, axis=1) if rev else t


def _axial_angles(n):
    rows = n // GRID_W
    row = jnp.repeat(jnp.arange(rows), GRID_W).astype(F32)
    col = jnp.tile(jnp.arange(GRID_W), rows).astype(F32)
    nf = ATTN_HEAD_DIM // 4
    inv = ROPE_BASE ** (-jnp.arange(nf, dtype=F32) / nf)
    return row[:, None] * inv, col[:, None] * inv


def _rotate_half(x, ang):
    f = ang.shape[-1]
    cos = jnp.cos(ang)[None, :, None, :]
    sin = jnp.sin(ang)[None, :, None, :]
    x1, x2 = x[..., :f], x[..., f:]
    return jnp.concatenate([x1 * cos - x2 * sin, x2 * cos + x1 * sin], axis=-1)


def _apply_axial_rope(x, ang_row, ang_col):
    half = ATTN_HEAD_DIM // 2
    return jnp.concatenate([_rotate_half(x[..., :half], ang_row), _rotate_half(x[..., half:], ang_col)], axis=-1)


def _band_mask(n):
    nb = n // ATTN_BLOCK
    blk = jnp.arange(nb)[:, None, None]
    qpos = blk * ATTN_BLOCK + jnp.arange(ATTN_BLOCK)[None, :, None]
    kpos = (blk - 1) * ATTN_BLOCK + jnp.arange(3 * ATTN_BLOCK)[None, None, :]
    return (jnp.abs(kpos - qpos) <= WINDOW) & (kpos >= 0) & (kpos < n)


def _window_attention_jnp(qx, kx, vx, qc, kc, vc, q_norm_w, k_norm_w, sink, ctx_out):
    b, n, _ = qx.shape
    ang_row, ang_col = _axial_angles(n)
    mask = _band_mask(n)
    lc = qc.shape[1]
    nb = n // ATTN_BLOCK
    dtype = qx.dtype
    scale = ATTN_HEAD_DIM ** -0.5
    q = _apply_axial_rope(_rms_norm(qx.reshape(b, n, ATTN_HEADS, ATTN_HEAD_DIM), q_norm_w), ang_row, ang_col) * scale
    k = _apply_axial_rope(_rms_norm(kx.reshape(b, n, ATTN_KV_HEADS, ATTN_HEAD_DIM), k_norm_w), ang_row, ang_col)
    v = vx.reshape(b, n, ATTN_KV_HEADS, ATTN_HEAD_DIM).astype(F32)
    q_c = _rms_norm(qc.reshape(b, lc, ATTN_HEADS, ATTN_HEAD_DIM), q_norm_w) * scale
    k_c = _rms_norm(kc.reshape(b, lc, ATTN_KV_HEADS, ATTN_HEAD_DIM), k_norm_w)
    v_c = vc.reshape(b, lc, ATTN_KV_HEADS, ATTN_HEAD_DIM).astype(F32)
    sink_logit = sink.astype(F32).reshape(ATTN_KV_HEADS, ATTN_GROUP)
    qb = q.reshape(b, nb, ATTN_BLOCK, ATTN_KV_HEADS, ATTN_GROUP, ATTN_HEAD_DIM)
    pad = ((0, 0), (ATTN_BLOCK, ATTN_BLOCK), (0, 0), (0, 0))
    kp = jnp.pad(k, pad).reshape(b, nb + 2, ATTN_BLOCK, ATTN_KV_HEADS, ATTN_HEAD_DIM)
    vp = jnp.pad(v, pad).reshape(b, nb + 2, ATTN_BLOCK, ATTN_KV_HEADS, ATTN_HEAD_DIM)
    k_band = jnp.concatenate([kp[:, :-2], kp[:, 1:-1], kp[:, 2:]], axis=2)
    v_band = jnp.concatenate([vp[:, :-2], vp[:, 1:-1], vp[:, 2:]], axis=2)
    s_lat = jnp.einsum('bnqkgd,bnjkd->bnkgqj', qb, k_band)
    s_lat = jnp.where(mask[None, :, None, None], s_lat, NEG_INF)
    s_ctx = jnp.einsum('bnqkgd,bjkd->bnkgqj', qb, k_c)
    s_sink = jnp.broadcast_to(sink_logit[None, None, :, :, None, None], s_lat.shape[:-1] + (1,))
    p = jax.nn.softmax(jnp.concatenate([s_lat, s_ctx, s_sink], axis=-1), axis=-1)
    nband = 3 * ATTN_BLOCK
    o = (jnp.einsum('bnkgqj,bnjkd->bnqkgd', p[..., :nband], v_band)
         + jnp.einsum('bnkgqj,bjkd->bnqkgd', p[..., nband:nband + lc], v_c))
    out_x = o.reshape(b, n, ATTN_W).astype(dtype)
    out_c = None
    if ctx_out:
        qcg = q_c.reshape(b, lc, ATTN_KV_HEADS, ATTN_GROUP, ATTN_HEAD_DIM)
        s = jnp.einsum('bqkgd,bjkd->bkgqj', qcg, k_c)
        s_sink_c = jnp.broadcast_to(sink_logit[None, :, :, None, None], s.shape[:-1] + (1,))
        pc = jax.nn.softmax(jnp.concatenate([s, s_sink_c], axis=-1), axis=-1)
        out_c = jnp.einsum('bkgqj,bjkd->bqkgd', pc[..., :lc], v_c).reshape(b, lc, ATTN_W).astype(dtype)
    return out_x, out_c


def _centred_dwconv(x, w):
    return lax.conv_general_dilated(x, w[:, None, :], window_strides=(1,), padding='SAME',
                                    dimension_numbers=('NWC', 'WIO', 'NWC'), feature_group_count=x.shape[-1])


def _gated_delta_chunked(q, k, v, beta, g, s0):
    b, n, h, dk = q.shape
    dv = v.shape[-1]
    nc = n // DN_CHUNK

    def chunks(t):
        return jnp.moveaxis(t.reshape(b, nc, DN_CHUNK, h, -1), 3, 1)

    qc, kc, vc = chunks(q), chunks(k), chunks(v)
    bc = chunks(beta[..., None])[..., 0]
    gcum = jnp.cumsum(chunks(g[..., None])[..., 0], axis=-1)
    idx = jnp.arange(DN_CHUNK)
    incl = idx[:, None] >= idx[None, :]
    strict = idx[:, None] > idx[None, :]
    diff = gcum[..., :, None] - gcum[..., None, :]
    decay = jnp.where(incl, jnp.exp(jnp.where(incl, diff, 0.0)), 0.0)
    kb = kc * bc[..., None]
    m = jnp.where(strict, jnp.einsum('bhnid,bhnjd->bhnij', kb, kc) * decay, 0.0)
    rhs = jnp.concatenate([vc * bc[..., None], kb * jnp.exp(gcum)[..., None]], axis=-1)
    sol = lax.linalg.triangular_solve(m, rhs, left_side=True, lower=True, unit_diagonal=True)
    u, w = sol[..., :dv], sol[..., dv:]
    aqk = jnp.where(incl, jnp.einsum('bhnid,bhnjd->bhnij', qc, kc) * decay, 0.0)
    qg = qc * jnp.exp(gcum)[..., None]
    kg = kc * jnp.exp(gcum[..., -1:] - gcum)[..., None]
    glast = jnp.exp(gcum[..., -1])
    xs = tuple(jnp.moveaxis(t, 2, 0) for t in (qg, kg, u, w, aqk, glast))

    def step(s, inp):
        qg_i, kg_i, u_i, w_i, aqk_i, gl_i = inp
        v_new = u_i - jnp.einsum('bhcd,bhde->bhce', w_i, s)
        o_i = jnp.einsum('bhcd,bhde->bhce', qg_i, s) + jnp.einsum('bhcj,bhje->bhce', aqk_i, v_new)
        s = s * gl_i[..., None, None] + jnp.einsum('bhcd,bhce->bhde', kg_i, v_new)
        return s, o_i

    s_fin, o = lax.scan(step, s0, xs)
    o = jnp.moveaxis(jnp.moveaxis(o, 0, 2), 1, 3).reshape(b, n, h, dv)
    return o, s_fin


def _gated_deltanet_jnp(lat_in, ctx_in, conv_w, a_log, dt_bias, o_norm_w, ctx_out):
    def prep(q, k, v, beta, alpha):
        b, n, _ = q.shape
        qkv = jax.nn.silu(_centred_dwconv(jnp.concatenate([q, k, v], axis=-1), conv_w)).astype(F32)
        q, k, v = jnp.split(qkv, 3, axis=-1)
        q = _l2_norm(q.reshape(b, n, DN_HEADS, DN_HEAD_DIM)) * (DN_HEAD_DIM ** -0.5)
        k = _l2_norm(k.reshape(b, n, DN_HEADS, DN_HEAD_DIM))
        v = v.reshape(b, n, DN_HEADS, DN_HEAD_DIM)
        bt = jax.nn.sigmoid(beta.astype(F32)).reshape(b, n, 2, DN_HEADS)
        gl = -jnp.exp(a_log.astype(F32)) * jax.nn.softplus(
            alpha.astype(F32).reshape(b, n, 2, DN_HEADS) + dt_bias.astype(F32))
        return q, k, v, bt, gl

    qx, kx, vx, gx, betax, alphax = lat_in
    qc, kc, vc, gc, betac, alphac = ctx_in
    lat = prep(qx, kx, vx, betax, alphax)
    ctx = prep(qc, kc, vc, betac, alphac)
    b = qx.shape[0]
    o_x, o_c = 0.0, 0.0
    for d, rev in enumerate((False, True)):
        s0 = jnp.zeros((b, DN_HEADS, DN_HEAD_DIM, DN_HEAD_DIM), F32)
        oc_d, s_ctx = _gated_delta_chunked(_flip_seq(ctx[0], rev), _flip_seq(ctx[1], rev), _flip_seq(ctx[2], rev),
                                           _flip_seq(ctx[3][:, :, d], rev), _flip_seq(ctx[4][:, :, d], rev), s0)
        ox_d, _ = _gated_delta_chunked(_flip_seq(lat[0], rev), _flip_seq(lat[1], rev), _flip_seq(lat[2], rev),
                                       _flip_seq(lat[3][:, :, d], rev), _flip_seq(lat[4][:, :, d], rev), s_ctx)
        o_x = o_x + _flip_seq(ox_d, rev)
        o_c = o_c + _flip_seq(oc_d, rev)

    def gated_out(o, gate):
        bb, n = gate.shape[:2]
        gt = jax.nn.silu(gate.astype(F32).reshape(bb, n, DN_HEADS, DN_HEAD_DIM))
        return (_rms_norm(o, o_norm_w) * gt).reshape(bb, n, DN_W).astype(gate.dtype)

    return gated_out(o_x, gx), gated_out(o_c, gc)


def _s5_discretize(lam_re, lam_im, log_step, b_re, b_im):
    lr = jnp.minimum(lam_re, -1e-4)
    dt = jnp.exp(log_step)[:, None]
    mag = jnp.exp(lr * dt)
    ar, ai = mag * jnp.cos(lam_im * dt), mag * jnp.sin(lam_im * dt)
    den = lr * lr + lam_im * lam_im
    nr, ni = ar - 1.0, ai
    fr = (nr * lr + ni * lam_im) / den
    fi = (ni * lr - nr * lam_im) / den
    bbr = fr[..., None] * b_re - fi[..., None] * b_im
    bbi = fr[..., None] * b_im + fi[..., None] * b_re
    return ar, ai, bbr, bbi


def _s5_scan(bu_r, bu_i, ar, ai, h0_r, h0_i, reverse):
    pos = -1 if reverse else 0
    bu_r = bu_r.at[:, pos].add(ar * h0_r - ai * h0_i)
    bu_i = bu_i.at[:, pos].add(ar * h0_i + ai * h0_r)
    a_r = jnp.broadcast_to(ar, bu_r.shape)
    a_i = jnp.broadcast_to(ai, bu_i.shape)

    def combine(e1, e2):
        a1r, a1i, b1r, b1i = e1
        a2r, a2i, b2r, b2i = e2
        return (a2r * a1r - a2i * a1i, a2r * a1i + a2i * a1r,
                a2r * b1r - a2i * b1i + b2r, a2r * b1i + a2i * b1r + b2i)

    _, _, hr, hi = lax.associative_scan(combine, (a_r, a_i, bu_r, bu_i), reverse=reverse, axis=1)
    return hr, hi


def _s5_readout(hr, hi, c_re, c_im):
    return jnp.einsum('blgp,ghp->blgh', hr, c_re) - jnp.einsum('blgp,ghp->blgh', hi, c_im)


def _s5_mixer_jnp(u_x, u_c, lam_re, lam_im, log_step, b_re, b_im, c_re, c_im, d_skip, w_glu):
    b, n, _ = u_x.shape
    lc = u_c.shape[1]
    dtype = u_x.dtype
    ux = u_x.astype(F32).reshape(b, n, S5_GROUPS, S5_GROUP_CH)
    uc = u_c.astype(F32).reshape(b, lc, S5_GROUPS, S5_GROUP_CH)
    dsk = d_skip.astype(F32).reshape(S5_GROUPS, S5_GROUP_CH)
    yx, yc = ux * dsk, uc * dsk
    zero = jnp.zeros((b, S5_GROUPS, S5_STATE), F32)
    for d, rev in enumerate((False, True)):
        ar, ai, bbr, bbi = _s5_discretize(lam_re[d], lam_im[d], log_step[d], b_re[d], b_im[d])
        cr, ci = c_re[d], c_im[d]
        hcr, hci = _s5_scan(jnp.einsum('blgh,gph->blgp', uc, bbr), jnp.einsum('blgh,gph->blgp', uc, bbi),
                            ar, ai, zero, zero, rev)
        end = 0 if rev else -1
        hxr, hxi = _s5_scan(jnp.einsum('blgh,gph->blgp', ux, bbr), jnp.einsum('blgh,gph->blgp', ux, bbi),
                            ar, ai, hcr[:, end], hci[:, end], rev)
        yx = yx + _s5_readout(hxr, hxi, cr, ci)
        yc = yc + _s5_readout(hcr, hci, cr, ci)

    def glu(y, m):
        y = jax.nn.gelu(y.reshape(b, m, S5_W))
        return (y * jax.nn.sigmoid(y @ w_glu.astype(F32))).astype(dtype)

    return glu(yx, n), glu(yc, lc)


def _permute_w_in(w):
    main = w[:, :4352]
    gates = w[:, 4352:4376]
    s5 = w[:, 4376:4888]
    pad = jnp.zeros((w.shape[0], Z_W - Z_USED), w.dtype)
    return jnp.concatenate([main, s5, gates, pad], axis=1).astype(BF16)


def kernel(x, c, ctx, c_ctx, w_ada, b_ada, norm1_w, norm2_w, w_in, w_out, attn_q_norm, attn_k_norm, attn_sink,
           dn_conv, dn_a_log, dn_dt_bias, dn_o_norm, s5_lam_re, s5_lam_im, s5_log_step, s5_b_re, s5_b_im,
           s5_c_re, s5_c_im, s5_d, s5_w_glu, moe_w_grp, moe_b_grp, moe_w_rt, moe_b_rt, moe_w1, moe_w3, moe_w2):
    b, n, d = x.shape
    assert b == 1 and ctx.shape[1] == CTX_LEN and d == D_MODEL
    lc = CTX_LEN
    xa = jnp.concatenate([ctx[0], x[0]], axis=0)
    cvec = jnp.zeros((8, D_MODEL), F32).at[0].set(c_ctx).at[1].set(c[0])
    mods = adaln(cvec, w_ada, b_ada)
    for layer in range(DEPTH):
        mod = mods[layer, 0:2].reshape(2, 6, D_MODEL)
        mod = jnp.concatenate([mod, jnp.zeros((2, 2, D_MODEL), F32)], axis=1)
        z = in_proj(xa, norm1_w[layer], mod, _permute_w_in(w_in[layer]))

        def sl(a, bnd):
            return z[None, :lc, a:bnd], z[None, lc:, a:bnd]

        (qc, qx), (kc, kx), (vc, vx) = sl(Z_Q, Z_K), sl(Z_K, Z_V), sl(Z_V, Z_DNQ)
        ax, ac = _window_attention_jnp(qx, kx, vx, qc, kc, vc, attn_q_norm[layer], attn_k_norm[layer],
                                       attn_sink[layer], True)
        dn_c, dn_x = zip(sl(Z_DNQ, Z_DNK), sl(Z_DNK, Z_DNV), sl(Z_DNV, Z_DNG), sl(Z_DNG, Z_S5),
                         sl(Z_BETA, Z_ALPHA), sl(Z_ALPHA, Z_USED))
        bx, bc = _gated_deltanet_jnp(dn_x, dn_c, dn_conv[layer], dn_a_log[layer], dn_dt_bias[layer],
                                     dn_o_norm[layer], True)
        uc, ux = sl(Z_S5, Z_BETA)
        sx, sc = _s5_mixer_jnp(ux, uc, s5_lam_re[layer], s5_lam_im[layer], s5_log_step[layer], s5_b_re[layer],
                               s5_b_im[layer], s5_c_re[layer], s5_c_im[layer], s5_d[layer], s5_w_glu[layer])
        mix = jnp.concatenate([jnp.concatenate([ac, bc, sc], axis=-1)[0],
                               jnp.concatenate([ax, bx, sx], axis=-1)[0]], axis=0).astype(BF16)
        w_router = jnp.concatenate([moe_w_grp[layer], moe_w_rt[layer],
                                    jnp.zeros((D_MODEL, 128 - N_GROUPS - N_EXPERTS), F32)], axis=1)
        b_router = jnp.concatenate([moe_b_grp[layer], moe_b_rt[layer],
                                    jnp.zeros((128 - N_GROUPS - N_EXPERTS,), F32)]).reshape(1, 128)
        xa, h2, logits = out_proj(mix, xa, mod, w_out[layer].astype(BF16), norm2_w[layer], w_router, b_router)
        xa = moe_block(xa, h2, logits, mod,
                       moe_w1[layer].reshape(N_EXPERTS, D_MODEL, EXPERT_HIDDEN),
                       moe_w3[layer].reshape(N_EXPERTS, D_MODEL, EXPERT_HIDDEN),
                       moe_w2[layer].reshape(N_EXPERTS, EXPERT_HIDDEN, D_MODEL))
    return xa[lc:][None]
```

```python
import functools

import jax
import jax.numpy as jnp
from jax import lax
from jax.experimental import pallas as pl
from jax.experimental.pallas import tpu as pltpu

F32 = jnp.float32
BF16 = jnp.bfloat16
HIGHEST = lax.Precision.HIGHEST

D_MODEL = 2048
SEQ = 8192
DEPTH = 2
GRID_W = 64
CTX_LEN = 256
N_ALL = CTX_LEN + SEQ
EPS = 1e-6
NEG_INF = -1e30

ATTN_HEADS = 12
ATTN_KV_HEADS = 4
ATTN_HEAD_DIM = 64
ATTN_GROUP = ATTN_HEADS // ATTN_KV_HEADS
ATTN_W = ATTN_HEADS * ATTN_HEAD_DIM
ATTN_KV_W = ATTN_KV_HEADS * ATTN_HEAD_DIM
WINDOW = 128
ATTN_BLOCK = 128
ROPE_BASE = 10000.0
DN_HEADS = 6
DN_HEAD_DIM = 128
DN_W = DN_HEADS * DN_HEAD_DIM
DN_CONV = 5
DN_CHUNK = 64
S5_W = D_MODEL - ATTN_W - DN_W
S5_GROUP_CH = 16
S5_GROUPS = S5_W // S5_GROUP_CH
S5_STATE = 64
N_GROUPS = 4
EXPERTS_PER_GROUP = 8
N_EXPERTS = N_GROUPS * EXPERTS_PER_GROUP
TOP_K = 2
EXPERT_HIDDEN = 512

Z_DNQ, Z_DNK, Z_DNV, Z_DNG = 0, 768, 1536, 2304
Z_Q, Z_K, Z_V = 3072, 3840, 4096
Z_GATES = 4352
Z_BETA, Z_ALPHA = Z_GATES, Z_GATES + 2 * DN_HEADS
Z_S5 = 4608
Z_W = 5120

VMEM_LIMIT = 56 * 1024 * 1024


def _cparams(sem, vmem=VMEM_LIMIT):
    return pltpu.CompilerParams(dimension_semantics=sem, vmem_limit_bytes=vmem)


def _adaln_kernel(c_ref, w_ref, b_ref, o_ref):
    cv = c_ref[...]
    s = cv * jax.nn.sigmoid(cv)
    o_ref[0] = jnp.dot(s, w_ref[0], preferred_element_type=F32, precision=HIGHEST) + b_ref[0]


def adaln(cvec, w_ada, b_ada):
    L = w_ada.shape[0]
    tn = 1024
    n6 = 6 * D_MODEL
    return pl.pallas_call(
        _adaln_kernel,
        grid=(L, n6 // tn),
        in_specs=[
            pl.BlockSpec((8, D_MODEL), lambda l, j: (0, 0)),
            pl.BlockSpec((1, D_MODEL, tn), lambda l, j: (l, 0, j)),
            pl.BlockSpec((1, 1, tn), lambda l, j: (l, 0, j)),
        ],
        out_specs=pl.BlockSpec((1, 8, tn), lambda l, j: (l, 0, j)),
        out_shape=jax.ShapeDtypeStruct((L, 8, n6), F32),
        compiler_params=_cparams(("parallel", "parallel")),
        name="adaln",
    )(cvec, w_ada, b_ada.reshape(L, 1, n6))


def _row_is_ctx(tm):
    row = pl.program_id(0) * tm + lax.broadcasted_iota(jnp.int32, (tm, 1), 0)
    return row < CTX_LEN


def _norm_mod(x, nw, mod_ref, shift_i, scale_i, is_ctx):
    ms = jnp.mean(x * x, axis=-1, keepdims=True)
    h = x * lax.rsqrt(ms + EPS) * nw
    sc = jnp.where(is_ctx, mod_ref[0, scale_i:scale_i + 1, :], mod_ref[1, scale_i:scale_i + 1, :])
    sh = jnp.where(is_ctx, mod_ref[0, shift_i:shift_i + 1, :], mod_ref[1, shift_i:shift_i + 1, :])
    return h * (1.0 + sc) + sh


def _in_proj_kernel(x_ref, nw_ref, mod_ref, w_ref, o_ref, h_scr, *, tm):
    @pl.when(pl.program_id(1) == 0)
    def _():
        h = _norm_mod(x_ref[...], nw_ref[...], mod_ref, 0, 1, _row_is_ctx(tm))
        h_scr[...] = h.astype(BF16)

    o_ref[...] = jnp.dot(h_scr[...], w_ref[...], preferred_element_type=F32)


def in_proj(xa, norm_w, mod, w_in_p):
    tm, tn = 512, 1024
    n = xa.shape[0]
    return pl.pallas_call(
        functools.partial(_in_proj_kernel, tm=tm),
        grid=(pl.cdiv(n, tm), Z_W // tn),
        in_specs=[
            pl.BlockSpec((tm, D_MODEL), lambda i, j: (i, 0)),
            pl.BlockSpec((1, D_MODEL), lambda i, j: (0, 0)),
            pl.BlockSpec((2, 8, D_MODEL), lambda i, j: (0, 0, 0)),
            pl.BlockSpec((D_MODEL, tn), lambda i, j: (0, j)),
        ],
        out_specs=pl.BlockSpec((tm, tn), lambda i, j: (i, j)),
        out_shape=jax.ShapeDtypeStruct((n, Z_W), F32),
        scratch_shapes=[pltpu.VMEM((tm, D_MODEL), BF16)],
        compiler_params=_cparams(("parallel", "arbitrary")),
        name="in_proj",
    )(xa, norm_w.reshape(1, D_MODEL), mod, w_in_p)


def _out_proj_kernel(a_ref, b_ref, s_ref, x_ref, mod_ref, w_ref, nw_ref, wr_ref, br_ref, xo_ref, h_ref, lg_ref, *, tm):
    is_ctx = _row_is_ctx(tm)
    y = (jnp.dot(a_ref[...], w_ref[0:ATTN_W, :], preferred_element_type=F32)
         + jnp.dot(b_ref[...], w_ref[ATTN_W:ATTN_W + DN_W, :], preferred_element_type=F32)
         + jnp.dot(s_ref[...], w_ref[ATTN_W + DN_W:, :], preferred_element_type=F32))
    gate = jnp.where(is_ctx, mod_ref[0, 2:3, :], mod_ref[1, 2:3, :])
    xn = x_ref[...] + gate * y
    xo_ref[...] = xn
    h = _norm_mod(xn, nw_ref[...], mod_ref, 3, 4, is_ctx)
    h_ref[...] = h
    lg_ref[...] = jnp.dot(h, wr_ref[...], preferred_element_type=F32, precision=HIGHEST) + br_ref[...]


def out_proj(mix_a, mix_b, mix_s, xa, mod, w_out_b, norm2_w, w_router, b_router):
    tm = 256
    n = xa.shape[0]
    return pl.pallas_call(
        functools.partial(_out_proj_kernel, tm=tm),
        grid=(n // tm,),
        in_specs=[
            pl.BlockSpec((tm, ATTN_W), lambda i: (i, 0)),
            pl.BlockSpec((tm, DN_W), lambda i: (i, 0)),
            pl.BlockSpec((tm, S5_W), lambda i: (i, 0)),
            pl.BlockSpec((tm, D_MODEL), lambda i: (i, 0)),
            pl.BlockSpec((2, 8, D_MODEL), lambda i: (0, 0, 0)),
            pl.BlockSpec((D_MODEL, D_MODEL), lambda i: (0, 0)),
            pl.BlockSpec((1, D_MODEL), lambda i: (0, 0)),
            pl.BlockSpec((D_MODEL, 128), lambda i: (0, 0)),
            pl.BlockSpec((1, 128), lambda i: (0, 0)),
        ],
        out_specs=[
            pl.BlockSpec((tm, D_MODEL), lambda i: (i, 0)),
            pl.BlockSpec((tm, D_MODEL), lambda i: (i, 0)),
            pl.BlockSpec((tm, 128), lambda i: (i, 0)),
        ],
        out_shape=[
            jax.ShapeDtypeStruct((n, D_MODEL), F32),
            jax.ShapeDtypeStruct((n, D_MODEL), F32),
            jax.ShapeDtypeStruct((n, 128), F32),
        ],
        compiler_params=_cparams(("parallel",)),
        name="out_proj",
    )(mix_a, mix_b, mix_s, xa, mod, w_out_b, norm2_w.reshape(1, D_MODEL), w_router, b_router)


def _router_kernel(lg_ref, o_ref):
    lg = lg_ref[...]
    tm = lg.shape[0]
    lane = lax.broadcasted_iota(jnp.int32, (tm, 128), 1)
    is_g = lane < N_GROUPS
    gl = jnp.where(is_g, lg, NEG_INF)
    gmax = jnp.max(gl, axis=-1, keepdims=True)
    gidx = jnp.min(jnp.where((gl == gmax) & is_g, lane, 128), axis=-1, keepdims=True)
    gsum = jnp.sum(jnp.where(is_g, jnp.exp(gl - gmax), 0.0), axis=-1, keepdims=True)
    g_w = 1.0 / gsum
    e_lane = lane - N_GROUPS
    in_grp = (e_lane >= gidx * EXPERTS_PER_GROUP) & (e_lane < (gidx + 1) * EXPERTS_PER_GROUP)
    el = jnp.where(in_grp, lg, NEG_INF)
    v1 = jnp.max(el, axis=-1, keepdims=True)
    i1 = jnp.min(jnp.where((el == v1) & in_grp, e_lane, 128), axis=-1, keepdims=True)
    el2 = jnp.where(e_lane == i1, NEG_INF, el)
    in2 = in_grp & (e_lane != i1)
    v2 = jnp.max(el2, axis=-1, keepdims=True)
    i2 = jnp.min(jnp.where((el2 == v2) & in2, e_lane, 128), axis=-1, keepdims=True)
    e2 = jnp.exp(v2 - v1)
    w1 = g_w / (1.0 + e2)
    w2 = g_w * e2 / (1.0 + e2)
    out = jnp.where(lane == 0, i1.astype(F32), 0.0)
    out = jnp.where(lane == 1, i2.astype(F32), out)
    out = jnp.where(lane == 2, w1, out)
    out = jnp.where(lane == 3, w2, out)
    o_ref[...] = out


def router(logits):
    tm = 256
    n = logits.shape[0]
    return pl.pallas_call(
        _router_kernel,
        grid=(n // tm,),
        in_specs=[pl.BlockSpec((tm, 128), lambda i: (i, 0))],
        out_specs=pl.BlockSpec((tm, 128), lambda i: (i, 0)),
        out_shape=jax.ShapeDtypeStruct((n, 128), F32),
        compiler_params=_cparams(("parallel",)),
        name="router",
    )(logits)


MOE_TILE = 256


def _moe_tiles(n):
    return TOP_K * n // MOE_TILE + N_EXPERTS


def _expert_kernel(te_ref, src_ref, nt_ref, h_hbm, wrow_ref, w1_ref, w3_ref, w2_ref, y_ref,
                   xbuf, sem, w1b, w3b, w2b):
    i = pl.program_id(0)
    n_tiles = nt_ref[0]

    def gather(tile, slot, start):
        def body(r, c):
            tok = src_ref[tile * MOE_TILE + r]
            cp = pltpu.make_async_copy(h_hbm.at[pl.ds(tok, 1)], xbuf.at[slot, pl.ds(r, 1)], sem.at[slot])
            if start:
                cp.start()
            else:
                cp.wait()
            return c
        lax.fori_loop(0, MOE_TILE, body, 0)

    @pl.when(i == 0)
    def _():
        gather(0, 0, True)

    @pl.when(i + 1 < n_tiles)
    def _():
        gather(i + 1, (i + 1) % 2, True)

    @pl.when(i < n_tiles)
    def _():
        slot = i % 2
        gather(i, slot, False)
        prev_e = te_ref[jnp.maximum(i - 1, 0)]
        new_e = (i == 0) | (te_ref[i] != prev_e)

        @pl.when(new_e)
        def _():
            w1b[...] = w1_ref[0].astype(BF16)
            w3b[...] = w3_ref[0].astype(BF16)
            w2b[...] = w2_ref[0].astype(BF16)

        xt = xbuf[slot].astype(BF16)
        a = jnp.dot(xt, w1b[...], preferred_element_type=F32)
        u = jnp.dot(xt, w3b[...], preferred_element_type=F32)
        act = (a * jax.nn.sigmoid(a)) * u * wrow_ref[...]
        y_ref[...] = jnp.dot(act.astype(BF16), w2b[...], preferred_element_type=F32)

    @pl.when(i >= n_tiles)
    def _():
        y_ref[...] = jnp.zeros_like(y_ref)


def expert_mlp(tile_expert, src_tok, n_tiles, h2, wrow, w1, w3, w2):
    e_map = lambda i, te, src, nt: (te[i], 0, 0)
    moe_tiles = tile_expert.shape[0]
    grid_spec = pltpu.PrefetchScalarGridSpec(
        num_scalar_prefetch=3,
        grid=(moe_tiles,),
        in_specs=[
            pl.BlockSpec(memory_space=pl.ANY),
            pl.BlockSpec((MOE_TILE, 1), lambda i, te, src, nt: (i, 0)),
            pl.BlockSpec((1, D_MODEL, EXPERT_HIDDEN), e_map),
            pl.BlockSpec((1, D_MODEL, EXPERT_HIDDEN), e_map),
            pl.BlockSpec((1, EXPERT_HIDDEN, D_MODEL), e_map),
        ],
        out_specs=pl.BlockSpec((MOE_TILE, D_MODEL), lambda i, te, src, nt: (i, 0)),
        scratch_shapes=[
            pltpu.VMEM((2, MOE_TILE, D_MODEL), F32),
            pltpu.SemaphoreType.DMA((2,)),
            pltpu.VMEM((D_MODEL, EXPERT_HIDDEN), BF16),
            pltpu.VMEM((D_MODEL, EXPERT_HIDDEN), BF16),
            pltpu.VMEM((EXPERT_HIDDEN, D_MODEL), BF16),
        ],
    )
    return pl.pallas_call(
        _expert_kernel,
        grid_spec=grid_spec,
        out_shape=jax.ShapeDtypeStruct((moe_tiles * MOE_TILE, D_MODEL), F32),
        compiler_params=_cparams(("arbitrary",)),
        name="expert_mlp",
    )(tile_expert, src_tok, n_tiles, h2, wrow, w1, w3, w2)


CMB_TILE = 256


def _combine_kernel(pos_ref, y_hbm, x_ref, mod_ref, o_ref, ybuf, sem):
    i = pl.program_id(0)
    nt = pl.num_programs(0)

    def gather(tile, slot, start):
        def body(r, c):
            for k in range(TOP_K):
                p = pos_ref[(tile * CMB_TILE + r) * TOP_K + k]
                cp = pltpu.make_async_copy(y_hbm.at[pl.ds(p, 1)], ybuf.at[slot, k, pl.ds(r, 1)], sem.at[slot])
                if start:
                    cp.start()
                else:
                    cp.wait()
            return c
        lax.fori_loop(0, CMB_TILE, body, 0)

    @pl.when(i == 0)
    def _():
        gather(0, 0, True)

    @pl.when(i + 1 < nt)
    def _():
        gather(i + 1, (i + 1) % 2, True)

    slot = i % 2
    gather(i, slot, False)
    gate = jnp.where(_row_is_ctx(CMB_TILE), mod_ref[0, 5:6, :], mod_ref[1, 5:6, :])
    o_ref[...] = x_ref[...] + gate * (ybuf[slot, 0] + ybuf[slot, 1])


def moe_combine(pos, y_sorted, xa, mod):
    n = xa.shape[0]
    grid_spec = pltpu.PrefetchScalarGridSpec(
        num_scalar_prefetch=1,
        grid=(n // CMB_TILE,),
        in_specs=[
            pl.BlockSpec(memory_space=pl.ANY),
            pl.BlockSpec((CMB_TILE, D_MODEL), lambda i, pos: (i, 0)),
            pl.BlockSpec((2, 8, D_MODEL), lambda i, pos: (0, 0, 0)),
        ],
        out_specs=pl.BlockSpec((CMB_TILE, D_MODEL), lambda i, pos: (i, 0)),
        scratch_shapes=[
            pltpu.VMEM((2, TOP_K, CMB_TILE, D_MODEL), F32),
            pltpu.SemaphoreType.DMA((2,)),
        ],
    )
    return pl.pallas_call(
        _combine_kernel,
        grid_spec=grid_spec,
        out_shape=jax.ShapeDtypeStruct((n, D_MODEL), F32),
        compiler_params=_cparams(("arbitrary",)),
        name="moe_combine",
    )(pos, y_sorted, xa, mod)


def moe_dispatch_plan(route):
    n = route.shape[0]
    eid = route[:, 0:TOP_K].astype(jnp.int32).reshape(-1)
    wgt = route[:, TOP_K:2 * TOP_K].reshape(-1)
    p_total = eid.shape[0]
    order = jnp.argsort(eid, stable=True).astype(jnp.int32)
    e_sorted = eid[order]
    counts = jnp.zeros((N_EXPERTS,), jnp.int32).at[eid].add(1)
    tiles_per = (counts + MOE_TILE - 1) // MOE_TILE
    tile_off = jnp.cumsum(tiles_per) - tiles_per
    off = jnp.cumsum(counts) - counts
    rank = jnp.arange(p_total, dtype=jnp.int32) - off[e_sorted]
    dest = tile_off[e_sorted] * MOE_TILE + rank
    moe_tiles = _moe_tiles(n)
    n_rows = moe_tiles * MOE_TILE
    src_tok = jnp.zeros((n_rows,), jnp.int32).at[dest].set(order // TOP_K)
    wrow = jnp.zeros((n_rows,), F32).at[dest].set(wgt[order])
    pos = jnp.zeros((p_total,), jnp.int32).at[order].set(dest)
    n_tiles = jnp.sum(tiles_per).astype(jnp.int32)
    tile_ids = jnp.arange(moe_tiles, dtype=jnp.int32)
    tile_expert = jnp.sum((tile_ids[:, None] >= (tile_off + tiles_per)[None, :]).astype(jnp.int32), axis=1)
    tile_expert = jnp.minimum(tile_expert, N_EXPERTS - 1)
    return tile_expert, src_tok, n_tiles.reshape(1), wrow.reshape(n_rows, 1), pos


def moe_block(xa, h2, logits, mod, w1, w3, w2):
    route = router(logits)
    tile_expert, src_tok, n_tiles, wrow, pos = moe_dispatch_plan(route)
    y_sorted = expert_mlp(tile_expert, src_tok, n_tiles, h2, wrow, w1, w3, w2)
    return moe_combine(pos, y_sorted, xa, mod)


QK_W = ATTN_W + ATTN_KV_W
QKV_W = QK_W + ATTN_KV_W
ROPE_F = ATTN_HEAD_DIM // 4


def _norm_rope_heads(x, nw, cos, sin, out_ref, n_heads, scale):
    tm, width = x.shape
    xw = x * nw
    lane = lax.broadcasted_iota(jnp.int32, (tm, width), 1)
    odd = (lane // ROPE_F) % 2 == 1
    xs = jnp.where(odd, pltpu.roll(xw, ROPE_F, 1), pltpu.roll(xw, width - ROPE_F, 1))
    for h in range(n_heads):
        sl = slice(h * ATTN_HEAD_DIM, (h + 1) * ATTN_HEAD_DIM)
        xh = x[:, sl]
        inv = lax.rsqrt(jnp.mean(xh * xh, axis=-1, keepdims=True) + EPS) * scale
        out_ref[h] = ((xw[:, sl] * cos + xs[:, sl] * sin) * inv).astype(out_ref.dtype)


def _attn_prep_kernel(zq_ref, zk_ref, zv_ref, qw_ref, kw_ref, cos_ref, sin_ref, q_ref, k_ref, v_ref):
    cos = cos_ref[...]
    sin = sin_ref[...]
    _norm_rope_heads(zq_ref[...], qw_ref[...], cos, sin, q_ref, ATTN_HEADS, ATTN_HEAD_DIM ** -0.5)
    _norm_rope_heads(zk_ref[...], kw_ref[...], cos, sin, k_ref, ATTN_KV_HEADS, 1.0)
    for h in range(ATTN_KV_HEADS):
        v_ref[h] = zv_ref[:, h * ATTN_HEAD_DIM:(h + 1) * ATTN_HEAD_DIM].astype(BF16)


def attn_prep(z, q_norm_w, k_norm_w, cos_t, sin_t):
    tm = 256
    n = z.shape[0]
    qw = jnp.tile(q_norm_w, ATTN_HEADS).reshape(1, ATTN_W)
    kw = jnp.tile(k_norm_w, ATTN_KV_HEADS).reshape(1, ATTN_KV_W)
    hm = lambda h: pl.BlockSpec((h, tm, ATTN_HEAD_DIM), lambda i: (0, i, 0))
    sds = lambda h: jax.ShapeDtypeStruct((h, n, ATTN_HEAD_DIM), BF16)
    return pl.pallas_call(
        _attn_prep_kernel,
        grid=(n // tm,),
        in_specs=[
            pl.BlockSpec((tm, ATTN_W), lambda i: (i, Z_Q // ATTN_W)),
            pl.BlockSpec((tm, ATTN_KV_W), lambda i: (i, Z_K // ATTN_KV_W)),
            pl.BlockSpec((tm, ATTN_KV_W), lambda i: (i, Z_V // ATTN_KV_W)),
            pl.BlockSpec((1, ATTN_W), lambda i: (0, 0)),
            pl.BlockSpec((1, ATTN_KV_W), lambda i: (0, 0)),
            pl.BlockSpec((tm, ATTN_HEAD_DIM), lambda i: (i, 0)),
            pl.BlockSpec((tm, ATTN_HEAD_DIM), lambda i: (i, 0)),
        ],
        out_specs=[hm(ATTN_HEADS), hm(ATTN_KV_HEADS), hm(ATTN_KV_HEADS)],
        out_shape=[sds(ATTN_HEADS), sds(ATTN_KV_HEADS), sds(ATTN_KV_HEADS)],
        compiler_params=_cparams(("parallel",)),
        name="attn_prep",
    )(z, z, z, qw, kw, cos_t, sin_t)


def _attn_kernel(sink_ref, q_ref, kp_ref, ko_ref, kn_ref, kc_ref, vp_ref, vo_ref, vn_ref, vc_ref, o_ref, *, n_blocks):
    i = pl.program_id(0)
    ctx_blocks = CTX_LEN // ATTN_BLOCK
    B = ATTN_BLOCK
    rows = ATTN_GROUP * B
    ncol = 3 * B + CTX_LEN
    r = lax.broadcasted_iota(jnp.int32, (rows, ncol), 0) % B
    c = lax.broadcasted_iota(jnp.int32, (rows, ncol), 1)
    lo = jnp.where(i > ctx_blocks, 0, B)
    hi = jnp.where(i < n_blocks - 1, 3 * B, 2 * B)
    hi = jnp.where(i < ctx_blocks, 0, hi)
    band = (c >= r) & (c <= r + 2 * WINDOW) & (c >= lo) & (c < hi)
    mask = band | (c >= 3 * B)
    grp = lax.broadcasted_iota(jnp.int32, (rows, 1), 0) // B
    for kh in range(ATTN_KV_HEADS):
        q3 = jnp.concatenate([q_ref[kh * ATTN_GROUP + g] for g in range(ATTN_GROUP)], axis=0)
        kcat = jnp.concatenate([kp_ref[kh], ko_ref[kh], kn_ref[kh], kc_ref[kh]], axis=0)
        vcat = jnp.concatenate([vp_ref[kh], vo_ref[kh], vn_ref[kh], vc_ref[kh]], axis=0)
        s = lax.dot_general(q3, kcat, (((1,), (1,)), ((), ())), preferred_element_type=F32)
        s = jnp.where(mask, s, NEG_INF)
        sink = jnp.zeros((rows, 1), F32)
        for g in range(ATTN_GROUP):
            sink = jnp.where(grp == g, sink_ref[kh * ATTN_GROUP + g], sink)
        m = jnp.maximum(jnp.max(s, axis=-1, keepdims=True), sink)
        p = jnp.exp(s - m)
        den = jnp.sum(p, axis=-1, keepdims=True) + jnp.exp(sink - m)
        o = jnp.dot(p.astype(BF16), vcat, preferred_element_type=F32) / den
        for g in range(ATTN_GROUP):
            h = kh * ATTN_GROUP + g
            o_ref[:, h * ATTN_HEAD_DIM:(h + 1) * ATTN_HEAD_DIM] = o[g * B:(g + 1) * B].astype(o_ref.dtype)


def attention(qh, kh, vh, sink):
    n = qh.shape[1]
    B = ATTN_BLOCK
    nblk = n // B
    cb = CTX_LEN // B
    prev = lambda i: (0, jnp.clip(i - 1, cb, nblk - 1), 0)
    own = lambda i: (0, i, 0)
    nxt = lambda i: (0, jnp.clip(i + 1, cb, nblk - 1), 0)
    ctx = lambda i: (0, 0, 0)
    kv = lambda m: pl.BlockSpec((ATTN_KV_HEADS, B, ATTN_HEAD_DIM), m)
    kvc = pl.BlockSpec((ATTN_KV_HEADS, CTX_LEN, ATTN_HEAD_DIM), ctx)
    return pl.pallas_call(
        functools.partial(_attn_kernel, n_blocks=nblk),
        grid=(nblk,),
        in_specs=[
            pl.BlockSpec(memory_space=pltpu.SMEM),
            pl.BlockSpec((ATTN_HEADS, B, ATTN_HEAD_DIM), own),
            kv(prev), kv(own), kv(nxt), kvc,
            kv(prev), kv(own), kv(nxt), kvc,
        ],
        out_specs=pl.BlockSpec((B, ATTN_W), lambda i: (i, 0)),
        out_shape=jax.ShapeDtypeStruct((n, ATTN_W), BF16),
        compiler_params=_cparams(("parallel",)),
        name="attention",
    )(sink, qh, kh, kh, kh, kh, vh, vh, vh, vh)


def rope_tables(n_lat):
    rows = n_lat // GRID_W
    row = jnp.repeat(jnp.arange(rows), GRID_W).astype(F32)
    col = jnp.tile(jnp.arange(GRID_W), rows).astype(F32)
    inv = ROPE_BASE ** (-jnp.arange(ROPE_F, dtype=F32) / ROPE_F)
    ar, ac = row[:, None] * inv, col[:, None] * inv
    cos = jnp.concatenate([jnp.cos(ar), jnp.cos(ar), jnp.cos(ac), jnp.cos(ac)], axis=1)
    sin = jnp.concatenate([-jnp.sin(ar), jnp.sin(ar), -jnp.sin(ac), jnp.sin(ac)], axis=1)
    cos = jnp.concatenate([jnp.ones((CTX_LEN, ATTN_HEAD_DIM), F32), cos], axis=0)
    sin = jnp.concatenate([jnp.zeros((CTX_LEN, ATTN_HEAD_DIM), F32), sin], axis=0)
    return cos, sin


def attn_mixer(z, q_norm_w, k_norm_w, sink, cos_t, sin_t):
    qh, kh, vh = attn_prep(z, q_norm_w, k_norm_w, cos_t, sin_t)
    return attention(qh, kh, vh, sink)


DN_QKV = 3 * DN_W
DN_HALO = 8
DN_DH = 2 * DN_HEADS
DN_C = DN_CHUNK


def _dn_prep_kernel(zm_ref, zp_ref, zn_ref, zg_ref, cw_ref, al_ref, dtb_ref, q_ref, k_ref, v_ref, g_ref, *, tm, n_tiles):
    i = pl.program_id(0)
    ctx_tiles = CTX_LEN // tm
    has_prev = (i != 0) & (i != ctx_tiles)
    has_next = (i != ctx_tiles - 1) & (i != n_tiles - 1)
    prev = jnp.where(has_prev, zp_ref[...], 0.0)
    nxt = jnp.where(has_next, zn_ref[...], 0.0)
    xcat = jnp.concatenate([prev, zm_ref[...], nxt], axis=0)
    half = DN_CONV // 2
    acc = None
    for t in range(DN_CONV):
        off = DN_HALO - half + t
        term = xcat[off:off + tm, :] * cw_ref[t:t + 1, :]
        acc = term if acc is None else acc + term
    y = acc * jax.nn.sigmoid(acc)
    for h in range(DN_HEADS):
        qh = y[:, h * DN_HEAD_DIM:(h + 1) * DN_HEAD_DIM]
        kh = y[:, DN_W + h * DN_HEAD_DIM:DN_W + (h + 1) * DN_HEAD_DIM]
        q_ref[h] = qh * (lax.rsqrt(jnp.sum(qh * qh, axis=-1, keepdims=True) + EPS) * (DN_HEAD_DIM ** -0.5))
        k_ref[h] = kh * lax.rsqrt(jnp.sum(kh * kh, axis=-1, keepdims=True) + EPS)
        v_ref[h] = y[:, 2 * DN_W + h * DN_HEAD_DIM:2 * DN_W + (h + 1) * DN_HEAD_DIM]
    zg = zg_ref[...]
    lane = lax.broadcasted_iota(jnp.int32, zg.shape, 1)
    beta = jax.nn.sigmoid(zg)
    alpha = pltpu.roll(zg, 128 - DN_DH, 1)
    gl = -jnp.exp(al_ref[...]) * jax.nn.softplus(alpha + dtb_ref[...])
    g_ref[...] = jnp.where(lane < DN_DH, beta, pltpu.roll(gl, DN_DH, 1))


def dn_prep(z, conv_w, a_log, dt_bias):
    tm = 256
    n = z.shape[0]
    n_tiles = n // tm
    hb = tm // DN_HALO
    cw = jnp.zeros((8, DN_QKV), F32).at[0:DN_CONV].set(conv_w)
    pad_row = lambda t: jnp.zeros((1, 128), F32).at[0, 0:DN_DH].set(t.reshape(-1))
    hm = pl.BlockSpec((DN_HEADS, tm, DN_HEAD_DIM), lambda i: (0, i, 0))
    sds = jax.ShapeDtypeStruct((DN_HEADS, n, DN_HEAD_DIM), F32)
    return pl.pallas_call(
        functools.partial(_dn_prep_kernel, tm=tm, n_tiles=n_tiles),
        grid=(n_tiles,),
        in_specs=[
            pl.BlockSpec((tm, DN_QKV), lambda i: (i, 0)),
            pl.BlockSpec((DN_HALO, DN_QKV), lambda i: (jnp.maximum(i * hb - 1, 0), 0)),
            pl.BlockSpec((DN_HALO, DN_QKV), lambda i: (jnp.minimum((i + 1) * hb, n // DN_HALO - 1), 0)),
            pl.BlockSpec((tm, 128), lambda i: (i, Z_GATES // 128)),
            pl.BlockSpec((8, DN_QKV), lambda i: (0, 0)),
            pl.BlockSpec((1, 128), lambda i: (0, 0)),
            pl.BlockSpec((1, 128), lambda i: (0, 0)),
        ],
        out_specs=[hm, hm, hm, pl.BlockSpec((tm, 128), lambda i: (i, 0))],
        out_shape=[sds, sds, sds, jax.ShapeDtypeStruct((n, 128), F32)],
        compiler_params=_cparams(("parallel",)),
        name="dn_prep",
    )(z, z, z, z, cw, pad_row(a_log), pad_row(dt_bias))


def _bdot(a, b):
    return jnp.dot(a.astype(BF16), b.astype(BF16), preferred_element_type=F32)


def _unit_tri_inverse(nmat, row, col):
    eye = (row == col).astype(F32)
    same = lambda b: (row // b) == (col // b)
    d1 = jnp.where(same(8), nmat, 0.0)
    d2 = _bdot(d1, d1)
    d3 = _bdot(d1, d2)
    d4 = _bdot(d2, d2)
    x = eye + d1 + d2 + d3
    t = x + _bdot(x, d4)
    for b in (8, 16, 32):
        off = jnp.where(same(2 * b) & jnp.logical_not(same(b)), nmat, 0.0)
        t = t + _bdot(t, _bdot(off, t))
    return t


def _dn_chunk_kernel(q_ref, k_ref, v_ref, g_ref, u_ref, wq_ref, kgt_ref, aqk_ref, gl_ref):
    C = DN_C
    row = lax.broadcasted_iota(jnp.int32, (C, C), 0)
    col = lax.broadcasted_iota(jnp.int32, (C, C), 1)
    g_all = g_ref[...]
    lane = lax.broadcasted_iota(jnp.int32, (C, 128), 1)
    gpart = jnp.where((lane >= DN_DH) & (lane < 2 * DN_DH), g_all, 0.0)
    lower = (row >= col).astype(F32)
    upper = (row <= col).astype(F32)
    csum_f = jnp.dot(lower, gpart, preferred_element_type=F32, precision=HIGHEST)
    csum_r = jnp.dot(upper, gpart, preferred_element_type=F32, precision=HIGHEST)
    gc_all = jnp.where(lane < DN_DH + DN_HEADS, csum_f, csum_r)
    gc_t = jnp.transpose(gc_all)
    tot = jnp.sum(gpart, axis=0, keepdims=True)
    gl_ref[0] = jnp.broadcast_to(jnp.exp(tot), (8, 128))
    for d in range(2):
        incl = (row >= col) if d == 0 else (row <= col)
        strict = (row > col) if d == 0 else (row < col)
        for h in range(DN_HEADS):
            j = d * DN_HEADS + h
            q, k, v = q_ref[h], k_ref[h], v_ref[h]
            beta = g_all[:, j:j + 1]
            gc_col = gc_all[:, DN_DH + j:DN_DH + j + 1]
            gc_row = gc_t[DN_DH + j:DN_DH + j + 1, :]
            tot_j = tot[:, DN_DH + j:DN_DH + j + 1]
            decay = jnp.where(incl, jnp.exp(jnp.where(incl, gc_col - gc_row, 0.0)), 0.0)
            kb = k * beta
            kk = lax.dot_general(jnp.concatenate([kb, q], axis=0).astype(BF16), k.astype(BF16),
                                 (((1,), (1,)), ((), ())), preferred_element_type=F32)
            m = jnp.where(strict, kk[:C] * decay, 0.0)
            aqk = jnp.where(incl, kk[C:] * decay, 0.0)
            egc = jnp.exp(gc_col)
            rhs = jnp.concatenate([v * beta, kb * egc], axis=1)
            sol = _bdot(_unit_tri_inverse(-m, row, col), rhs)
            u_ref[j] = sol[:, :DN_HEAD_DIM]
            wq_ref[j] = jnp.concatenate([sol[:, DN_HEAD_DIM:], q * egc], axis=0).astype(BF16)
            kgt_ref[j] = jnp.transpose(k * jnp.exp(tot_j - gc_col)).astype(BF16)
            aqk_ref[j] = aqk.astype(BF16)


def dn_chunk(qn, kn, vn, gates):
    n = qn.shape[1]
    nc = n // DN_C
    hm = pl.BlockSpec((DN_HEADS, DN_C, DN_HEAD_DIM), lambda c: (0, c, 0))
    sds = jax.ShapeDtypeStruct
    return pl.pallas_call(
        _dn_chunk_kernel,
        grid=(nc,),
        in_specs=[hm, hm, hm, pl.BlockSpec((DN_C, 128), lambda c: (c, 0))],
        out_specs=[
            pl.BlockSpec((DN_DH, DN_C, DN_HEAD_DIM), lambda c: (0, c, 0)),
            pl.BlockSpec((DN_DH, 2 * DN_C, DN_HEAD_DIM), lambda c: (0, c, 0)),
            pl.BlockSpec((DN_DH, DN_HEAD_DIM, DN_C), lambda c: (0, c, 0)),
            pl.BlockSpec((DN_DH, DN_C, DN_C), lambda c: (0, c, 0)),
            pl.BlockSpec((1, 8, 128), lambda c: (c, 0, 0)),
        ],
        out_shape=[
            sds((DN_DH, n, DN_HEAD_DIM), F32),
            sds((DN_DH, 2 * n, DN_HEAD_DIM), BF16),
            sds((DN_DH, nc * DN_HEAD_DIM, DN_C), BF16),
            sds((DN_DH, n, DN_C), BF16),
            sds((nc, 8, 128), F32),
        ],
        compiler_params=_cparams(("parallel",)),
        name="dn_chunk",
    )(qn, kn, vn, gates)


def _dn_scan_kernel(uf_ref, wqf_ref, kgf_ref, aqf_ref, glf_ref, ur_ref, wqr_ref, kgr_ref, aqr_ref, glr_ref,
                    of_ref, or_ref, state):
    @pl.when(pl.program_id(0) == 0)
    def _():
        state[...] = jnp.zeros_like(state)

    C = DN_C
    sets = ((uf_ref, wqf_ref, kgf_ref, aqf_ref, glf_ref, of_ref), (ur_ref, wqr_ref, kgr_ref, aqr_ref, glr_ref, or_ref))
    for d, (u_ref, wq_ref, kg_ref, aq_ref, gl_ref, o_ref) in enumerate(sets):
        gl_row = gl_ref[0][0:1, :]
        for h in range(DN_HEADS):
            j = d * DN_HEADS + h
            s = state[j]
            t = jnp.dot(wq_ref[h], s.astype(BF16), preferred_element_type=F32)
            v_new = u_ref[h] - t[:C]
            vb = v_new.astype(BF16)
            o_ref[h] = t[C:] + jnp.dot(aq_ref[h], vb, preferred_element_type=F32)
            state[j] = s * gl_row[:, DN_DH + j:DN_DH + j + 1] + jnp.dot(kg_ref[h], vb, preferred_element_type=F32)


def dn_scan(u, wq, kgt, aqk, gl):
    n = u.shape[1]
    nc = n // DN_C
    cc = CTX_LEN // DN_C
    fwd = lambda s: s
    rev = lambda s: jnp.where(s < cc, cc - 1 - s, nc + cc - 1 - s)
    specs = []
    for d, cm in enumerate((fwd, rev)):
        specs += [
            pl.BlockSpec((DN_HEADS, DN_C, DN_HEAD_DIM), lambda s, d=d, cm=cm: (d, cm(s), 0)),
            pl.BlockSpec((DN_HEADS, 2 * DN_C, DN_HEAD_DIM), lambda s, d=d, cm=cm: (d, cm(s), 0)),
            pl.BlockSpec((DN_HEADS, DN_HEAD_DIM, DN_C), lambda s, d=d, cm=cm: (d, cm(s), 0)),
            pl.BlockSpec((DN_HEADS, DN_C, DN_C), lambda s, d=d, cm=cm: (d, cm(s), 0)),
            pl.BlockSpec((1, 8, 128), lambda s, cm=cm: (cm(s), 0, 0)),
        ]
    osd = jax.ShapeDtypeStruct((DN_HEADS, n, DN_HEAD_DIM), F32)
    return pl.pallas_call(
        _dn_scan_kernel,
        grid=(nc,),
        in_specs=specs,
        out_specs=[pl.BlockSpec((DN_HEADS, DN_C, DN_HEAD_DIM), lambda s: (0, fwd(s), 0)),
                   pl.BlockSpec((DN_HEADS, DN_C, DN_HEAD_DIM), lambda s: (0, rev(s), 0))],
        out_shape=[osd, osd],
        scratch_shapes=[pltpu.VMEM((DN_DH, DN_HEAD_DIM, DN_HEAD_DIM), F32)],
        compiler_params=_cparams(("arbitrary",)),
        name="dn_scan",
    )(u, wq, kgt, aqk, gl, u, wq, kgt, aqk, gl)


def _dn_out_kernel(of_ref, or_ref, zg_ref, nw_ref, o_ref):
    for h in range(DN_HEADS):
        o = of_ref[h] + or_ref[h]
        o = o * lax.rsqrt(jnp.mean(o * o, axis=-1, keepdims=True) + EPS) * nw_ref[...]
        gate = zg_ref[:, h * DN_HEAD_DIM:(h + 1) * DN_HEAD_DIM]
        o_ref[:, h * DN_HEAD_DIM:(h + 1) * DN_HEAD_DIM] = (o * (gate * jax.nn.sigmoid(gate))).astype(o_ref.dtype)


def dn_out(o_f, o_r, z, o_norm_w):
    tm = 256
    n = z.shape[0]
    hm = pl.BlockSpec((DN_HEADS, tm, DN_HEAD_DIM), lambda i: (0, i, 0))
    return pl.pallas_call(
        _dn_out_kernel,
        grid=(n // tm,),
        in_specs=[hm, hm, pl.BlockSpec((tm, DN_W), lambda i: (i, Z_DNG // DN_W)),
                  pl.BlockSpec((1, DN_HEAD_DIM), lambda i: (0, 0))],
        out_specs=pl.BlockSpec((tm, DN_W), lambda i: (i, 0)),
        out_shape=jax.ShapeDtypeStruct((n, DN_W), BF16),
        compiler_params=_cparams(("parallel",)),
        name="dn_out",
    )(o_f, o_r, z, o_norm_w.reshape(1, DN_HEAD_DIM))


def dn_mixer(z, conv_w, a_log, dt_bias, o_norm_w):
    qn, kn, vn, gates = dn_prep(z, conv_w, a_log, dt_bias)
    u, wq, kgt, aqk, gl = dn_chunk(qn, kn, vn, gates)
    o_f, o_r = dn_scan(u, wq, kgt, aqk, gl)
    return dn_out(o_f, o_r, z, o_norm_w)


S5_SUB = 8
S5_BLK = S5_GROUPS * S5_STATE // S5_SUB
S5_UB = S5_W // S5_SUB
S5_T = 64


def _s5_param_kernel(lr_ref, li_ref, ls_ref, br_ref, bi_ref, ar_ref, ai_ref, bbr_ref, bbi_ref):
    lr = jnp.minimum(lr_ref[...], -1e-4)
    li = li_ref[...]
    dt = jnp.exp(ls_ref[...])
    mag = jnp.exp(lr * dt)
    ar = mag * jnp.cos(li * dt)
    ai = mag * jnp.sin(li * dt)
    den = lr * lr + li * li
    nr, ni = ar - 1.0, ai
    fr = (nr * lr + ni * li) / den
    fi = (ni * lr - nr * li) / den
    ar_ref[...] = ar
    ai_ref[...] = ai
    for h in range(S5_GROUP_CH):
        bbr_ref[h] = fr * br_ref[h] - fi * bi_ref[h]
        bbi_ref[h] = fr * bi_ref[h] + fi * br_ref[h]


def s5_params(lam_re, lam_im, log_step, b_re, b_im):
    r = 2 * S5_GROUPS
    ls = jnp.broadcast_to(log_step.reshape(r, 1), (r, S5_STATE))
    bt = lambda t: jnp.transpose(t.reshape(r, S5_STATE, S5_GROUP_CH), (2, 0, 1))
    sds = jax.ShapeDtypeStruct
    return pl.pallas_call(
        _s5_param_kernel,
        out_shape=[sds((r, S5_STATE), F32), sds((r, S5_STATE), F32),
                   sds((S5_GROUP_CH, r, S5_STATE), F32), sds((S5_GROUP_CH, r, S5_STATE), F32)],
        name="s5_params",
    )(lam_re.reshape(r, S5_STATE), lam_im.reshape(r, S5_STATE), ls, bt(b_re), bt(b_im))


def _s5_scan_kernel(uf_ref, ur_ref, a_ref, rb_ref, cc_ref, yf_ref, yr_ref, bu, hh, st):
    rows = S5_T * S5_SUB

    @pl.when(pl.program_id(0) == 0)
    def _():
        st[...] = jnp.zeros_like(st)

    sub = lax.broadcasted_iota(jnp.int32, (rows, S5_W), 0) % S5_SUB
    blk = lax.broadcasted_iota(jnp.int32, (rows, S5_W), 1) // S5_UB
    own = sub == blk
    for d, u_ref in enumerate((uf_ref, ur_ref)):
        u8 = u_ref[...]
        lhs = jnp.where(own, jnp.concatenate([u8] * S5_SUB, axis=1), 0.0).astype(BF16)
        bu[d, 0] = jnp.dot(lhs, rb_ref[d, 0], preferred_element_type=F32)
        bu[d, 1] = jnp.dot(lhs, rb_ref[d, 1], preferred_element_type=F32)

    a = [[a_ref[d, c] for c in range(2)] for d in range(2)]

    def body(t, carry):
        fr, fi, rr, ri = carry
        rf = pl.multiple_of(t * S5_SUB, S5_SUB)
        rv = pl.multiple_of((S5_T - 1 - t) * S5_SUB, S5_SUB)
        nfr = a[0][0] * fr - a[0][1] * fi + bu[0, 0, pl.ds(rf, S5_SUB), :]
        nfi = a[0][0] * fi + a[0][1] * fr + bu[0, 1, pl.ds(rf, S5_SUB), :]
        nrr = a[1][0] * rr - a[1][1] * ri + bu[1, 0, pl.ds(rv, S5_SUB), :]
        nri = a[1][0] * ri + a[1][1] * rr + bu[1, 1, pl.ds(rv, S5_SUB), :]
        hh[0, 0, pl.ds(rf, S5_SUB), :] = nfr
        hh[0, 1, pl.ds(rf, S5_SUB), :] = nfi
        hh[1, 0, pl.ds(rv, S5_SUB), :] = nrr
        hh[1, 1, pl.ds(rv, S5_SUB), :] = nri
        return nfr, nfi, nrr, nri

    fin = lax.fori_loop(0, S5_T, body, (st[0, 0], st[0, 1], st[1, 0], st[1, 1]), unroll=8)
    st[0, 0], st[0, 1], st[1, 0], st[1, 1] = fin

    for d, y_ref in enumerate((yf_ref, yr_ref)):
        ye = (jnp.dot(hh[d, 0].astype(BF16), cc_ref[d, 0], preferred_element_type=F32)
              + jnp.dot(hh[d, 1].astype(BF16), cc_ref[d, 1], preferred_element_type=F32))
        ye = jnp.where(own, ye, 0.0)
        acc = ye[:, 0:S5_UB]
        for s in range(1, S5_SUB):
            acc = acc + ye[:, s * S5_UB:(s + 1) * S5_UB]
        y_ref[...] = acc


def s5_scan(u8, a8, rb, cc):
    rows = S5_T * S5_SUB
    nt = u8.shape[0] // rows
    ct = CTX_LEN // S5_T

    def rev_map(i):
        return (jnp.where(i < ct, ct - 1 - i, nt + ct - 1 - i), 0)

    full = lambda shape: pl.BlockSpec(shape, lambda i: (0,) * len(shape))
    sds = jax.ShapeDtypeStruct(u8.shape, F32)
    return pl.pallas_call(
        _s5_scan_kernel,
        grid=(nt,),
        in_specs=[
            pl.BlockSpec((rows, S5_UB), lambda i: (i, 0)),
            pl.BlockSpec((rows, S5_UB), rev_map),
            full((2, 2, S5_SUB, S5_BLK)),
            full((2, 2, S5_W, S5_BLK)),
            full((2, 2, S5_BLK, S5_W)),
        ],
        out_specs=[pl.BlockSpec((rows, S5_UB), lambda i: (i, 0)), pl.BlockSpec((rows, S5_UB), rev_map)],
        out_shape=[sds, sds],
        scratch_shapes=[pltpu.VMEM((2, 2, rows, S5_BLK), F32), pltpu.VMEM((2, 2, rows, S5_BLK), F32),
                        pltpu.VMEM((2, 2, S5_SUB, S5_BLK), F32)],
        compiler_params=_cparams(("arbitrary",)),
        name="s5_scan",
    )(u8, u8, a8, rb, cc)


def _s5_glu_kernel(u_ref, yf_ref, yr_ref, d_ref, w_ref, o_ref):
    y = u_ref[...] * d_ref[...] + yf_ref[...] + yr_ref[...]
    y = jax.nn.gelu(y)
    gate = jnp.dot(y.astype(BF16), w_ref[...], preferred_element_type=F32)
    o_ref[...] = (y * jax.nn.sigmoid(gate)).astype(o_ref.dtype)


def s5_glu(z, yf, yr, d_skip, w_glu_b):
    n = z.shape[0]
    tm = 1056 if n % 1056 == 0 else 256
    row = lambda i: (i, 0)
    return pl.pallas_call(
        _s5_glu_kernel,
        grid=(n // tm,),
        in_specs=[
            pl.BlockSpec((tm, S5_W), lambda i: (i, Z_S5 // S5_W)),
            pl.BlockSpec((tm, S5_W), row),
            pl.BlockSpec((tm, S5_W), row),
            pl.BlockSpec((1, S5_W), lambda i: (0, 0)),
            pl.BlockSpec((S5_W, S5_W), lambda i: (0, 0)),
        ],
        out_specs=pl.BlockSpec((tm, S5_W), row),
        out_shape=jax.ShapeDtypeStruct((n, S5_W), BF16),
        compiler_params=_cparams(("parallel",)),
        name="s5_glu",
    )(z, yf, yr, d_skip.reshape(1, S5_W), w_glu_b)


def s5_mixer(z, lam_re, lam_im, log_step, b_re, b_im, c_re, c_im, d_skip, w_glu):
    n = z.shape[0]
    ar, ai, bbr, bbi = s5_params(lam_re, lam_im, log_step, b_re, b_im)
    g_blk = jax.nn.one_hot(jnp.arange(S5_GROUPS) % (S5_GROUPS // S5_SUB), S5_GROUPS // S5_SUB, dtype=F32)

    def place_b(bb):
        bb = jnp.transpose(bb.reshape(S5_GROUP_CH, 2, S5_GROUPS, S5_STATE), (1, 2, 0, 3))
        return jnp.einsum('dghp,gj->dghjp', bb, g_blk).reshape(2, S5_W, S5_BLK)

    def place_c(cm):
        return jnp.einsum('dghp,gj->djpgh', cm, g_blk).reshape(2, S5_BLK, S5_W)

    rb = jnp.stack([place_b(bbr), place_b(bbi)], axis=1).astype(BF16)
    cc = jnp.stack([place_c(c_re), -place_c(c_im)], axis=1).astype(BF16)
    a8 = jnp.stack([ar.reshape(2, S5_SUB, S5_BLK), ai.reshape(2, S5_SUB, S5_BLK)], axis=1)
    u8 = z[:, Z_S5:Z_S5 + S5_W].reshape(n * S5_SUB, S5_UB)
    yf8, yr8 = s5_scan(u8, a8, rb, cc)
    return s5_glu(z, yf8.reshape(n, S5_W), yr8.reshape(n, S5_W), d_skip, w_glu.astype(BF16))


def _permute_w_in(w):
    n_gate = 4 * DN_HEADS
    attn = w[:, :QKV_W]
    dn = w[:, QKV_W:QKV_W + 4 * DN_W]
    gates = w[:, QKV_W + 4 * DN_W:QKV_W + 4 * DN_W + n_gate]
    s5 = w[:, QKV_W + 4 * DN_W + n_gate:]
    pad = jnp.zeros((w.shape[0], Z_S5 - Z_GATES - n_gate), w.dtype)
    return jnp.concatenate([dn, attn, gates, pad, s5], axis=1).astype(BF16)


def kernel(x, c, ctx, c_ctx, w_ada, b_ada, norm1_w, norm2_w, w_in, w_out, attn_q_norm, attn_k_norm, attn_sink,
           dn_conv, dn_a_log, dn_dt_bias, dn_o_norm, s5_lam_re, s5_lam_im, s5_log_step, s5_b_re, s5_b_im,
           s5_c_re, s5_c_im, s5_d, s5_w_glu, moe_w_grp, moe_b_grp, moe_w_rt, moe_b_rt, moe_w1, moe_w3, moe_w2):
    b, n, d = x.shape
    assert b == 1 and ctx.shape[1] == CTX_LEN and d == D_MODEL
    lc = CTX_LEN
    xa = jnp.concatenate([ctx[0], x[0]], axis=0)
    cvec = jnp.zeros((8, D_MODEL), F32).at[0].set(c_ctx).at[1].set(c[0])
    mods = adaln(cvec, w_ada, b_ada)
    cos_t, sin_t = rope_tables(n)
    for layer in range(DEPTH):
        mod = mods[layer, 0:2].reshape(2, 6, D_MODEL)
        mod = jnp.concatenate([mod, jnp.zeros((2, 2, D_MODEL), F32)], axis=1)
        z = in_proj(xa, norm1_w[layer], mod, _permute_w_in(w_in[layer]))
        mix_a = attn_mixer(z, attn_q_norm[layer], attn_k_norm[layer], attn_sink[layer], cos_t, sin_t)
        mix_b = dn_mixer(z, dn_conv[layer], dn_a_log[layer], dn_dt_bias[layer], dn_o_norm[layer])
        mix_s = s5_mixer(z, s5_lam_re[layer], s5_lam_im[layer], s5_log_step[layer], s5_b_re[layer],
                         s5_b_im[layer], s5_c_re[layer], s5_c_im[layer], s5_d[layer], s5_w_glu[layer])
        w_router = jnp.concatenate([moe_w_grp[layer], moe_w_rt[layer],
                                    jnp.zeros((D_MODEL, 128 - N_GROUPS - N_EXPERTS), F32)], axis=1)
        b_router = jnp.concatenate([moe_b_grp[layer], moe_b_rt[layer],
                                    jnp.zeros((128 - N_GROUPS - N_EXPERTS,), F32)]).reshape(1, 128)
        xa, h2, logits = out_proj(mix_a, mix_b, mix_s, xa, mod, w_out[layer].astype(BF16), norm2_w[layer],
                                  w_router, b_router)
        xa = moe_block(xa, h2, logits, mod,
                       moe_w1[layer].reshape(N_EXPERTS, D_MODEL, EXPERT_HIDDEN),
                       moe_w3[layer].reshape(N_EXPERTS, D_MODEL, EXPERT_HIDDEN),
                       moe_w2[layer].reshape(N_EXPERTS, EXPERT_HIDDEN, D_MODEL))
    return xa[lc:][None]
```

```python
import functools

import jax
import jax.numpy as jnp
from jax import lax
from jax.experimental import pallas as pl
from jax.experimental.pallas import tpu as pltpu

F32 = jnp.float32
BF16 = jnp.bfloat16
HIGHEST = lax.Precision.HIGHEST

D_MODEL = 2048
SEQ = 8192
DEPTH = 2
GRID_W = 64
CTX_LEN = 256
N_ALL = CTX_LEN + SEQ
EPS = 1e-6
NEG_INF = -1e30

ATTN_HEADS = 12
ATTN_KV_HEADS = 4
ATTN_HEAD_DIM = 64
ATTN_GROUP = ATTN_HEADS // ATTN_KV_HEADS
ATTN_W = ATTN_HEADS * ATTN_HEAD_DIM
ATTN_KV_W = ATTN_KV_HEADS * ATTN_HEAD_DIM
WINDOW = 128
ATTN_BLOCK = 128
ROPE_BASE = 10000.0
DN_HEADS = 6
DN_HEAD_DIM = 128
DN_W = DN_HEADS * DN_HEAD_DIM
DN_CONV = 5
DN_CHUNK = 64
S5_W = D_MODEL - ATTN_W - DN_W
S5_GROUP_CH = 16
S5_GROUPS = S5_W // S5_GROUP_CH
S5_STATE = 64
N_GROUPS = 4
EXPERTS_PER_GROUP = 8
N_EXPERTS = N_GROUPS * EXPERTS_PER_GROUP
TOP_K = 2
EXPERT_HIDDEN = 512

Z_DNQ, Z_DNK, Z_DNV, Z_DNG = 0, 768, 1536, 2304
Z_Q, Z_K, Z_V = 3072, 3840, 4096
Z_GATES = 4352
Z_BETA, Z_ALPHA = Z_GATES, Z_GATES + 2 * DN_HEADS
Z_S5 = 4608
Z_W = 5120

VMEM_LIMIT = 56 * 1024 * 1024


def _cparams(sem, vmem=VMEM_LIMIT):
    return pltpu.CompilerParams(dimension_semantics=sem, vmem_limit_bytes=vmem)


def _adaln_kernel(c_ref, w_ref, b_ref, o_ref):
    cv = c_ref[...]
    s = cv * jax.nn.sigmoid(cv)
    o_ref[0] = jnp.dot(s, w_ref[0], preferred_element_type=F32, precision=HIGHEST) + b_ref[0]


def adaln(cvec, w_ada, b_ada):
    L = w_ada.shape[0]
    tn = 1024
    n6 = 6 * D_MODEL
    return pl.pallas_call(
        _adaln_kernel,
        grid=(L, n6 // tn),
        in_specs=[
            pl.BlockSpec((8, D_MODEL), lambda l, j: (0, 0)),
            pl.BlockSpec((1, D_MODEL, tn), lambda l, j: (l, 0, j)),
            pl.BlockSpec((1, 1, tn), lambda l, j: (l, 0, j)),
        ],
        out_specs=pl.BlockSpec((1, 8, tn), lambda l, j: (l, 0, j)),
        out_shape=jax.ShapeDtypeStruct((L, 8, n6), F32),
        compiler_params=_cparams(("parallel", "parallel")),
        name="adaln",
    )(cvec, w_ada, b_ada.reshape(L, 1, n6))


def _row_is_ctx(tm):
    row = pl.program_id(0) * tm + lax.broadcasted_iota(jnp.int32, (tm, 1), 0)
    return row < CTX_LEN


def _norm_mod(x, nw, mod_ref, shift_i, scale_i, is_ctx):
    ms = jnp.mean(x * x, axis=-1, keepdims=True)
    h = x * lax.rsqrt(ms + EPS) * nw
    sc = jnp.where(is_ctx, mod_ref[0, scale_i:scale_i + 1, :], mod_ref[1, scale_i:scale_i + 1, :])
    sh = jnp.where(is_ctx, mod_ref[0, shift_i:shift_i + 1, :], mod_ref[1, shift_i:shift_i + 1, :])
    return h * (1.0 + sc) + sh


def _in_proj_kernel(x_ref, nw_ref, mod_ref, w_ref, o_ref, h_scr, *, tm):
    @pl.when(pl.program_id(1) == 0)
    def _():
        h = _norm_mod(x_ref[...], nw_ref[...], mod_ref, 0, 1, _row_is_ctx(tm))
        h_scr[...] = h.astype(BF16)

    o_ref[...] = jnp.dot(h_scr[...], w_ref[...], preferred_element_type=F32)


def in_proj(xa, norm_w, mod, w_in_p):
    tm, tn = 512, 1024
    n = xa.shape[0]
    return pl.pallas_call(
        functools.partial(_in_proj_kernel, tm=tm),
        grid=(pl.cdiv(n, tm), Z_W // tn),
        in_specs=[
            pl.BlockSpec((tm, D_MODEL), lambda i, j: (i, 0)),
            pl.BlockSpec((1, D_MODEL), lambda i, j: (0, 0)),
            pl.BlockSpec((2, 8, D_MODEL), lambda i, j: (0, 0, 0)),
            pl.BlockSpec((D_MODEL, tn), lambda i, j: (0, j)),
        ],
        out_specs=pl.BlockSpec((tm, tn), lambda i, j: (i, j)),
        out_shape=jax.ShapeDtypeStruct((n, Z_W), F32),
        scratch_shapes=[pltpu.VMEM((tm, D_MODEL), BF16)],
        compiler_params=_cparams(("parallel", "arbitrary")),
        name="in_proj",
    )(xa, norm_w.reshape(1, D_MODEL), mod, w_in_p)


def _out_proj_kernel(a_ref, b_ref, s_ref, x_ref, mod_ref, w_ref, nw_ref, wr_ref, br_ref, xo_ref, h_ref, lg_ref, *, tm):
    is_ctx = _row_is_ctx(tm)
    y = (jnp.dot(a_ref[...], w_ref[0:ATTN_W, :], preferred_element_type=F32)
         + jnp.dot(b_ref[...], w_ref[ATTN_W:ATTN_W + DN_W, :], preferred_element_type=F32)
         + jnp.dot(s_ref[...], w_ref[ATTN_W + DN_W:, :], preferred_element_type=F32))
    gate = jnp.where(is_ctx, mod_ref[0, 2:3, :], mod_ref[1, 2:3, :])
    xn = x_ref[...] + gate * y
    xo_ref[...] = xn
    h = _norm_mod(xn, nw_ref[...], mod_ref, 3, 4, is_ctx)
    h_ref[...] = h
    wr = wr_ref[...]
    h_hi = h.astype(BF16)
    h_lo = (h - h_hi.astype(F32)).astype(BF16)
    w_hi = wr.astype(BF16)
    w_lo = (wr - w_hi.astype(F32)).astype(BF16)
    lg_ref[...] = (jnp.dot(h_hi, w_hi, preferred_element_type=F32) + jnp.dot(h_hi, w_lo, preferred_element_type=F32)
                   + jnp.dot(h_lo, w_hi, preferred_element_type=F32) + br_ref[...])


def out_proj(mix_a, mix_b, mix_s, xa, mod, w_out_b, norm2_w, w_router, b_router):
    tm = 256
    n = xa.shape[0]
    return pl.pallas_call(
        functools.partial(_out_proj_kernel, tm=tm),
        grid=(n // tm,),
        in_specs=[
            pl.BlockSpec((tm, ATTN_W), lambda i: (i, 0)),
            pl.BlockSpec((tm, DN_W), lambda i: (i, 0)),
            pl.BlockSpec((tm, S5_W), lambda i: (i, 0)),
            pl.BlockSpec((tm, D_MODEL), lambda i: (i, 0)),
            pl.BlockSpec((2, 8, D_MODEL), lambda i: (0, 0, 0)),
            pl.BlockSpec((D_MODEL, D_MODEL), lambda i: (0, 0)),
            pl.BlockSpec((1, D_MODEL), lambda i: (0, 0)),
            pl.BlockSpec((D_MODEL, 128), lambda i: (0, 0)),
            pl.BlockSpec((1, 128), lambda i: (0, 0)),
        ],
        out_specs=[
            pl.BlockSpec((tm, D_MODEL), lambda i: (i, 0)),
            pl.BlockSpec((tm, D_MODEL), lambda i: (i, 0)),
            pl.BlockSpec((tm, 128), lambda i: (i, 0)),
        ],
        out_shape=[
            jax.ShapeDtypeStruct((n, D_MODEL), F32),
            jax.ShapeDtypeStruct((n, D_MODEL), F32),
            jax.ShapeDtypeStruct((n, 128), F32),
        ],
        compiler_params=_cparams(("parallel",)),
        name="out_proj",
    )(mix_a, mix_b, mix_s, xa, mod, w_out_b, norm2_w.reshape(1, D_MODEL), w_router, b_router)


def _router_kernel(lg_ref, o_ref):
    lg = lg_ref[...]
    tm = lg.shape[0]
    lane = lax.broadcasted_iota(jnp.int32, (tm, 128), 1)
    is_g = lane < N_GROUPS
    gl = jnp.where(is_g, lg, NEG_INF)
    gmax = jnp.max(gl, axis=-1, keepdims=True)
    gidx = jnp.min(jnp.where((gl == gmax) & is_g, lane, 128), axis=-1, keepdims=True)
    gsum = jnp.sum(jnp.where(is_g, jnp.exp(gl - gmax), 0.0), axis=-1, keepdims=True)
    g_w = 1.0 / gsum
    e_lane = lane - N_GROUPS
    in_grp = (e_lane >= gidx * EXPERTS_PER_GROUP) & (e_lane < (gidx + 1) * EXPERTS_PER_GROUP)
    el = jnp.where(in_grp, lg, NEG_INF)
    v1 = jnp.max(el, axis=-1, keepdims=True)
    i1 = jnp.min(jnp.where((el == v1) & in_grp, e_lane, 128), axis=-1, keepdims=True)
    el2 = jnp.where(e_lane == i1, NEG_INF, el)
    in2 = in_grp & (e_lane != i1)
    v2 = jnp.max(el2, axis=-1, keepdims=True)
    i2 = jnp.min(jnp.where((el2 == v2) & in2, e_lane, 128), axis=-1, keepdims=True)
    e2 = jnp.exp(v2 - v1)
    w1 = g_w / (1.0 + e2)
    w2 = g_w * e2 / (1.0 + e2)
    out = jnp.where(lane == 0, i1.astype(F32), 0.0)
    out = jnp.where(lane == 1, i2.astype(F32), out)
    out = jnp.where(lane == 2, w1, out)
    out = jnp.where(lane == 3, w2, out)
    o_ref[...] = out


def router(logits):
    tm = 256
    n = logits.shape[0]
    return pl.pallas_call(
        _router_kernel,
        grid=(n // tm,),
        in_specs=[pl.BlockSpec((tm, 128), lambda i: (i, 0))],
        out_specs=pl.BlockSpec((tm, 128), lambda i: (i, 0)),
        out_shape=jax.ShapeDtypeStruct((n, 128), F32),
        compiler_params=_cparams(("parallel",)),
        name="router",
    )(logits)


MOE_TILE = 256


def _moe_tiles(n):
    return TOP_K * n // MOE_TILE + N_EXPERTS


def _expert_kernel(te_ref, ts_ref, tok_ref, nt_ref, h_hbm, w1_ref, w3_ref, w2_ref, y_ref,
                   xbuf0, xbuf1, sem, w1b, w3b, w2b):
    i = pl.program_id(0)
    n_tiles = nt_ref[0]
    last = tok_ref.shape[0] - 1
    bufs = (xbuf0, xbuf1)

    def gather_start(tile, slot):
        base = ts_ref[tile]
        for r in range(MOE_TILE):
            tok = tok_ref[jnp.minimum(base + r, last)]
            pltpu.make_async_copy(h_hbm.at[pl.ds(tok, 1)], bufs[slot].at[pl.ds(r, 1)], sem.at[slot]).start()

    def gather_wait(slot):
        pltpu.make_async_copy(bufs[slot], bufs[slot], sem.at[slot]).wait()

    @pl.when(i == 0)
    def _():
        gather_start(0, 0)

    prev_e = te_ref[jnp.maximum(i - 1, 0)]
    new_e = (i == 0) | (te_ref[i] != prev_e)

    @pl.when((i < n_tiles) & new_e)
    def _():
        w1b[...] = w1_ref[0].astype(BF16)
        w3b[...] = w3_ref[0].astype(BF16)
        w2b[...] = w2_ref[0].astype(BF16)

    for slot in range(2):
        @pl.when((i < n_tiles) & (i % 2 == slot))
        def _():
            gather_wait(slot)
            gather_start(jnp.minimum(i + 1, n_tiles - 1), 1 - slot)
            xt = bufs[slot][...].astype(BF16)
            a = jnp.dot(xt, w1b[...], preferred_element_type=F32)
            u = jnp.dot(xt, w3b[...], preferred_element_type=F32)
            act = (a * jax.nn.sigmoid(a)) * u
            y_ref[...] = jnp.dot(act.astype(BF16), w2b[...], preferred_element_type=F32)

            @pl.when(i == n_tiles - 1)
            def _():
                gather_wait(1 - slot)

    @pl.when(i >= n_tiles)
    def _():
        y_ref[...] = jnp.zeros_like(y_ref)


def expert_mlp(tile_expert, tile_start, sorted_tok, n_tiles, h2, w1, w3, w2):
    e_map = lambda i, te, ts, tok, nt: (te[i], 0, 0)
    moe_tiles = tile_expert.shape[0]
    grid_spec = pltpu.PrefetchScalarGridSpec(
        num_scalar_prefetch=4,
        grid=(moe_tiles,),
        in_specs=[
            pl.BlockSpec(memory_space=pl.ANY),
            pl.BlockSpec((1, D_MODEL, EXPERT_HIDDEN), e_map),
            pl.BlockSpec((1, D_MODEL, EXPERT_HIDDEN), e_map),
            pl.BlockSpec((1, EXPERT_HIDDEN, D_MODEL), e_map),
        ],
        out_specs=pl.BlockSpec((MOE_TILE, D_MODEL), lambda i, te, ts, tok, nt: (i, 0)),
        scratch_shapes=[
            pltpu.VMEM((MOE_TILE, D_MODEL), F32),
            pltpu.VMEM((MOE_TILE, D_MODEL), F32),
            pltpu.SemaphoreType.DMA((2,)),
            pltpu.VMEM((D_MODEL, EXPERT_HIDDEN), BF16),
            pltpu.VMEM((D_MODEL, EXPERT_HIDDEN), BF16),
            pltpu.VMEM((EXPERT_HIDDEN, D_MODEL), BF16),
        ],
    )
    return pl.pallas_call(
        _expert_kernel,
        grid_spec=grid_spec,
        out_shape=jax.ShapeDtypeStruct((moe_tiles * MOE_TILE, D_MODEL), F32),
        compiler_params=_cparams(("arbitrary",)),
        name="expert_mlp",
    )(tile_expert, tile_start, sorted_tok, n_tiles, h2, w1, w3, w2)


CMB_TILE = 256


def _combine_kernel(pos_ref, y_hbm, x_ref, route_ref, mod_ref, o_ref, ybuf0, ybuf1, sem):
    i = pl.program_id(0)
    nt = pl.num_programs(0)
    bufs = (ybuf0, ybuf1)

    def gather_start(tile, slot):
        base = tile * (CMB_TILE * TOP_K)
        for r in range(CMB_TILE):
            for k in range(TOP_K):
                p = pos_ref[base + r * TOP_K + k]
                pltpu.make_async_copy(y_hbm.at[pl.ds(p, 1)], bufs[slot].at[k, pl.ds(r, 1)], sem.at[slot]).start()

    def gather_wait(slot):
        pltpu.make_async_copy(bufs[slot], bufs[slot], sem.at[slot]).wait()

    @pl.when(i == 0)
    def _():
        gather_start(0, 0)

    for slot in range(2):
        @pl.when(i % 2 == slot)
        def _():
            gather_wait(slot)
            gather_start(jnp.minimum(i + 1, nt - 1), 1 - slot)
            gate = jnp.where(_row_is_ctx(CMB_TILE), mod_ref[0, 5:6, :], mod_ref[1, 5:6, :])
            w0 = route_ref[:, TOP_K:TOP_K + 1]
            w1 = route_ref[:, TOP_K + 1:TOP_K + 2]
            o_ref[...] = x_ref[...] + gate * (w0 * bufs[slot][0] + w1 * bufs[slot][1])

            @pl.when(i == nt - 1)
            def _():
                gather_wait(1 - slot)


def moe_combine(pos, y_sorted, xa, route, mod):
    n = xa.shape[0]
    grid_spec = pltpu.PrefetchScalarGridSpec(
        num_scalar_prefetch=1,
        grid=(n // CMB_TILE,),
        in_specs=[
            pl.BlockSpec(memory_space=pl.ANY),
            pl.BlockSpec((CMB_TILE, D_MODEL), lambda i, pos: (i, 0)),
            pl.BlockSpec((CMB_TILE, 128), lambda i, pos: (i, 0)),
            pl.BlockSpec((2, 8, D_MODEL), lambda i, pos: (0, 0, 0)),
        ],
        out_specs=pl.BlockSpec((CMB_TILE, D_MODEL), lambda i, pos: (i, 0)),
        scratch_shapes=[
            pltpu.VMEM((TOP_K, CMB_TILE, D_MODEL), F32),
            pltpu.VMEM((TOP_K, CMB_TILE, D_MODEL), F32),
            pltpu.SemaphoreType.DMA((2,)),
        ],
    )
    return pl.pallas_call(
        _combine_kernel,
        grid_spec=grid_spec,
        out_shape=jax.ShapeDtypeStruct((n, D_MODEL), F32),
        compiler_params=_cparams(("arbitrary",)),
        name="moe_combine",
    )(pos, y_sorted, xa, route, mod)


def moe_dispatch_plan(route):
    n = route.shape[0]
    eid = route[:, 0:TOP_K].astype(jnp.int32).reshape(-1)
    p_total = eid.shape[0]
    experts = jnp.arange(N_EXPERTS, dtype=jnp.int32)
    onehot = (eid[:, None] == experts[None, :]).astype(jnp.int32)
    csum = jnp.cumsum(onehot, axis=0)
    counts = csum[-1]
    rank = jnp.sum(csum * onehot, axis=1) - 1
    tiles_per = (counts + MOE_TILE - 1) // MOE_TILE
    tile_off = jnp.cumsum(tiles_per) - tiles_per
    off = jnp.cumsum(counts) - counts
    pos = jnp.sum(onehot * tile_off[None, :], axis=1) * MOE_TILE + rank
    _, sorted_pair = lax.sort((eid, jnp.arange(p_total, dtype=jnp.int32)), num_keys=1, is_stable=True)
    sorted_tok = sorted_pair // TOP_K
    moe_tiles = _moe_tiles(n)
    n_tiles = jnp.sum(tiles_per).astype(jnp.int32)
    tile_ids = jnp.arange(moe_tiles, dtype=jnp.int32)
    tile_expert = jnp.sum((tile_ids[:, None] >= (tile_off + tiles_per)[None, :]).astype(jnp.int32), axis=1)
    tile_expert = jnp.minimum(tile_expert, N_EXPERTS - 1)
    t_onehot = (tile_expert[:, None] == experts[None, :]).astype(jnp.int32)
    tile_start = (jnp.sum(t_onehot * off[None, :], axis=1)
                  + (tile_ids - jnp.sum(t_onehot * tile_off[None, :], axis=1)) * MOE_TILE)
    return tile_expert, tile_start, sorted_tok, n_tiles.reshape(1), pos


def moe_block(xa, h2, logits, mod, w1, w3, w2, layer):
    route = router(logits)
    tile_expert, tile_start, sorted_tok, n_tiles, pos = moe_dispatch_plan(route)
    y_sorted = expert_mlp(tile_expert + layer * N_EXPERTS, tile_start, sorted_tok, n_tiles, h2, w1, w3, w2)
    return moe_combine(pos, y_sorted, xa, route, mod)


QK_W = ATTN_W + ATTN_KV_W
QKV_W = QK_W + ATTN_KV_W
ROPE_F = ATTN_HEAD_DIM // 4


def _norm_rope_heads(x, nw, cos, sin, out_ref, n_heads, scale):
    tm, width = x.shape
    xw = x * nw
    lane = lax.broadcasted_iota(jnp.int32, (tm, width), 1)
    odd = (lane // ROPE_F) % 2 == 1
    xs = jnp.where(odd, pltpu.roll(xw, ROPE_F, 1), pltpu.roll(xw, width - ROPE_F, 1))
    for h in range(n_heads):
        sl = slice(h * ATTN_HEAD_DIM, (h + 1) * ATTN_HEAD_DIM)
        xh = x[:, sl]
        inv = lax.rsqrt(jnp.mean(xh * xh, axis=-1, keepdims=True) + EPS) * scale
        out_ref[h] = ((xw[:, sl] * cos + xs[:, sl] * sin) * inv).astype(out_ref.dtype)


def _attn_prep_kernel(zq_ref, zk_ref, zv_ref, qw_ref, kw_ref, cos_ref, sin_ref, q_ref, k_ref, v_ref):
    cos = cos_ref[...]
    sin = sin_ref[...]
    _norm_rope_heads(zq_ref[...], qw_ref[...], cos, sin, q_ref, ATTN_HEADS, ATTN_HEAD_DIM ** -0.5)
    _norm_rope_heads(zk_ref[...], kw_ref[...], cos, sin, k_ref, ATTN_KV_HEADS, 1.0)
    for h in range(ATTN_KV_HEADS):
        v_ref[h] = zv_ref[:, h * ATTN_HEAD_DIM:(h + 1) * ATTN_HEAD_DIM].astype(BF16)


def attn_prep(z, q_norm_w, k_norm_w, cos_t, sin_t):
    tm = 256
    n = z.shape[0]
    qw = jnp.tile(q_norm_w, ATTN_HEADS).reshape(1, ATTN_W)
    kw = jnp.tile(k_norm_w, ATTN_KV_HEADS).reshape(1, ATTN_KV_W)
    hm = lambda h: pl.BlockSpec((h, tm, ATTN_HEAD_DIM), lambda i: (0, i, 0))
    sds = lambda h: jax.ShapeDtypeStruct((h, n, ATTN_HEAD_DIM), BF16)
    return pl.pallas_call(
        _attn_prep_kernel,
        grid=(n // tm,),
        in_specs=[
            pl.BlockSpec((tm, ATTN_W), lambda i: (i, Z_Q // ATTN_W)),
            pl.BlockSpec((tm, ATTN_KV_W), lambda i: (i, Z_K // ATTN_KV_W)),
            pl.BlockSpec((tm, ATTN_KV_W), lambda i: (i, Z_V // ATTN_KV_W)),
            pl.BlockSpec((1, ATTN_W), lambda i: (0, 0)),
            pl.BlockSpec((1, ATTN_KV_W), lambda i: (0, 0)),
            pl.BlockSpec((tm, ATTN_HEAD_DIM), lambda i: (i, 0)),
            pl.BlockSpec((tm, ATTN_HEAD_DIM), lambda i: (i, 0)),
        ],
        out_specs=[hm(ATTN_HEADS), hm(ATTN_KV_HEADS), hm(ATTN_KV_HEADS)],
        out_shape=[sds(ATTN_HEADS), sds(ATTN_KV_HEADS), sds(ATTN_KV_HEADS)],
        compiler_params=_cparams(("parallel",)),
        name="attn_prep",
    )(z, z, z, qw, kw, cos_t, sin_t)


def _attn_kernel(sink_ref, q_ref, kp_ref, ko_ref, kn_ref, kc_ref, vp_ref, vo_ref, vn_ref, vc_ref, o_ref, *, n_blocks):
    i = pl.program_id(0)
    ctx_blocks = CTX_LEN // ATTN_BLOCK
    B = ATTN_BLOCK
    rows = ATTN_GROUP * B
    ncol = 3 * B + CTX_LEN
    r = lax.broadcasted_iota(jnp.int32, (rows, ncol), 0) % B
    c = lax.broadcasted_iota(jnp.int32, (rows, ncol), 1)
    lo = jnp.where(i > ctx_blocks, 0, B)
    hi = jnp.where(i < n_blocks - 1, 3 * B, 2 * B)
    hi = jnp.where(i < ctx_blocks, 0, hi)
    band = (c >= r) & (c <= r + 2 * WINDOW) & (c >= lo) & (c < hi)
    mask = band | (c >= 3 * B)
    grp = lax.broadcasted_iota(jnp.int32, (rows, 1), 0) // B
    for kh in range(ATTN_KV_HEADS):
        q3 = jnp.concatenate([q_ref[kh * ATTN_GROUP + g] for g in range(ATTN_GROUP)], axis=0)
        kcat = jnp.concatenate([kp_ref[kh], ko_ref[kh], kn_ref[kh], kc_ref[kh]], axis=0)
        vcat = jnp.concatenate([vp_ref[kh], vo_ref[kh], vn_ref[kh], vc_ref[kh]], axis=0)
        s = lax.dot_general(q3, kcat, (((1,), (1,)), ((), ())), preferred_element_type=F32)
        s = jnp.where(mask, s, NEG_INF)
        sink = jnp.zeros((rows, 1), F32)
        for g in range(ATTN_GROUP):
            sink = jnp.where(grp == g, sink_ref[kh * ATTN_GROUP + g], sink)
        m = jnp.maximum(jnp.max(s, axis=-1, keepdims=True), sink)
        p = jnp.exp(s - m)
        den = jnp.sum(p, axis=-1, keepdims=True) + jnp.exp(sink - m)
        o = jnp.dot(p.astype(BF16), vcat, preferred_element_type=F32) / den
        for g in range(ATTN_GROUP):
            h = kh * ATTN_GROUP + g
            o_ref[:, h * ATTN_HEAD_DIM:(h + 1) * ATTN_HEAD_DIM] = o[g * B:(g + 1) * B].astype(o_ref.dtype)


def attention(qh, kh, vh, sink):
    n = qh.shape[1]
    B = ATTN_BLOCK
    nblk = n // B
    cb = CTX_LEN // B
    prev = lambda i: (0, jnp.clip(i - 1, cb, nblk - 1), 0)
    own = lambda i: (0, i, 0)
    nxt = lambda i: (0, jnp.clip(i + 1, cb, nblk - 1), 0)
    ctx = lambda i: (0, 0, 0)
    kv = lambda m: pl.BlockSpec((ATTN_KV_HEADS, B, ATTN_HEAD_DIM), m)
    kvc = pl.BlockSpec((ATTN_KV_HEADS, CTX_LEN, ATTN_HEAD_DIM), ctx)
    return pl.pallas_call(
        functools.partial(_attn_kernel, n_blocks=nblk),
        grid=(nblk,),
        in_specs=[
            pl.BlockSpec(memory_space=pltpu.SMEM),
            pl.BlockSpec((ATTN_HEADS, B, ATTN_HEAD_DIM), own),
            kv(prev), kv(own), kv(nxt), kvc,
            kv(prev), kv(own), kv(nxt), kvc,
        ],
        out_specs=pl.BlockSpec((B, ATTN_W), lambda i: (i, 0)),
        out_shape=jax.ShapeDtypeStruct((n, ATTN_W), BF16),
        compiler_params=_cparams(("parallel",)),
        name="attention",
    )(sink, qh, kh, kh, kh, kh, vh, vh, vh, vh)


def rope_tables(n_lat):
    rows = n_lat // GRID_W
    row = jnp.repeat(jnp.arange(rows), GRID_W).astype(F32)
    col = jnp.tile(jnp.arange(GRID_W), rows).astype(F32)
    inv = ROPE_BASE ** (-jnp.arange(ROPE_F, dtype=F32) / ROPE_F)
    ar, ac = row[:, None] * inv, col[:, None] * inv
    cos = jnp.concatenate([jnp.cos(ar), jnp.cos(ar), jnp.cos(ac), jnp.cos(ac)], axis=1)
    sin = jnp.concatenate([-jnp.sin(ar), jnp.sin(ar), -jnp.sin(ac), jnp.sin(ac)], axis=1)
    cos = jnp.concatenate([jnp.ones((CTX_LEN, ATTN_HEAD_DIM), F32), cos], axis=0)
    sin = jnp.concatenate([jnp.zeros((CTX_LEN, ATTN_HEAD_DIM), F32), sin], axis=0)
    return cos, sin


def attn_mixer(z, q_norm_w, k_norm_w, sink, cos_t, sin_t):
    qh, kh, vh = attn_prep(z, q_norm_w, k_norm_w, cos_t, sin_t)
    return attention(qh, kh, vh, sink)


DN_QKV = 3 * DN_W
DN_HALO = 8
DN_DH = 2 * DN_HEADS
DN_C = DN_CHUNK


def _dn_prep_kernel(zm_ref, zp_ref, zn_ref, zg_ref, cw_ref, al_ref, dtb_ref, q_ref, k_ref, v_ref, g_ref, *, tm, n_tiles):
    i = pl.program_id(0)
    ctx_tiles = CTX_LEN // tm
    has_prev = (i != 0) & (i != ctx_tiles)
    has_next = (i != ctx_tiles - 1) & (i != n_tiles - 1)
    prev = jnp.where(has_prev, zp_ref[...], 0.0)
    nxt = jnp.where(has_next, zn_ref[...], 0.0)
    xcat = jnp.concatenate([prev, zm_ref[...], nxt], axis=0)
    half = DN_CONV // 2
    acc = None
    for t in range(DN_CONV):
        off = DN_HALO - half + t
        term = xcat[off:off + tm, :] * cw_ref[t:t + 1, :]
        acc = term if acc is None else acc + term
    y = acc * jax.nn.sigmoid(acc)
    for h in range(DN_HEADS):
        qh = y[:, h * DN_HEAD_DIM:(h + 1) * DN_HEAD_DIM]
        kh = y[:, DN_W + h * DN_HEAD_DIM:DN_W + (h + 1) * DN_HEAD_DIM]
        q_ref[h] = qh * (lax.rsqrt(jnp.sum(qh * qh, axis=-1, keepdims=True) + EPS) * (DN_HEAD_DIM ** -0.5))
        k_ref[h] = kh * lax.rsqrt(jnp.sum(kh * kh, axis=-1, keepdims=True) + EPS)
        v_ref[h] = y[:, 2 * DN_W + h * DN_HEAD_DIM:2 * DN_W + (h + 1) * DN_HEAD_DIM]
    zg = zg_ref[...]
    lane = lax.broadcasted_iota(jnp.int32, zg.shape, 1)
    beta = jax.nn.sigmoid(zg)
    alpha = pltpu.roll(zg, 128 - DN_DH, 1)
    gl = -jnp.exp(al_ref[...]) * jax.nn.softplus(alpha + dtb_ref[...])
    g_ref[...] = jnp.where(lane < DN_DH, beta, pltpu.roll(gl, DN_DH, 1))


def dn_prep(z, conv_w, a_log, dt_bias):
    tm = 256
    n = z.shape[0]
    n_tiles = n // tm
    hb = tm // DN_HALO
    cw = jnp.zeros((8, DN_QKV), F32).at[0:DN_CONV].set(conv_w)
    pad_row = lambda t: jnp.zeros((1, 128), F32).at[0, 0:DN_DH].set(t.reshape(-1))
    hm = pl.BlockSpec((DN_HEADS, tm, DN_HEAD_DIM), lambda i: (0, i, 0))
    sds = jax.ShapeDtypeStruct((DN_HEADS, n, DN_HEAD_DIM), F32)
    return pl.pallas_call(
        functools.partial(_dn_prep_kernel, tm=tm, n_tiles=n_tiles),
        grid=(n_tiles,),
        in_specs=[
            pl.BlockSpec((tm, DN_QKV), lambda i: (i, 0)),
            pl.BlockSpec((DN_HALO, DN_QKV), lambda i: (jnp.maximum(i * hb - 1, 0), 0)),
            pl.BlockSpec((DN_HALO, DN_QKV), lambda i: (jnp.minimum((i + 1) * hb, n // DN_HALO - 1), 0)),
            pl.BlockSpec((tm, 128), lambda i: (i, Z_GATES // 128)),
            pl.BlockSpec((8, DN_QKV), lambda i: (0, 0)),
            pl.BlockSpec((1, 128), lambda i: (0, 0)),
            pl.BlockSpec((1, 128), lambda i: (0, 0)),
        ],
        out_specs=[hm, hm, hm, pl.BlockSpec((tm, 128), lambda i: (i, 0))],
        out_shape=[sds, sds, sds, jax.ShapeDtypeStruct((n, 128), F32)],
        compiler_params=_cparams(("parallel",)),
        name="dn_prep",
    )(z, z, z, z, cw, pad_row(a_log), pad_row(dt_bias))


def _bdot(a, b):
    return jnp.dot(a.astype(BF16), b.astype(BF16), preferred_element_type=F32)


def _unit_tri_inverse_many(nmats, row, col):
    eye = (row == col).astype(F32)
    same = lambda b: (row // b) == (col // b)
    d1 = [jnp.where(same(8), m, 0.0) for m in nmats]
    d2 = [_bdot(a, a) for a in d1]
    d3 = [_bdot(a, b) for a, b in zip(d1, d2)]
    d4 = [_bdot(b, b) for b in d2]
    x = [eye + a + b + c for a, b, c in zip(d1, d2, d3)]
    t = [a + _bdot(a, b) for a, b in zip(x, d4)]
    for b in (8, 16, 32):
        sel = same(2 * b) & jnp.logical_not(same(b))
        tmp = [_bdot(jnp.where(sel, m, 0.0), a) for m, a in zip(nmats, t)]
        t = [a + _bdot(a, c) for a, c in zip(t, tmp)]
    return t


def _dn_chunk_kernel(q_ref, k_ref, v_ref, g_ref, u_ref, wq_ref, kgt_ref, aqk_ref, gl_ref):
    C = DN_C
    row = lax.broadcasted_iota(jnp.int32, (C, C), 0)
    col = lax.broadcasted_iota(jnp.int32, (C, C), 1)
    g_all = g_ref[...]
    lane = lax.broadcasted_iota(jnp.int32, (C, 128), 1)
    gpart = jnp.where((lane >= DN_DH) & (lane < 2 * DN_DH), g_all, 0.0)
    lower = (row >= col).astype(F32)
    upper = (row <= col).astype(F32)
    csum_f = jnp.dot(lower, gpart, preferred_element_type=F32, precision=HIGHEST)
    csum_r = jnp.dot(upper, gpart, preferred_element_type=F32, precision=HIGHEST)
    gc_all = jnp.where(lane < DN_DH + DN_HEADS, csum_f, csum_r)
    gc_t = jnp.transpose(gc_all)
    tot = jnp.sum(gpart, axis=0, keepdims=True)
    gl_ref[0] = jnp.broadcast_to(jnp.exp(tot), (8, 128))
    pairs = [(d, h) for d in range(2) for h in range(DN_HEADS)]
    qs = [q_ref[h] for h in range(DN_HEADS)]
    ks = [k_ref[h] for h in range(DN_HEADS)]
    kbs, egcs, decays, kks = [], [], [], []
    for d, h in pairs:
        j = d * DN_HEADS + h
        incl = (row >= col) if d == 0 else (row <= col)
        gc_col = gc_all[:, DN_DH + j:DN_DH + j + 1]
        gc_row = gc_t[DN_DH + j:DN_DH + j + 1, :]
        decays.append(jnp.where(incl, jnp.exp(jnp.where(incl, gc_col - gc_row, 0.0)), 0.0))
        egcs.append(jnp.exp(gc_col))
        kbs.append(ks[h] * g_all[:, j:j + 1])
    for (d, h), kb in zip(pairs, kbs):
        kks.append(lax.dot_general(jnp.concatenate([kb, qs[h]], axis=0).astype(BF16), ks[h].astype(BF16),
                                   (((1,), (1,)), ((), ())), preferred_element_type=F32))
    nmats = []
    for (d, h), kk, decay in zip(pairs, kks, decays):
        strict = (row > col) if d == 0 else (row < col)
        nmats.append(jnp.where(strict, -(kk[:C] * decay), 0.0))
    tinv = _unit_tri_inverse_many(nmats, row, col)
    sols = []
    for (d, h), t, kb, egc in zip(pairs, tinv, kbs, egcs):
        j = d * DN_HEADS + h
        rhs = jnp.concatenate([v_ref[h] * g_all[:, j:j + 1], kb * egc], axis=1)
        sols.append(_bdot(t, rhs))
    for (d, h), sol, kk, decay, egc in zip(pairs, sols, kks, decays, egcs):
        j = d * DN_HEADS + h
        incl = (row >= col) if d == 0 else (row <= col)
        gc_col = gc_all[:, DN_DH + j:DN_DH + j + 1]
        tot_j = tot[:, DN_DH + j:DN_DH + j + 1]
        u_ref[j] = sol[:, :DN_HEAD_DIM]
        wq_ref[j] = jnp.concatenate([sol[:, DN_HEAD_DIM:], qs[h] * egc], axis=0).astype(BF16)
        kgt_ref[j] = jnp.transpose(ks[h] * jnp.exp(tot_j - gc_col)).astype(BF16)
        aqk_ref[j] = jnp.where(incl, kk[C:] * decay, 0.0).astype(BF16)


def dn_chunk(qn, kn, vn, gates):
    n = qn.shape[1]
    nc = n // DN_C
    hm = pl.BlockSpec((DN_HEADS, DN_C, DN_HEAD_DIM), lambda c: (0, c, 0))
    sds = jax.ShapeDtypeStruct
    return pl.pallas_call(
        _dn_chunk_kernel,
        grid=(nc,),
        in_specs=[hm, hm, hm, pl.BlockSpec((DN_C, 128), lambda c: (c, 0))],
        out_specs=[
            pl.BlockSpec((DN_DH, DN_C, DN_HEAD_DIM), lambda c: (0, c, 0)),
            pl.BlockSpec((DN_DH, 2 * DN_C, DN_HEAD_DIM), lambda c: (0, c, 0)),
            pl.BlockSpec((DN_DH, DN_HEAD_DIM, DN_C), lambda c: (0, c, 0)),
            pl.BlockSpec((DN_DH, DN_C, DN_C), lambda c: (0, c, 0)),
            pl.BlockSpec((1, 8, 128), lambda c: (c, 0, 0)),
        ],
        out_shape=[
            sds((DN_DH, n, DN_HEAD_DIM), F32),
            sds((DN_DH, 2 * n, DN_HEAD_DIM), BF16),
            sds((DN_DH, nc * DN_HEAD_DIM, DN_C), BF16),
            sds((DN_DH, n, DN_C), BF16),
            sds((nc, 8, 128), F32),
        ],
        compiler_params=_cparams(("parallel",)),
        name="dn_chunk",
    )(qn, kn, vn, gates)


def _dn_scan_kernel(uf_ref, wqf_ref, kgf_ref, aqf_ref, glf_ref, ur_ref, wqr_ref, kgr_ref, aqr_ref, glr_ref,
                    of_ref, or_ref, state):
    @pl.when(pl.program_id(0) == 0)
    def _():
        state[...] = jnp.zeros_like(state)

    C = DN_C
    sets = ((uf_ref, wqf_ref, kgf_ref, aqf_ref, glf_ref, of_ref), (ur_ref, wqr_ref, kgr_ref, aqr_ref, glr_ref, or_ref))
    chains = [(d, h) + sets[d] for d in range(2) for h in range(DN_HEADS)]
    ss = [state[d * DN_HEADS + h] for d, h, *_ in chains]
    ts = [jnp.dot(wq_ref[h], s.astype(BF16), preferred_element_type=F32)
          for (d, h, u_ref, wq_ref, *_), s in zip(chains, ss)]
    vbs = [(u_ref[h] - t[:C]).astype(BF16) for (d, h, u_ref, *_), t in zip(chains, ts)]
    os_ = [t[C:] + jnp.dot(aq_ref[h], vb, preferred_element_type=F32)
           for (d, h, u_ref, wq_ref, kg_ref, aq_ref, *_), t, vb in zip(chains, ts, vbs)]
    ns = [s * gl_ref[0][0:1, DN_DH + d * DN_HEADS + h:DN_DH + d * DN_HEADS + h + 1]
          + jnp.dot(kg_ref[h], vb, preferred_element_type=F32)
          for (d, h, u_ref, wq_ref, kg_ref, aq_ref, gl_ref, o_ref), s, vb in zip(chains, ss, vbs)]
    for (d, h, *_, o_ref), o, s_new in zip(chains, os_, ns):
        o_ref[h] = o
        state[d * DN_HEADS + h] = s_new


def dn_scan(u, wq, kgt, aqk, gl):
    n = u.shape[1]
    nc = n // DN_C
    cc = CTX_LEN // DN_C
    fwd = lambda s: s
    rev = lambda s: jnp.where(s < cc, cc - 1 - s, nc + cc - 1 - s)
    specs = []
    for d, cm in enumerate((fwd, rev)):
        specs += [
            pl.BlockSpec((DN_HEADS, DN_C, DN_HEAD_DIM), lambda s, d=d, cm=cm: (d, cm(s), 0)),
            pl.BlockSpec((DN_HEADS, 2 * DN_C, DN_HEAD_DIM), lambda s, d=d, cm=cm: (d, cm(s), 0)),
            pl.BlockSpec((DN_HEADS, DN_HEAD_DIM, DN_C), lambda s, d=d, cm=cm: (d, cm(s), 0)),
            pl.BlockSpec((DN_HEADS, DN_C, DN_C), lambda s, d=d, cm=cm: (d, cm(s), 0)),
            pl.BlockSpec((1, 8, 128), lambda s, cm=cm: (cm(s), 0, 0)),
        ]
    osd = jax.ShapeDtypeStruct((DN_HEADS, n, DN_HEAD_DIM), F32)
    return pl.pallas_call(
        _dn_scan_kernel,
        grid=(nc,),
        in_specs=specs,
        out_specs=[pl.BlockSpec((DN_HEADS, DN_C, DN_HEAD_DIM), lambda s: (0, fwd(s), 0)),
                   pl.BlockSpec((DN_HEADS, DN_C, DN_HEAD_DIM), lambda s: (0, rev(s), 0))],
        out_shape=[osd, osd],
        scratch_shapes=[pltpu.VMEM((DN_DH, DN_HEAD_DIM, DN_HEAD_DIM), F32)],
        compiler_params=_cparams(("arbitrary",)),
        name="dn_scan",
    )(u, wq, kgt, aqk, gl, u, wq, kgt, aqk, gl)


def _dn_out_kernel(of_ref, or_ref, zg_ref, nw_ref, o_ref):
    for h in range(DN_HEADS):
        o = of_ref[h] + or_ref[h]
        o = o * lax.rsqrt(jnp.mean(o * o, axis=-1, keepdims=True) + EPS) * nw_ref[...]
        gate = zg_ref[:, h * DN_HEAD_DIM:(h + 1) * DN_HEAD_DIM]
        o_ref[:, h * DN_HEAD_DIM:(h + 1) * DN_HEAD_DIM] = (o * (gate * jax.nn.sigmoid(gate))).astype(o_ref.dtype)


def dn_out(o_f, o_r, z, o_norm_w):
    tm = 256
    n = z.shape[0]
    hm = pl.BlockSpec((DN_HEADS, tm, DN_HEAD_DIM), lambda i: (0, i, 0))
    return pl.pallas_call(
        _dn_out_kernel,
        grid=(n // tm,),
        in_specs=[hm, hm, pl.BlockSpec((tm, DN_W), lambda i: (i, Z_DNG // DN_W)),
                  pl.BlockSpec((1, DN_HEAD_DIM), lambda i: (0, 0))],
        out_specs=pl.BlockSpec((tm, DN_W), lambda i: (i, 0)),
        out_shape=jax.ShapeDtypeStruct((n, DN_W), BF16),
        compiler_params=_cparams(("parallel",)),
        name="dn_out",
    )(o_f, o_r, z, o_norm_w.reshape(1, DN_HEAD_DIM))


def dn_mixer(z, conv_w, a_log, dt_bias, o_norm_w):
    qn, kn, vn, gates = dn_prep(z, conv_w, a_log, dt_bias)
    u, wq, kgt, aqk, gl = dn_chunk(qn, kn, vn, gates)
    o_f, o_r = dn_scan(u, wq, kgt, aqk, gl)
    return dn_out(o_f, o_r, z, o_norm_w)


S5_SUB = 8
S5_BLK = S5_GROUPS * S5_STATE // S5_SUB
S5_UB = S5_W // S5_SUB
S5_T = 64


def _s5_param_kernel(lr_ref, li_ref, ls_ref, br_ref, bi_ref, ar_ref, ai_ref, bbr_ref, bbi_ref):
    lr = jnp.minimum(lr_ref[...], -1e-4)
    li = li_ref[...]
    dt = jnp.exp(ls_ref[...])
    mag = jnp.exp(lr * dt)
    ar = mag * jnp.cos(li * dt)
    ai = mag * jnp.sin(li * dt)
    den = lr * lr + li * li
    nr, ni = ar - 1.0, ai
    fr = (nr * lr + ni * li) / den
    fi = (ni * lr - nr * li) / den
    ar_ref[...] = ar
    ai_ref[...] = ai
    for h in range(S5_GROUP_CH):
        bbr_ref[h] = fr * br_ref[h] - fi * bi_ref[h]
        bbi_ref[h] = fr * bi_ref[h] + fi * br_ref[h]


def s5_params(lam_re, lam_im, log_step, b_re, b_im):
    r = 2 * S5_GROUPS
    ls = jnp.broadcast_to(log_step.reshape(r, 1), (r, S5_STATE))
    bt = lambda t: jnp.transpose(t.reshape(r, S5_STATE, S5_GROUP_CH), (2, 0, 1))
    sds = jax.ShapeDtypeStruct
    return pl.pallas_call(
        _s5_param_kernel,
        out_shape=[sds((r, S5_STATE), F32), sds((r, S5_STATE), F32),
                   sds((S5_GROUP_CH, r, S5_STATE), F32), sds((S5_GROUP_CH, r, S5_STATE), F32)],
        name="s5_params",
    )(lam_re.reshape(r, S5_STATE), lam_im.reshape(r, S5_STATE), ls, bt(b_re), bt(b_im))


def _s5_scan_kernel(uf_ref, ur_ref, a_ref, rb_ref, cc_ref, yf_ref, yr_ref, bu, hh, st):
    rows = S5_T * S5_SUB

    @pl.when(pl.program_id(0) == 0)
    def _():
        st[...] = jnp.zeros_like(st)

    sub = lax.broadcasted_iota(jnp.int32, (rows, S5_W), 0) % S5_SUB
    blk = lax.broadcasted_iota(jnp.int32, (rows, S5_W), 1) // S5_UB
    own = sub == blk
    for d, u_ref in enumerate((uf_ref, ur_ref)):
        ue = jnp.broadcast_to(u_ref[...][:, None, :], (S5_T, S5_SUB, S5_W)).reshape(rows, S5_W)
        lhs = jnp.where(own, ue, 0.0).astype(BF16)
        bu[d, 0] = jnp.dot(lhs, rb_ref[d, 0], preferred_element_type=F32)
        bu[d, 1] = jnp.dot(lhs, rb_ref[d, 1], preferred_element_type=F32)

    a = [[a_ref[d, c] for c in range(2)] for d in range(2)]

    def body(t, carry):
        fr, fi, rr, ri = carry
        rf = pl.multiple_of(t * S5_SUB, S5_SUB)
        rv = pl.multiple_of((S5_T - 1 - t) * S5_SUB, S5_SUB)
        nfr = a[0][0] * fr - a[0][1] * fi + bu[0, 0, pl.ds(rf, S5_SUB), :]
        nfi = a[0][0] * fi + a[0][1] * fr + bu[0, 1, pl.ds(rf, S5_SUB), :]
        nrr = a[1][0] * rr - a[1][1] * ri + bu[1, 0, pl.ds(rv, S5_SUB), :]
        nri = a[1][0] * ri + a[1][1] * rr + bu[1, 1, pl.ds(rv, S5_SUB), :]
        hh[0, 0, pl.ds(rf, S5_SUB), :] = nfr
        hh[0, 1, pl.ds(rf, S5_SUB), :] = nfi
        hh[1, 0, pl.ds(rv, S5_SUB), :] = nrr
        hh[1, 1, pl.ds(rv, S5_SUB), :] = nri
        return nfr, nfi, nrr, nri

    fin = lax.fori_loop(0, S5_T, body, (st[0, 0], st[0, 1], st[1, 0], st[1, 1]), unroll=8)
    st[0, 0], st[0, 1], st[1, 0], st[1, 1] = fin

    for d, y_ref in enumerate((yf_ref, yr_ref)):
        ye = (jnp.dot(hh[d, 0].astype(BF16), cc_ref[d, 0], preferred_element_type=F32)
              + jnp.dot(hh[d, 1].astype(BF16), cc_ref[d, 1], preferred_element_type=F32))
        ye = jnp.where(own, ye, 0.0)
        y_ref[...] = jnp.sum(ye.reshape(S5_T, S5_SUB, S5_W), axis=1)


def s5_scan(z, a8, rb, cc):
    rows = S5_T * S5_SUB
    n = z.shape[0]
    nt = n // S5_T
    ct = CTX_LEN // S5_T
    ucol = Z_S5 // S5_W

    def rev_tile(i):
        return jnp.where(i < ct, ct - 1 - i, nt + ct - 1 - i)

    full = lambda shape: pl.BlockSpec(shape, lambda i: (0,) * len(shape))
    sds = jax.ShapeDtypeStruct((n, S5_W), F32)
    return pl.pallas_call(
        _s5_scan_kernel,
        grid=(nt,),
        in_specs=[
            pl.BlockSpec((S5_T, S5_W), lambda i: (i, ucol)),
            pl.BlockSpec((S5_T, S5_W), lambda i: (rev_tile(i), ucol)),
            full((2, 2, S5_SUB, S5_BLK)),
            full((2, 2, S5_W, S5_BLK)),
            full((2, 2, S5_BLK, S5_W)),
        ],
        out_specs=[pl.BlockSpec((S5_T, S5_W), lambda i: (i, 0)),
                   pl.BlockSpec((S5_T, S5_W), lambda i: (rev_tile(i), 0))],
        out_shape=[sds, sds],
        scratch_shapes=[pltpu.VMEM((2, 2, rows, S5_BLK), F32), pltpu.VMEM((2, 2, rows, S5_BLK), F32),
                        pltpu.VMEM((2, 2, S5_SUB, S5_BLK), F32)],
        compiler_params=_cparams(("arbitrary",)),
        name="s5_scan",
    )(z, z, a8, rb, cc)


def _s5_glu_kernel(u_ref, yf_ref, yr_ref, d_ref, w_ref, o_ref):
    y = u_ref[...] * d_ref[...] + yf_ref[...] + yr_ref[...]
    y = jax.nn.gelu(y)
    gate = jnp.dot(y.astype(BF16), w_ref[...], preferred_element_type=F32)
    o_ref[...] = (y * jax.nn.sigmoid(gate)).astype(o_ref.dtype)


def s5_glu(z, yf, yr, d_skip, w_glu_b):
    n = z.shape[0]
    tm = 1056 if n % 1056 == 0 else 256
    row = lambda i: (i, 0)
    return pl.pallas_call(
        _s5_glu_kernel,
        grid=(n // tm,),
        in_specs=[
            pl.BlockSpec((tm, S5_W), lambda i: (i, Z_S5 // S5_W)),
            pl.BlockSpec((tm, S5_W), row),
            pl.BlockSpec((tm, S5_W), row),
            pl.BlockSpec((1, S5_W), lambda i: (0, 0)),
            pl.BlockSpec((S5_W, S5_W), lambda i: (0, 0)),
        ],
        out_specs=pl.BlockSpec((tm, S5_W), row),
        out_shape=jax.ShapeDtypeStruct((n, S5_W), BF16),
        compiler_params=_cparams(("parallel",)),
        name="s5_glu",
    )(z, yf, yr, d_skip.reshape(1, S5_W), w_glu_b)


def s5_mixer(z, lam_re, lam_im, log_step, b_re, b_im, c_re, c_im, d_skip, w_glu):
    n = z.shape[0]
    ar, ai, bbr, bbi = s5_params(lam_re, lam_im, log_step, b_re, b_im)
    g_blk = jax.nn.one_hot(jnp.arange(S5_GROUPS) % (S5_GROUPS // S5_SUB), S5_GROUPS // S5_SUB, dtype=F32)

    def place_b(bb):
        bb = jnp.transpose(bb.reshape(S5_GROUP_CH, 2, S5_GROUPS, S5_STATE), (1, 2, 0, 3))
        return jnp.einsum('dghp,gj->dghjp', bb, g_blk).reshape(2, S5_W, S5_BLK)

    def place_c(cm):
        return jnp.einsum('dghp,gj->djpgh', cm, g_blk).reshape(2, S5_BLK, S5_W)

    rb = jnp.stack([place_b(bbr), place_b(bbi)], axis=1).astype(BF16)
    cc = jnp.stack([place_c(c_re), -place_c(c_im)], axis=1).astype(BF16)
    a8 = jnp.stack([ar.reshape(2, S5_SUB, S5_BLK), ai.reshape(2, S5_SUB, S5_BLK)], axis=1)
    yf, yr = s5_scan(z, a8, rb, cc)
    return s5_glu(z, yf, yr, d_skip, w_glu.astype(BF16))


def _permute_w_in(w):
    n_gate = 4 * DN_HEADS
    attn = w[:, :QKV_W]
    dn = w[:, QKV_W:QKV_W + 4 * DN_W]
    gates = w[:, QKV_W + 4 * DN_W:QKV_W + 4 * DN_W + n_gate]
    s5 = w[:, QKV_W + 4 * DN_W + n_gate:]
    pad = jnp.zeros((w.shape[0], Z_S5 - Z_GATES - n_gate), w.dtype)
    return jnp.concatenate([dn, attn, gates, pad, s5], axis=1).astype(BF16)


def kernel(x, c, ctx, c_ctx, w_ada, b_ada, norm1_w, norm2_w, w_in, w_out, attn_q_norm, attn_k_norm, attn_sink,
           dn_conv, dn_a_log, dn_dt_bias, dn_o_norm, s5_lam_re, s5_lam_im, s5_log_step, s5_b_re, s5_b_im,
           s5_c_re, s5_c_im, s5_d, s5_w_glu, moe_w_grp, moe_b_grp, moe_w_rt, moe_b_rt, moe_w1, moe_w3, moe_w2):
    b, n, d = x.shape
    assert b == 1 and ctx.shape[1] == CTX_LEN and d == D_MODEL
    lc = CTX_LEN
    xa = jnp.concatenate([ctx[0], x[0]], axis=0)
    cvec = jnp.zeros((8, D_MODEL), F32).at[0].set(c_ctx).at[1].set(c[0])
    mods = adaln(cvec, w_ada, b_ada)
    cos_t, sin_t = rope_tables(n)
    w1_all = moe_w1.reshape(DEPTH * N_EXPERTS, D_MODEL, EXPERT_HIDDEN)
    w3_all = moe_w3.reshape(DEPTH * N_EXPERTS, D_MODEL, EXPERT_HIDDEN)
    w2_all = moe_w2.reshape(DEPTH * N_EXPERTS, EXPERT_HIDDEN, D_MODEL)
    for layer in range(DEPTH):
        mod = mods[layer, 0:2].reshape(2, 6, D_MODEL)
        mod = jnp.concatenate([mod, jnp.zeros((2, 2, D_MODEL), F32)], axis=1)
        z = in_proj(xa, norm1_w[layer], mod, _permute_w_in(w_in[layer]))
        mix_a = attn_mixer(z, attn_q_norm[layer], attn_k_norm[layer], attn_sink[layer], cos_t, sin_t)
        mix_b = dn_mixer(z, dn_conv[layer], dn_a_log[layer], dn_dt_bias[layer], dn_o_norm[layer])
        mix_s = s5_mixer(z, s5_lam_re[layer], s5_lam_im[layer], s5_log_step[layer], s5_b_re[layer],
                         s5_b_im[layer], s5_c_re[layer], s5_c_im[layer], s5_d[layer], s5_w_glu[layer])
        w_router = jnp.concatenate([moe_w_grp[layer], moe_w_rt[layer],
                                    jnp.zeros((D_MODEL, 128 - N_GROUPS - N_EXPERTS), F32)], axis=1)
        b_router = jnp.concatenate([moe_b_grp[layer], moe_b_rt[layer],
                                    jnp.zeros((128 - N_GROUPS - N_EXPERTS,), F32)]).reshape(1, 128)
        xa, h2, logits = out_proj(mix_a, mix_b, mix_s, xa, mod, w_out[layer].astype(BF16), norm2_w[layer],
                                  w_router, b_router)
        xa = moe_block(xa, h2, logits, mod, w1_all, w3_all, w2_all, layer)
    return xa[lc:][None]
```

```python
import functools

import jax
import jax.numpy as jnp
from jax import lax
from jax.experimental import pallas as pl
from jax.experimental.pallas import tpu as pltpu

F32 = jnp.float32
BF16 = jnp.bfloat16
HIGHEST = lax.Precision.HIGHEST

D_MODEL = 2048
SEQ = 8192
DEPTH = 2
GRID_W = 64
CTX_LEN = 256
N_ALL = CTX_LEN + SEQ
EPS = 1e-6
NEG_INF = -1e30

ATTN_HEADS = 12
ATTN_KV_HEADS = 4
ATTN_HEAD_DIM = 64
ATTN_GROUP = ATTN_HEADS // ATTN_KV_HEADS
ATTN_W = ATTN_HEADS * ATTN_HEAD_DIM
ATTN_KV_W = ATTN_KV_HEADS * ATTN_HEAD_DIM
WINDOW = 128
ATTN_BLOCK = 128
ROPE_BASE = 10000.0
DN_HEADS = 6
DN_HEAD_DIM = 128
DN_W = DN_HEADS * DN_HEAD_DIM
DN_CONV = 5
DN_CHUNK = 64
S5_W = D_MODEL - ATTN_W - DN_W
S5_GROUP_CH = 16
S5_GROUPS = S5_W // S5_GROUP_CH
S5_STATE = 64
N_GROUPS = 4
EXPERTS_PER_GROUP = 8
N_EXPERTS = N_GROUPS * EXPERTS_PER_GROUP
TOP_K = 2
EXPERT_HIDDEN = 512

Z_DNQ, Z_DNK, Z_DNV, Z_DNG = 0, 768, 1536, 2304
Z_Q, Z_K, Z_V = 3072, 3840, 4096
Z_GATES = 4352
Z_BETA, Z_ALPHA = Z_GATES, Z_GATES + 2 * DN_HEADS
Z_S5 = 4608
Z_W = 5120

VMEM_LIMIT = 56 * 1024 * 1024


def _cparams(sem, vmem=VMEM_LIMIT):
    return pltpu.CompilerParams(dimension_semantics=sem, vmem_limit_bytes=vmem)


def _adaln_kernel(c_ref, w_ref, b_ref, o_ref):
    cv = c_ref[...]
    s = cv * jax.nn.sigmoid(cv)
    o_ref[0] = jnp.dot(s, w_ref[0], preferred_element_type=F32, precision=HIGHEST) + b_ref[0]


def adaln(cvec, w_ada, b_ada):
    L = w_ada.shape[0]
    tn = 1024
    n6 = 6 * D_MODEL
    return pl.pallas_call(
        _adaln_kernel,
        grid=(L, n6 // tn),
        in_specs=[
            pl.BlockSpec((8, D_MODEL), lambda l, j: (0, 0)),
            pl.BlockSpec((1, D_MODEL, tn), lambda l, j: (l, 0, j)),
            pl.BlockSpec((1, 1, tn), lambda l, j: (l, 0, j)),
        ],
        out_specs=pl.BlockSpec((1, 8, tn), lambda l, j: (l, 0, j)),
        out_shape=jax.ShapeDtypeStruct((L, 8, n6), F32),
        compiler_params=_cparams(("parallel", "parallel")),
        name="adaln",
    )(cvec, w_ada, b_ada.reshape(L, 1, n6))


NORM_ROWS = 64


def _is_ctx_rows(base, rows):
    return base + lax.broadcasted_iota(jnp.int32, (rows, 1), 0) < CTX_LEN


def _row_is_ctx(tm):
    return _is_ctx_rows(pl.program_id(0) * tm, tm)


def _norm_mod(x, nw, mod_ref, shift_i, scale_i, is_ctx):
    ms = jnp.mean(x * x, axis=-1, keepdims=True)
    h = x * lax.rsqrt(ms + EPS) * nw
    sc = jnp.where(is_ctx, mod_ref[0, scale_i:scale_i + 1, :], mod_ref[1, scale_i:scale_i + 1, :])
    sh = jnp.where(is_ctx, mod_ref[0, shift_i:shift_i + 1, :], mod_ref[1, shift_i:shift_i + 1, :])
    return h * (1.0 + sc) + sh


def _in_proj_kernel(x_ref, nw_ref, mod_ref, w_ref, o_ref, h_scr, *, tm):
    @pl.when(pl.program_id(1) == 0)
    def _():
        def chunk(c, carry):
            r0 = pl.multiple_of(c * NORM_ROWS, NORM_ROWS)
            h = _norm_mod(x_ref[pl.ds(r0, NORM_ROWS), :], nw_ref[...], mod_ref, 0, 1,
                          _is_ctx_rows(pl.program_id(0) * tm + r0, NORM_ROWS))
            h_scr[pl.ds(r0, NORM_ROWS), :] = h.astype(BF16)
            return carry
        lax.fori_loop(0, tm // NORM_ROWS, chunk, 0)

    o_ref[...] = jnp.dot(h_scr[...], w_ref[...], preferred_element_type=F32)


def in_proj(xa, norm_w, mod, w_in_p):
    tm, tn = 512, 1024
    n = xa.shape[0]
    return pl.pallas_call(
        functools.partial(_in_proj_kernel, tm=tm),
        grid=(pl.cdiv(n, tm), Z_W // tn),
        in_specs=[
            pl.BlockSpec((tm, D_MODEL), lambda i, j: (i, 0)),
            pl.BlockSpec((1, D_MODEL), lambda i, j: (0, 0)),
            pl.BlockSpec((2, 8, D_MODEL), lambda i, j: (0, 0, 0)),
            pl.BlockSpec((D_MODEL, tn), lambda i, j: (0, j)),
        ],
        out_specs=pl.BlockSpec((tm, tn), lambda i, j: (i, j)),
        out_shape=jax.ShapeDtypeStruct((n, Z_W), F32),
        scratch_shapes=[pltpu.VMEM((tm, D_MODEL), BF16)],
        compiler_params=_cparams(("parallel", "arbitrary")),
        name="in_proj",
    )(xa, norm_w.reshape(1, D_MODEL), mod, w_in_p)


def _out_proj_kernel(a_ref, of_ref, or_ref, zg_ref, onw_ref, zu_ref, yf_ref, yr_ref, dsk_ref, wg_ref,
                     x_ref, mod_ref, w_ref, nw_ref, wr_ref, br_ref, xo_ref, h_ref, rt_ref, *, tm):
    is_ctx = _row_is_ctx(tm)
    b_parts = []
    for hd in range(DN_HEADS):
        o = of_ref[hd] + or_ref[hd]
        o = o * lax.rsqrt(jnp.mean(o * o, axis=-1, keepdims=True) + EPS) * onw_ref[...]
        g = zg_ref[:, hd * DN_HEAD_DIM:(hd + 1) * DN_HEAD_DIM]
        b_parts.append((o * (g * jax.nn.sigmoid(g))).astype(BF16))
    b = jnp.concatenate(b_parts, axis=1)
    ys = jax.nn.gelu(zu_ref[...] * dsk_ref[...] + yf_ref[...] + yr_ref[...])
    s = (ys * jax.nn.sigmoid(jnp.dot(ys.astype(BF16), wg_ref[...], preferred_element_type=F32))).astype(BF16)
    y = (jnp.dot(a_ref[...], w_ref[0:ATTN_W, :], preferred_element_type=F32)
         + jnp.dot(b, w_ref[ATTN_W:ATTN_W + DN_W, :], preferred_element_type=F32)
         + jnp.dot(s, w_ref[ATTN_W + DN_W:, :], preferred_element_type=F32))
    gate = jnp.where(is_ctx, mod_ref[0, 2:3, :], mod_ref[1, 2:3, :])
    xn = x_ref[...] + gate * y
    xo_ref[...] = xn
    h = _norm_mod(xn, nw_ref[...], mod_ref, 3, 4, is_ctx)
    h_ref[...] = h
    wr = wr_ref[...]
    h_hi = h.astype(BF16)
    h_lo = (h - h_hi.astype(F32)).astype(BF16)
    w_hi = wr.astype(BF16)
    w_lo = (wr - w_hi.astype(F32)).astype(BF16)
    lg = (jnp.dot(h_hi, w_hi, preferred_element_type=F32) + jnp.dot(h_hi, w_lo, preferred_element_type=F32)
          + jnp.dot(h_lo, w_hi, preferred_element_type=F32) + br_ref[...])
    rt_ref[...] = _route(lg)


def out_proj(mix_a, o_f, o_r, z, yf, yr, xa, mod, w_out_b, norm2_w, w_router, b_router, o_norm_w, d_skip, w_glu_b):
    tm = 256
    n = xa.shape[0]
    row = lambda i: (i, 0)
    const2 = lambda i: (0, 0)
    hm = pl.BlockSpec((DN_HEADS, tm, DN_HEAD_DIM), lambda i: (0, i, 0))
    return pl.pallas_call(
        functools.partial(_out_proj_kernel, tm=tm),
        grid=(n // tm,),
        in_specs=[
            pl.BlockSpec((tm, ATTN_W), row),
            hm, hm,
            pl.BlockSpec((tm, DN_W), lambda i: (i, Z_DNG // DN_W)),
            pl.BlockSpec((1, DN_HEAD_DIM), const2),
            pl.BlockSpec((tm, S5_W), lambda i: (i, Z_S5 // S5_W)),
            pl.BlockSpec((tm, S5_W), row),
            pl.BlockSpec((tm, S5_W), row),
            pl.BlockSpec((1, S5_W), const2),
            pl.BlockSpec((S5_W, S5_W), const2),
            pl.BlockSpec((tm, D_MODEL), row),
            pl.BlockSpec((2, 8, D_MODEL), lambda i: (0, 0, 0)),
            pl.BlockSpec((D_MODEL, D_MODEL), const2),
            pl.BlockSpec((1, D_MODEL), const2),
            pl.BlockSpec((D_MODEL, 128), const2),
            pl.BlockSpec((1, 128), const2),
        ],
        out_specs=[
            pl.BlockSpec((tm, D_MODEL), row),
            pl.BlockSpec((tm, D_MODEL), row),
            pl.BlockSpec((tm, 128), row),
        ],
        out_shape=[
            jax.ShapeDtypeStruct((n, D_MODEL), F32),
            jax.ShapeDtypeStruct((n, D_MODEL), F32),
            jax.ShapeDtypeStruct((n, 128), F32),
        ],
        compiler_params=_cparams(("parallel",)),
        name="out_proj",
    )(mix_a, o_f, o_r, z, o_norm_w.reshape(1, DN_HEAD_DIM), z, yf, yr, d_skip.reshape(1, S5_W), w_glu_b,
      xa, mod, w_out_b, norm2_w.reshape(1, D_MODEL), w_router, b_router)


def _route(lg):
    tm = lg.shape[0]
    lane = lax.broadcasted_iota(jnp.int32, (tm, 128), 1)
    is_g = lane < N_GROUPS
    gl = jnp.where(is_g, lg, NEG_INF)
    gmax = jnp.max(gl, axis=-1, keepdims=True)
    gidx = jnp.min(jnp.where((gl == gmax) & is_g, lane, 128), axis=-1, keepdims=True)
    gsum = jnp.sum(jnp.where(is_g, jnp.exp(gl - gmax), 0.0), axis=-1, keepdims=True)
    g_w = 1.0 / gsum
    e_lane = lane - N_GROUPS
    in_grp = (e_lane >= gidx * EXPERTS_PER_GROUP) & (e_lane < (gidx + 1) * EXPERTS_PER_GROUP)
    el = jnp.where(in_grp, lg, NEG_INF)
    v1 = jnp.max(el, axis=-1, keepdims=True)
    i1 = jnp.min(jnp.where((el == v1) & in_grp, e_lane, 128), axis=-1, keepdims=True)
    el2 = jnp.where(e_lane == i1, NEG_INF, el)
    in2 = in_grp & (e_lane != i1)
    v2 = jnp.max(el2, axis=-1, keepdims=True)
    i2 = jnp.min(jnp.where((el2 == v2) & in2, e_lane, 128), axis=-1, keepdims=True)
    e2 = jnp.exp(v2 - v1)
    w1 = g_w / (1.0 + e2)
    w2 = g_w * e2 / (1.0 + e2)
    out = jnp.where(lane == 0, i1.astype(F32), 0.0)
    out = jnp.where(lane == 1, i2.astype(F32), out)
    out = jnp.where(lane == 2, w1, out)
    out = jnp.where(lane == 3, w2, out)
    return out


MOE_TILE = 256


def _moe_tiles(n):
    return TOP_K * n // MOE_TILE + N_EXPERTS


def _expert_kernel(te_ref, ts_ref, tok_ref, nt_ref, h_hbm, w1_ref, w3_ref, w2_ref, y_ref,
                   xbuf0, xbuf1, sem, w1b, w3b, w2b):
    i = pl.program_id(0)
    n_tiles = nt_ref[0]
    last = tok_ref.shape[0] - 1
    bufs = (xbuf0, xbuf1)

    def gather_start(tile, slot):
        base = ts_ref[tile]
        for r in range(MOE_TILE):
            tok = tok_ref[jnp.minimum(base + r, last)]
            pltpu.make_async_copy(h_hbm.at[pl.ds(tok, 1)], bufs[slot].at[pl.ds(r, 1)], sem.at[slot]).start()

    def gather_wait(slot):
        pltpu.make_async_copy(bufs[slot], bufs[slot], sem.at[slot]).wait()

    @pl.when(i == 0)
    def _():
        gather_start(0, 0)

    prev_e = te_ref[jnp.maximum(i - 1, 0)]
    new_e = (i == 0) | (te_ref[i] != prev_e)

    @pl.when((i < n_tiles) & new_e)
    def _():
        w1b[...] = w1_ref[0].astype(BF16)
        w3b[...] = w3_ref[0].astype(BF16)
        w2b[...] = w2_ref[0].astype(BF16)

    for slot in range(2):
        @pl.when((i < n_tiles) & (i % 2 == slot))
        def _():
            gather_wait(slot)
            gather_start(jnp.minimum(i + 1, n_tiles - 1), 1 - slot)
            xt = bufs[slot][...].astype(BF16)
            a = jnp.dot(xt, w1b[...], preferred_element_type=F32)
            u = jnp.dot(xt, w3b[...], preferred_element_type=F32)
            act = (a * jax.nn.sigmoid(a)) * u
            y_ref[...] = jnp.dot(act.astype(BF16), w2b[...], preferred_element_type=F32)

            @pl.when(i == n_tiles - 1)
            def _():
                gather_wait(1 - slot)

    @pl.when(i >= n_tiles)
    def _():
        y_ref[...] = jnp.zeros_like(y_ref)


def expert_mlp(tile_expert, tile_start, sorted_tok, n_tiles, h2, w1, w3, w2):
    e_map = lambda i, te, ts, tok, nt: (te[i], 0, 0)
    moe_tiles = tile_expert.shape[0]
    grid_spec = pltpu.PrefetchScalarGridSpec(
        num_scalar_prefetch=4,
        grid=(moe_tiles,),
        in_specs=[
            pl.BlockSpec(memory_space=pl.ANY),
            pl.BlockSpec((1, D_MODEL, EXPERT_HIDDEN), e_map),
            pl.BlockSpec((1, D_MODEL, EXPERT_HIDDEN), e_map),
            pl.BlockSpec((1, EXPERT_HIDDEN, D_MODEL), e_map),
        ],
        out_specs=pl.BlockSpec((MOE_TILE, D_MODEL), lambda i, te, ts, tok, nt: (i, 0)),
        scratch_shapes=[
            pltpu.VMEM((MOE_TILE, D_MODEL), F32),
            pltpu.VMEM((MOE_TILE, D_MODEL), F32),
            pltpu.SemaphoreType.DMA((2,)),
            pltpu.VMEM((D_MODEL, EXPERT_HIDDEN), BF16),
            pltpu.VMEM((D_MODEL, EXPERT_HIDDEN), BF16),
            pltpu.VMEM((EXPERT_HIDDEN, D_MODEL), BF16),
        ],
    )
    return pl.pallas_call(
        _expert_kernel,
        grid_spec=grid_spec,
        out_shape=jax.ShapeDtypeStruct((moe_tiles * MOE_TILE, D_MODEL), F32),
        compiler_params=_cparams(("arbitrary",)),
        name="expert_mlp",
    )(tile_expert, tile_start, sorted_tok, n_tiles, h2, w1, w3, w2)


CMB_TILE = 256


def _combine_kernel(pos_ref, y_hbm, x_ref, route_ref, mod_ref, o_ref, ybuf0, ybuf1, sem, *, first_tile):
    i = pl.program_id(0)
    nt = pl.num_programs(0)
    bufs = (ybuf0, ybuf1)

    def gather_start(tile, slot):
        base = (tile + first_tile) * (CMB_TILE * TOP_K)
        for r in range(CMB_TILE):
            for k in range(TOP_K):
                p = pos_ref[base + r * TOP_K + k]
                pltpu.make_async_copy(y_hbm.at[pl.ds(p, 1)], bufs[slot].at[k, pl.ds(r, 1)], sem.at[slot]).start()

    def gather_wait(slot):
        pltpu.make_async_copy(bufs[slot], bufs[slot], sem.at[slot]).wait()

    @pl.when(i == 0)
    def _():
        gather_start(0, 0)

    for slot in range(2):
        @pl.when(i % 2 == slot)
        def _():
            gather_wait(slot)
            gather_start(jnp.minimum(i + 1, nt - 1), 1 - slot)
            is_ctx = _is_ctx_rows((i + first_tile) * CMB_TILE, CMB_TILE)
            gate = jnp.where(is_ctx, mod_ref[0, 5:6, :], mod_ref[1, 5:6, :])
            w0 = route_ref[:, TOP_K:TOP_K + 1]
            w1 = route_ref[:, TOP_K + 1:TOP_K + 2]
            o_ref[...] = x_ref[...] + gate * (w0 * bufs[slot][0] + w1 * bufs[slot][1])

            @pl.when(i == nt - 1)
            def _():
                gather_wait(1 - slot)


def moe_combine(pos, y_sorted, xa, route, mod, skip_ctx):
    first_tile = CTX_LEN // CMB_TILE if skip_ctx else 0
    n = xa.shape[0] - first_tile * CMB_TILE
    grid_spec = pltpu.PrefetchScalarGridSpec(
        num_scalar_prefetch=1,
        grid=(n // CMB_TILE,),
        in_specs=[
            pl.BlockSpec(memory_space=pl.ANY),
            pl.BlockSpec((CMB_TILE, D_MODEL), lambda i, pos: (i + first_tile, 0)),
            pl.BlockSpec((CMB_TILE, 128), lambda i, pos: (i + first_tile, 0)),
            pl.BlockSpec((2, 8, D_MODEL), lambda i, pos: (0, 0, 0)),
        ],
        out_specs=pl.BlockSpec((CMB_TILE, D_MODEL), lambda i, pos: (i, 0)),
        scratch_shapes=[
            pltpu.VMEM((TOP_K, CMB_TILE, D_MODEL), F32),
            pltpu.VMEM((TOP_K, CMB_TILE, D_MODEL), F32),
            pltpu.SemaphoreType.DMA((2,)),
        ],
    )
    return pl.pallas_call(
        functools.partial(_combine_kernel, first_tile=first_tile),
        grid_spec=grid_spec,
        out_shape=jax.ShapeDtypeStruct((n, D_MODEL), F32),
        compiler_params=_cparams(("arbitrary",)),
        name="moe_combine",
    )(pos, y_sorted, xa, route, mod)


def moe_dispatch_plan(route):
    n = route.shape[0]
    eid = route[:, 0:TOP_K].astype(jnp.int32).reshape(-1)
    p_total = eid.shape[0]
    experts = jnp.arange(N_EXPERTS, dtype=jnp.int32)
    onehot = (eid[:, None] == experts[None, :]).astype(jnp.int32)
    csum = jnp.cumsum(onehot, axis=0)
    counts = csum[-1]
    rank = jnp.sum(csum * onehot, axis=1) - 1
    tiles_per = (counts + MOE_TILE - 1) // MOE_TILE
    tile_off = jnp.cumsum(tiles_per) - tiles_per
    off = jnp.cumsum(counts) - counts
    pos = jnp.sum(onehot * tile_off[None, :], axis=1) * MOE_TILE + rank
    _, sorted_pair = lax.sort((eid, jnp.arange(p_total, dtype=jnp.int32)), num_keys=1, is_stable=True)
    sorted_tok = sorted_pair // TOP_K
    moe_tiles = _moe_tiles(n)
    n_tiles = jnp.sum(tiles_per).astype(jnp.int32)
    tile_ids = jnp.arange(moe_tiles, dtype=jnp.int32)
    tile_expert = jnp.sum((tile_ids[:, None] >= (tile_off + tiles_per)[None, :]).astype(jnp.int32), axis=1)
    tile_expert = jnp.minimum(tile_expert, N_EXPERTS - 1)
    t_onehot = (tile_expert[:, None] == experts[None, :]).astype(jnp.int32)
    tile_start = (jnp.sum(t_onehot * off[None, :], axis=1)
                  + (tile_ids - jnp.sum(t_onehot * tile_off[None, :], axis=1)) * MOE_TILE)
    return tile_expert, tile_start, sorted_tok, n_tiles.reshape(1), pos


def moe_block(xa, h2, route, mod, w1, w3, w2, layer, skip_ctx):
    tile_expert, tile_start, sorted_tok, n_tiles, pos = moe_dispatch_plan(route)
    y_sorted = expert_mlp(tile_expert + layer * N_EXPERTS, tile_start, sorted_tok, n_tiles, h2, w1, w3, w2)
    return moe_combine(pos, y_sorted, xa, route, mod, skip_ctx)


QK_W = ATTN_W + ATTN_KV_W
QKV_W = QK_W + ATTN_KV_W
ROPE_F = ATTN_HEAD_DIM // 4


def _norm_rope_heads(x, nw, cos, sin, out_ref, n_heads, scale):
    tm, width = x.shape
    xw = x * nw
    lane = lax.broadcasted_iota(jnp.int32, (tm, width), 1)
    odd = (lane // ROPE_F) % 2 == 1
    xs = jnp.where(odd, pltpu.roll(xw, ROPE_F, 1), pltpu.roll(xw, width - ROPE_F, 1))
    for h in range(n_heads):
        sl = slice(h * ATTN_HEAD_DIM, (h + 1) * ATTN_HEAD_DIM)
        xh = x[:, sl]
        inv = lax.rsqrt(jnp.mean(xh * xh, axis=-1, keepdims=True) + EPS) * scale
        out_ref[h] = ((xw[:, sl] * cos + xs[:, sl] * sin) * inv).astype(out_ref.dtype)


def _attn_prep_kernel(zq_ref, zk_ref, zv_ref, qw_ref, kw_ref, cos_ref, sin_ref, q_ref, k_ref, v_ref):
    cos = cos_ref[...]
    sin = sin_ref[...]
    _norm_rope_heads(zq_ref[...], qw_ref[...], cos, sin, q_ref, ATTN_HEADS, ATTN_HEAD_DIM ** -0.5)
    _norm_rope_heads(zk_ref[...], kw_ref[...], cos, sin, k_ref, ATTN_KV_HEADS, 1.0)
    for h in range(ATTN_KV_HEADS):
        v_ref[h] = zv_ref[:, h * ATTN_HEAD_DIM:(h + 1) * ATTN_HEAD_DIM].astype(BF16)


def attn_prep(z, q_norm_w, k_norm_w, cos_t, sin_t):
    tm = 256
    n = z.shape[0]
    qw = jnp.tile(q_norm_w, ATTN_HEADS).reshape(1, ATTN_W)
    kw = jnp.tile(k_norm_w, ATTN_KV_HEADS).reshape(1, ATTN_KV_W)
    hm = lambda h: pl.BlockSpec((h, tm, ATTN_HEAD_DIM), lambda i: (0, i, 0))
    sds = lambda h: jax.ShapeDtypeStruct((h, n, ATTN_HEAD_DIM), BF16)
    return pl.pallas_call(
        _attn_prep_kernel,
        grid=(n // tm,),
        in_specs=[
            pl.BlockSpec((tm, ATTN_W), lambda i: (i, Z_Q // ATTN_W)),
            pl.BlockSpec((tm, ATTN_KV_W), lambda i: (i, Z_K // ATTN_KV_W)),
            pl.BlockSpec((tm, ATTN_KV_W), lambda i: (i, Z_V // ATTN_KV_W)),
            pl.BlockSpec((1, ATTN_W), lambda i: (0, 0)),
            pl.BlockSpec((1, ATTN_KV_W), lambda i: (0, 0)),
            pl.BlockSpec((tm, ATTN_HEAD_DIM), lambda i: (i, 0)),
            pl.BlockSpec((tm, ATTN_HEAD_DIM), lambda i: (i, 0)),
        ],
        out_specs=[hm(ATTN_HEADS), hm(ATTN_KV_HEADS), hm(ATTN_KV_HEADS)],
        out_shape=[sds(ATTN_HEADS), sds(ATTN_KV_HEADS), sds(ATTN_KV_HEADS)],
        compiler_params=_cparams(("parallel",)),
        name="attn_prep",
    )(z, z, z, qw, kw, cos_t, sin_t)


def _attn_kernel(sink_ref, q_ref, kp_ref, ko_ref, kn_ref, kc_ref, vp_ref, vo_ref, vn_ref, vc_ref, o_ref, *, n_blocks):
    i = pl.program_id(0)
    ctx_blocks = CTX_LEN // ATTN_BLOCK
    B = ATTN_BLOCK
    rows = ATTN_GROUP * B
    ncol = 3 * B + CTX_LEN
    r = lax.broadcasted_iota(jnp.int32, (rows, ncol), 0) % B
    c = lax.broadcasted_iota(jnp.int32, (rows, ncol), 1)
    lo = jnp.where(i > ctx_blocks, 0, B)
    hi = jnp.where(i < n_blocks - 1, 3 * B, 2 * B)
    hi = jnp.where(i < ctx_blocks, 0, hi)
    band = (c >= r) & (c <= r + 2 * WINDOW) & (c >= lo) & (c < hi)
    mask = band | (c >= 3 * B)
    grp = lax.broadcasted_iota(jnp.int32, (rows, 1), 0) // B
    for kh in range(ATTN_KV_HEADS):
        q3 = jnp.concatenate([q_ref[kh * ATTN_GROUP + g] for g in range(ATTN_GROUP)], axis=0)
        kcat = jnp.concatenate([kp_ref[kh], ko_ref[kh], kn_ref[kh], kc_ref[kh]], axis=0)
        vcat = jnp.concatenate([vp_ref[kh], vo_ref[kh], vn_ref[kh], vc_ref[kh]], axis=0)
        s = lax.dot_general(q3, kcat, (((1,), (1,)), ((), ())), preferred_element_type=F32)
        s = jnp.where(mask, s, NEG_INF)
        sink = jnp.zeros((rows, 1), F32)
        for g in range(ATTN_GROUP):
            sink = jnp.where(grp == g, sink_ref[kh * ATTN_GROUP + g], sink)
        m = jnp.maximum(jnp.max(s, axis=-1, keepdims=True), sink)
        p = jnp.exp(s - m)
        den = jnp.sum(p, axis=-1, keepdims=True) + jnp.exp(sink - m)
        o = jnp.dot(p.astype(BF16), vcat, preferred_element_type=F32) / den
        for g in range(ATTN_GROUP):
            h = kh * ATTN_GROUP + g
            o_ref[:, h * ATTN_HEAD_DIM:(h + 1) * ATTN_HEAD_DIM] = o[g * B:(g + 1) * B].astype(o_ref.dtype)


def attention(qh, kh, vh, sink):
    n = qh.shape[1]
    B = ATTN_BLOCK
    nblk = n // B
    cb = CTX_LEN // B
    prev = lambda i: (0, jnp.clip(i - 1, cb, nblk - 1), 0)
    own = lambda i: (0, i, 0)
    nxt = lambda i: (0, jnp.clip(i + 1, cb, nblk - 1), 0)
    ctx = lambda i: (0, 0, 0)
    kv = lambda m: pl.BlockSpec((ATTN_KV_HEADS, B, ATTN_HEAD_DIM), m)
    kvc = pl.BlockSpec((ATTN_KV_HEADS, CTX_LEN, ATTN_HEAD_DIM), ctx)
    return pl.pallas_call(
        functools.partial(_attn_kernel, n_blocks=nblk),
        grid=(nblk,),
        in_specs=[
            pl.BlockSpec(memory_space=pltpu.SMEM),
            pl.BlockSpec((ATTN_HEADS, B, ATTN_HEAD_DIM), own),
            kv(prev), kv(own), kv(nxt), kvc,
            kv(prev), kv(own), kv(nxt), kvc,
        ],
        out_specs=pl.BlockSpec((B, ATTN_W), lambda i: (i, 0)),
        out_shape=jax.ShapeDtypeStruct((n, ATTN_W), BF16),
        compiler_params=_cparams(("parallel",)),
        name="attention",
    )(sink, qh, kh, kh, kh, kh, vh, vh, vh, vh)


def rope_tables(n_lat):
    rows = n_lat // GRID_W
    row = jnp.repeat(jnp.arange(rows), GRID_W).astype(F32)
    col = jnp.tile(jnp.arange(GRID_W), rows).astype(F32)
    inv = ROPE_BASE ** (-jnp.arange(ROPE_F, dtype=F32) / ROPE_F)
    ar, ac = row[:, None] * inv, col[:, None] * inv
    cos = jnp.concatenate([jnp.cos(ar), jnp.cos(ar), jnp.cos(ac), jnp.cos(ac)], axis=1)
    sin = jnp.concatenate([-jnp.sin(ar), jnp.sin(ar), -jnp.sin(ac), jnp.sin(ac)], axis=1)
    cos = jnp.concatenate([jnp.ones((CTX_LEN, ATTN_HEAD_DIM), F32), cos], axis=0)
    sin = jnp.concatenate([jnp.zeros((CTX_LEN, ATTN_HEAD_DIM), F32), sin], axis=0)
    return cos, sin


def attn_mixer(z, q_norm_w, k_norm_w, sink, cos_t, sin_t):
    qh, kh, vh = attn_prep(z, q_norm_w, k_norm_w, cos_t, sin_t)
    return attention(qh, kh, vh, sink)


DN_QKV = 3 * DN_W
DN_HALO = 8
DN_DH = 2 * DN_HEADS
DN_C = DN_CHUNK


def _dn_prep_kernel(zm_ref, zp_ref, zn_ref, zg_ref, cw_ref, al_ref, dtb_ref, q_ref, k_ref, v_ref, g_ref, *, tm, n_tiles):
    i = pl.program_id(0)
    ctx_tiles = CTX_LEN // tm
    has_prev = (i != 0) & (i != ctx_tiles)
    has_next = (i != ctx_tiles - 1) & (i != n_tiles - 1)
    prev = jnp.where(has_prev, zp_ref[...], 0.0)
    nxt = jnp.where(has_next, zn_ref[...], 0.0)
    xcat = jnp.concatenate([prev, zm_ref[...], nxt], axis=0)
    half = DN_CONV // 2
    acc = None
    for t in range(DN_CONV):
        off = DN_HALO - half + t
        term = xcat[off:off + tm, :] * cw_ref[t:t + 1, :]
        acc = term if acc is None else acc + term
    y = acc * jax.nn.sigmoid(acc)
    for h in range(DN_HEADS):
        qh = y[:, h * DN_HEAD_DIM:(h + 1) * DN_HEAD_DIM]
        kh = y[:, DN_W + h * DN_HEAD_DIM:DN_W + (h + 1) * DN_HEAD_DIM]
        q_ref[h] = qh * (lax.rsqrt(jnp.sum(qh * qh, axis=-1, keepdims=True) + EPS) * (DN_HEAD_DIM ** -0.5))
        k_ref[h] = kh * lax.rsqrt(jnp.sum(kh * kh, axis=-1, keepdims=True) + EPS)
        v_ref[h] = y[:, 2 * DN_W + h * DN_HEAD_DIM:2 * DN_W + (h + 1) * DN_HEAD_DIM]
    zg = zg_ref[...]
    lane = lax.broadcasted_iota(jnp.int32, zg.shape, 1)
    beta = jax.nn.sigmoid(zg)
    alpha = pltpu.roll(zg, 128 - DN_DH, 1)
    gl = -jnp.exp(al_ref[...]) * jax.nn.softplus(alpha + dtb_ref[...])
    g_ref[...] = jnp.where(lane < DN_DH, beta, pltpu.roll(gl, DN_DH, 1))


def dn_prep(z, conv_w, a_log, dt_bias):
    tm = 256
    n = z.shape[0]
    n_tiles = n // tm
    hb = tm // DN_HALO
    cw = jnp.zeros((8, DN_QKV), F32).at[0:DN_CONV].set(conv_w)
    pad_row = lambda t: jnp.zeros((1, 128), F32).at[0, 0:DN_DH].set(t.reshape(-1))
    hm = pl.BlockSpec((DN_HEADS, tm, DN_HEAD_DIM), lambda i: (0, i, 0))
    sds = jax.ShapeDtypeStruct((DN_HEADS, n, DN_HEAD_DIM), F32)
    return pl.pallas_call(
        functools.partial(_dn_prep_kernel, tm=tm, n_tiles=n_tiles),
        grid=(n_tiles,),
        in_specs=[
            pl.BlockSpec((tm, DN_QKV), lambda i: (i, 0)),
            pl.BlockSpec((DN_HALO, DN_QKV), lambda i: (jnp.maximum(i * hb - 1, 0), 0)),
            pl.BlockSpec((DN_HALO, DN_QKV), lambda i: (jnp.minimum((i + 1) * hb, n // DN_HALO - 1), 0)),
            pl.BlockSpec((tm, 128), lambda i: (i, Z_GATES // 128)),
            pl.BlockSpec((8, DN_QKV), lambda i: (0, 0)),
            pl.BlockSpec((1, 128), lambda i: (0, 0)),
            pl.BlockSpec((1, 128), lambda i: (0, 0)),
        ],
        out_specs=[hm, hm, hm, pl.BlockSpec((tm, 128), lambda i: (i, 0))],
        out_shape=[sds, sds, sds, jax.ShapeDtypeStruct((n, 128), F32)],
        compiler_params=_cparams(("parallel",)),
        name="dn_prep",
    )(z, z, z, z, cw, pad_row(a_log), pad_row(dt_bias))


def _bdot(a, b):
    return jnp.dot(a.astype(BF16), b.astype(BF16), preferred_element_type=F32)


def _unit_tri_inverse_many(nmats, row, col):
    eye = (row == col).astype(F32)
    same = lambda b: (row // b) == (col // b)
    d1 = [jnp.where(same(8), m, 0.0) for m in nmats]
    d2 = [_bdot(a, a) for a in d1]
    d3 = [_bdot(a, b) for a, b in zip(d1, d2)]
    d4 = [_bdot(b, b) for b in d2]
    x = [eye + a + b + c for a, b, c in zip(d1, d2, d3)]
    t = [a + _bdot(a, b) for a, b in zip(x, d4)]
    for b in (8, 16, 32):
        sel = same(2 * b) & jnp.logical_not(same(b))
        tmp = [_bdot(jnp.where(sel, m, 0.0), a) for m, a in zip(nmats, t)]
        t = [a + _bdot(a, c) for a, c in zip(t, tmp)]
    return t


def _dn_chunk_kernel(q_ref, k_ref, v_ref, g_ref, u_ref, wq_ref, kgt_ref, aqk_ref, gl_ref):
    C = DN_C
    row = lax.broadcasted_iota(jnp.int32, (C, C), 0)
    col = lax.broadcasted_iota(jnp.int32, (C, C), 1)
    g_all = g_ref[...]
    lane = lax.broadcasted_iota(jnp.int32, (C, 128), 1)
    gpart = jnp.where((lane >= DN_DH) & (lane < 2 * DN_DH), g_all, 0.0)
    lower = (row >= col).astype(F32)
    upper = (row <= col).astype(F32)
    csum_f = jnp.dot(lower, gpart, preferred_element_type=F32, precision=HIGHEST)
    csum_r = jnp.dot(upper, gpart, preferred_element_type=F32, precision=HIGHEST)
    gc_all = jnp.where(lane < DN_DH + DN_HEADS, csum_f, csum_r)
    gc_t = jnp.transpose(gc_all)
    tot = jnp.sum(gpart, axis=0, keepdims=True)
    gl_ref[0] = jnp.broadcast_to(jnp.exp(tot), (8, 128))
    pairs = [(d, h) for d in range(2) for h in range(DN_HEADS)]
    qs = [q_ref[h] for h in range(DN_HEADS)]
    ks = [k_ref[h] for h in range(DN_HEADS)]
    kbs, egcs, decays, kks = [], [], [], []
    for d, h in pairs:
        j = d * DN_HEADS + h
        incl = (row >= col) if d == 0 else (row <= col)
        gc_col = gc_all[:, DN_DH + j:DN_DH + j + 1]
        gc_row = gc_t[DN_DH + j:DN_DH + j + 1, :]
        decays.append(jnp.where(incl, jnp.exp(jnp.where(incl, gc_col - gc_row, 0.0)), 0.0))
        egcs.append(jnp.exp(gc_col))
        kbs.append(ks[h] * g_all[:, j:j + 1])
    for (d, h), kb in zip(pairs, kbs):
        kks.append(lax.dot_general(jnp.concatenate([kb, qs[h]], axis=0).astype(BF16), ks[h].astype(BF16),
                                   (((1,), (1,)), ((), ())), preferred_element_type=F32))
    nmats = []
    for (d, h), kk, decay in zip(pairs, kks, decays):
        strict = (row > col) if d == 0 else (row < col)
        nmats.append(jnp.where(strict, -(kk[:C] * decay), 0.0))
    tinv = _unit_tri_inverse_many(nmats, row, col)
    sols = []
    for (d, h), t, kb, egc in zip(pairs, tinv, kbs, egcs):
        j = d * DN_HEADS + h
        rhs = jnp.concatenate([v_ref[h] * g_all[:, j:j + 1], kb * egc], axis=1)
        sols.append(_bdot(t, rhs))
    for (d, h), sol, kk, decay, egc in zip(pairs, sols, kks, decays, egcs):
        j = d * DN_HEADS + h
        incl = (row >= col) if d == 0 else (row <= col)
        gc_col = gc_all[:, DN_DH + j:DN_DH + j + 1]
        tot_j = tot[:, DN_DH + j:DN_DH + j + 1]
        u_ref[j] = sol[:, :DN_HEAD_DIM]
        wq_ref[j] = jnp.concatenate([sol[:, DN_HEAD_DIM:], qs[h] * egc], axis=0).astype(BF16)
        kgt_ref[j] = jnp.transpose(ks[h] * jnp.exp(tot_j - gc_col)).astype(BF16)
        aqk_ref[j] = jnp.where(incl, kk[C:] * decay, 0.0).astype(BF16)


def dn_chunk(qn, kn, vn, gates):
    n = qn.shape[1]
    nc = n // DN_C
    hm = pl.BlockSpec((DN_HEADS, DN_C, DN_HEAD_DIM), lambda c: (0, c, 0))
    sds = jax.ShapeDtypeStruct
    return pl.pallas_call(
        _dn_chunk_kernel,
        grid=(nc,),
        in_specs=[hm, hm, hm, pl.BlockSpec((DN_C, 128), lambda c: (c, 0))],
        out_specs=[
            pl.BlockSpec((DN_DH, DN_C, DN_HEAD_DIM), lambda c: (0, c, 0)),
            pl.BlockSpec((DN_DH, 2 * DN_C, DN_HEAD_DIM), lambda c: (0, c, 0)),
            pl.BlockSpec((DN_DH, DN_HEAD_DIM, DN_C), lambda c: (0, c, 0)),
            pl.BlockSpec((DN_DH, DN_C, DN_C), lambda c: (0, c, 0)),
            pl.BlockSpec((1, 8, 128), lambda c: (c, 0, 0)),
        ],
        out_shape=[
            sds((DN_DH, n, DN_HEAD_DIM), F32),
            sds((DN_DH, 2 * n, DN_HEAD_DIM), BF16),
            sds((DN_DH, nc * DN_HEAD_DIM, DN_C), BF16),
            sds((DN_DH, n, DN_C), BF16),
            sds((nc, 8, 128), F32),
        ],
        compiler_params=_cparams(("parallel",)),
        name="dn_chunk",
    )(qn, kn, vn, gates)


def _dn_scan_kernel(uf_ref, wqf_ref, kgf_ref, aqf_ref, glf_ref, ur_ref, wqr_ref, kgr_ref, aqr_ref, glr_ref,
                    of_ref, or_ref, state):
    @pl.when(pl.program_id(0) == 0)
    def _():
        state[...] = jnp.zeros_like(state)

    C = DN_C
    sets = ((uf_ref, wqf_ref, kgf_ref, aqf_ref, glf_ref, of_ref), (ur_ref, wqr_ref, kgr_ref, aqr_ref, glr_ref, or_ref))
    chains = [(d, h) + sets[d] for d in range(2) for h in range(DN_HEADS)]
    ss = [state[d * DN_HEADS + h] for d, h, *_ in chains]
    ts = [jnp.dot(wq_ref[h], s.astype(BF16), preferred_element_type=F32)
          for (d, h, u_ref, wq_ref, *_), s in zip(chains, ss)]
    vbs = [(u_ref[h] - t[:C]).astype(BF16) for (d, h, u_ref, *_), t in zip(chains, ts)]
    os_ = [t[C:] + jnp.dot(aq_ref[h], vb, preferred_element_type=F32)
           for (d, h, u_ref, wq_ref, kg_ref, aq_ref, *_), t, vb in zip(chains, ts, vbs)]
    ns = [s * gl_ref[0][0:1, DN_DH + d * DN_HEADS + h:DN_DH + d * DN_HEADS + h + 1]
          + jnp.dot(kg_ref[h], vb, preferred_element_type=F32)
          for (d, h, u_ref, wq_ref, kg_ref, aq_ref, gl_ref, o_ref), s, vb in zip(chains, ss, vbs)]
    for (d, h, *_, o_ref), o, s_new in zip(chains, os_, ns):
        o_ref[h] = o
        state[d * DN_HEADS + h] = s_new


def dn_scan(u, wq, kgt, aqk, gl):
    n = u.shape[1]
    nc = n // DN_C
    cc = CTX_LEN // DN_C
    fwd = lambda s: s
    rev = lambda s: jnp.where(s < cc, cc - 1 - s, nc + cc - 1 - s)
    specs = []
    for d, cm in enumerate((fwd, rev)):
        specs += [
            pl.BlockSpec((DN_HEADS, DN_C, DN_HEAD_DIM), lambda s, d=d, cm=cm: (d, cm(s), 0)),
            pl.BlockSpec((DN_HEADS, 2 * DN_C, DN_HEAD_DIM), lambda s, d=d, cm=cm: (d, cm(s), 0)),
            pl.BlockSpec((DN_HEADS, DN_HEAD_DIM, DN_C), lambda s, d=d, cm=cm: (d, cm(s), 0)),
            pl.BlockSpec((DN_HEADS, DN_C, DN_C), lambda s, d=d, cm=cm: (d, cm(s), 0)),
            pl.BlockSpec((1, 8, 128), lambda s, cm=cm: (cm(s), 0, 0)),
        ]
    osd = jax.ShapeDtypeStruct((DN_HEADS, n, DN_HEAD_DIM), F32)
    return pl.pallas_call(
        _dn_scan_kernel,
        grid=(nc,),
        in_specs=specs,
        out_specs=[pl.BlockSpec((DN_HEADS, DN_C, DN_HEAD_DIM), lambda s: (0, fwd(s), 0)),
                   pl.BlockSpec((DN_HEADS, DN_C, DN_HEAD_DIM), lambda s: (0, rev(s), 0))],
        out_shape=[osd, osd],
        scratch_shapes=[pltpu.VMEM((DN_DH, DN_HEAD_DIM, DN_HEAD_DIM), F32)],
        compiler_params=_cparams(("arbitrary",)),
        name="dn_scan",
    )(u, wq, kgt, aqk, gl, u, wq, kgt, aqk, gl)


def dn_mixer(z, conv_w, a_log, dt_bias):
    qn, kn, vn, gates = dn_prep(z, conv_w, a_log, dt_bias)
    u, wq, kgt, aqk, gl = dn_chunk(qn, kn, vn, gates)
    return dn_scan(u, wq, kgt, aqk, gl)


S5_SUB = 8
S5_BLK = S5_GROUPS * S5_STATE // S5_SUB
S5_UB = S5_W // S5_SUB
S5_T = 128


def _s5_param_kernel(lr_ref, li_ref, ls_ref, br_ref, bi_ref, ar_ref, ai_ref, bbr_ref, bbi_ref):
    lr = jnp.minimum(lr_ref[...], -1e-4)
    li = li_ref[...]
    dt = jnp.exp(ls_ref[...])
    mag = jnp.exp(lr * dt)
    ar = mag * jnp.cos(li * dt)
    ai = mag * jnp.sin(li * dt)
    den = lr * lr + li * li
    nr, ni = ar - 1.0, ai
    fr = (nr * lr + ni * li) / den
    fi = (ni * lr - nr * li) / den
    ar_ref[...] = ar
    ai_ref[...] = ai
    for h in range(S5_GROUP_CH):
        bbr_ref[h] = fr * br_ref[h] - fi * bi_ref[h]
        bbi_ref[h] = fr * bi_ref[h] + fi * br_ref[h]


def s5_params(lam_re, lam_im, log_step, b_re, b_im):
    r = 2 * S5_GROUPS
    ls = jnp.broadcast_to(log_step.reshape(r, 1), (r, S5_STATE))
    bt = lambda t: jnp.transpose(t.reshape(r, S5_STATE, S5_GROUP_CH), (2, 0, 1))
    sds = jax.ShapeDtypeStruct
    return pl.pallas_call(
        _s5_param_kernel,
        out_shape=[sds((r, S5_STATE), F32), sds((r, S5_STATE), F32),
                   sds((S5_GROUP_CH, r, S5_STATE), F32), sds((S5_GROUP_CH, r, S5_STATE), F32)],
        name="s5_params",
    )(lam_re.reshape(r, S5_STATE), lam_im.reshape(r, S5_STATE), ls, bt(b_re), bt(b_im))


def _s5_scan_kernel(uf_ref, ur_ref, a_ref, rb_ref, cc_ref, yf_ref, yr_ref, bu, hh, st):
    rows = S5_T * S5_SUB

    @pl.when(pl.program_id(0) == 0)
    def _():
        st[...] = jnp.zeros_like(st)

    sub = lax.broadcasted_iota(jnp.int32, (rows, S5_W), 0) % S5_SUB
    blk = lax.broadcasted_iota(jnp.int32, (rows, S5_W), 1) // S5_UB
    own = sub == blk
    for d, u_ref in enumerate((uf_ref, ur_ref)):
        ue = jnp.broadcast_to(u_ref[...][:, None, :], (S5_T, S5_SUB, S5_W)).reshape(rows, S5_W)
        lhs = jnp.where(own, ue, 0.0).astype(BF16)
        bu[d, 0] = jnp.dot(lhs, rb_ref[d, 0], preferred_element_type=F32)
        bu[d, 1] = jnp.dot(lhs, rb_ref[d, 1], preferred_element_type=F32)

    a = [[a_ref[d, c] for c in range(2)] for d in range(2)]

    def body(t, carry):
        fr, fi, rr, ri = carry
        rf = pl.multiple_of(t * S5_SUB, S5_SUB)
        rv = pl.multiple_of((S5_T - 1 - t) * S5_SUB, S5_SUB)
        nfr = a[0][0] * fr - a[0][1] * fi + bu[0, 0, pl.ds(rf, S5_SUB), :]
        nfi = a[0][0] * fi + a[0][1] * fr + bu[0, 1, pl.ds(rf, S5_SUB), :]
        nrr = a[1][0] * rr - a[1][1] * ri + bu[1, 0, pl.ds(rv, S5_SUB), :]
        nri = a[1][0] * ri + a[1][1] * rr + bu[1, 1, pl.ds(rv, S5_SUB), :]
        hh[0, 0, pl.ds(rf, S5_SUB), :] = nfr
        hh[0, 1, pl.ds(rf, S5_SUB), :] = nfi
        hh[1, 0, pl.ds(rv, S5_SUB), :] = nrr
        hh[1, 1, pl.ds(rv, S5_SUB), :] = nri
        return nfr, nfi, nrr, nri

    fin = lax.fori_loop(0, S5_T, body, (st[0, 0], st[0, 1], st[1, 0], st[1, 1]), unroll=8)
    st[0, 0], st[0, 1], st[1, 0], st[1, 1] = fin

    for d, y_ref in enumerate((yf_ref, yr_ref)):
        ye = (jnp.dot(hh[d, 0].astype(BF16), cc_ref[d, 0], preferred_element_type=F32)
              + jnp.dot(hh[d, 1].astype(BF16), cc_ref[d, 1], preferred_element_type=F32))
        ye = jnp.where(own, ye, 0.0)
        y_ref[...] = jnp.sum(ye.reshape(S5_T, S5_SUB, S5_W), axis=1)


def s5_scan(z, a8, rb, cc):
    rows = S5_T * S5_SUB
    n = z.shape[0]
    nt = n // S5_T
    ct = CTX_LEN // S5_T
    ucol = Z_S5 // S5_W

    def rev_tile(i):
        return jnp.where(i < ct, ct - 1 - i, nt + ct - 1 - i)

    full = lambda shape: pl.BlockSpec(shape, lambda i: (0,) * len(shape))
    sds = jax.ShapeDtypeStruct((n, S5_W), F32)
    return pl.pallas_call(
        _s5_scan_kernel,
        grid=(nt,),
        in_specs=[
            pl.BlockSpec((S5_T, S5_W), lambda i: (i, ucol)),
            pl.BlockSpec((S5_T, S5_W), lambda i: (rev_tile(i), ucol)),
            full((2, 2, S5_SUB, S5_BLK)),
            full((2, 2, S5_W, S5_BLK)),
            full((2, 2, S5_BLK, S5_W)),
        ],
        out_specs=[pl.BlockSpec((S5_T, S5_W), lambda i: (i, 0)),
                   pl.BlockSpec((S5_T, S5_W), lambda i: (rev_tile(i), 0))],
        out_shape=[sds, sds],
        scratch_shapes=[pltpu.VMEM((2, 2, rows, S5_BLK), F32), pltpu.VMEM((2, 2, rows, S5_BLK), F32),
                        pltpu.VMEM((2, 2, S5_SUB, S5_BLK), F32)],
        compiler_params=_cparams(("arbitrary",)),
        name="s5_scan",
    )(z, z, a8, rb, cc)


def s5_mixer(z, lam_re, lam_im, log_step, b_re, b_im, c_re, c_im):
    ar, ai, bbr, bbi = s5_params(lam_re, lam_im, log_step, b_re, b_im)
    g_blk = jax.nn.one_hot(jnp.arange(S5_GROUPS) % (S5_GROUPS // S5_SUB), S5_GROUPS // S5_SUB, dtype=F32)

    def place_b(bb):
        bb = jnp.transpose(bb.reshape(S5_GROUP_CH, 2, S5_GROUPS, S5_STATE), (1, 2, 0, 3))
        return jnp.einsum('dghp,gj->dghjp', bb, g_blk).reshape(2, S5_W, S5_BLK)

    def place_c(cm):
        return jnp.einsum('dghp,gj->djpgh', cm, g_blk).reshape(2, S5_BLK, S5_W)

    rb = jnp.stack([place_b(bbr), place_b(bbi)], axis=1).astype(BF16)
    cc = jnp.stack([place_c(c_re), -place_c(c_im)], axis=1).astype(BF16)
    a8 = jnp.stack([ar.reshape(2, S5_SUB, S5_BLK), ai.reshape(2, S5_SUB, S5_BLK)], axis=1)
    return s5_scan(z, a8, rb, cc)


def _permute_w_in(w):
    n_gate = 4 * DN_HEADS
    attn = w[:, :QKV_W]
    dn = w[:, QKV_W:QKV_W + 4 * DN_W]
    gates = w[:, QKV_W + 4 * DN_W:QKV_W + 4 * DN_W + n_gate]
    s5 = w[:, QKV_W + 4 * DN_W + n_gate:]
    pad = jnp.zeros((w.shape[0], Z_S5 - Z_GATES - n_gate), w.dtype)
    return jnp.concatenate([dn, attn, gates, pad, s5], axis=1).astype(BF16)


def kernel(x, c, ctx, c_ctx, w_ada, b_ada, norm1_w, norm2_w, w_in, w_out, attn_q_norm, attn_k_norm, attn_sink,
           dn_conv, dn_a_log, dn_dt_bias, dn_o_norm, s5_lam_re, s5_lam_im, s5_log_step, s5_b_re, s5_b_im,
           s5_c_re, s5_c_im, s5_d, s5_w_glu, moe_w_grp, moe_b_grp, moe_w_rt, moe_b_rt, moe_w1, moe_w3, moe_w2):
    b, n, d = x.shape
    assert b == 1 and ctx.shape[1] == CTX_LEN and d == D_MODEL
    lc = CTX_LEN
    xa = jnp.concatenate([ctx[0], x[0]], axis=0)
    cvec = jnp.zeros((8, D_MODEL), F32).at[0].set(c_ctx).at[1].set(c[0])
    mods = adaln(cvec, w_ada, b_ada)
    cos_t, sin_t = rope_tables(n)
    w1_all = moe_w1.reshape(DEPTH * N_EXPERTS, D_MODEL, EXPERT_HIDDEN)
    w3_all = moe_w3.reshape(DEPTH * N_EXPERTS, D_MODEL, EXPERT_HIDDEN)
    w2_all = moe_w2.reshape(DEPTH * N_EXPERTS, EXPERT_HIDDEN, D_MODEL)
    for layer in range(DEPTH):
        mod = mods[layer, 0:2].reshape(2, 6, D_MODEL)
        mod = jnp.concatenate([mod, jnp.zeros((2, 2, D_MODEL), F32)], axis=1)
        z = in_proj(xa, norm1_w[layer], mod, _permute_w_in(w_in[layer]))
        mix_a = attn_mixer(z, attn_q_norm[layer], attn_k_norm[layer], attn_sink[layer], cos_t, sin_t)
        o_f, o_r = dn_mixer(z, dn_conv[layer], dn_a_log[layer], dn_dt_bias[layer])
        yf, yr = s5_mixer(z, s5_lam_re[layer], s5_lam_im[layer], s5_log_step[layer], s5_b_re[layer],
                          s5_b_im[layer], s5_c_re[layer], s5_c_im[layer])
        w_router = jnp.concatenate([moe_w_grp[layer], moe_w_rt[layer],
                                    jnp.zeros((D_MODEL, 128 - N_GROUPS - N_EXPERTS), F32)], axis=1)
        b_router = jnp.concatenate([moe_b_grp[layer], moe_b_rt[layer],
                                    jnp.zeros((128 - N_GROUPS - N_EXPERTS,), F32)]).reshape(1, 128)
        xa, h2, route = out_proj(mix_a, o_f, o_r, z, yf, yr, xa, mod, w_out[layer].astype(BF16), norm2_w[layer],
                                 w_router, b_router, dn_o_norm[layer], s5_d[layer], s5_w_glu[layer].astype(BF16))
        xa = moe_block(xa, h2, route, mod, w1_all, w3_all, w2_all, layer, skip_ctx=(layer == DEPTH - 1))
    return xa[None]
```

```python
import functools

import jax
import jax.numpy as jnp
import numpy as np
from jax import lax
from jax.experimental import pallas as pl
from jax.experimental.pallas import tpu as pltpu

F32 = jnp.float32
BF16 = jnp.bfloat16
HIGHEST = lax.Precision.HIGHEST

D_MODEL = 2048
SEQ = 8192
DEPTH = 2
GRID_W = 64
CTX_LEN = 256
N_ALL = CTX_LEN + SEQ
EPS = 1e-6
NEG_INF = -1e30

ATTN_HEADS = 12
ATTN_KV_HEADS = 4
ATTN_HEAD_DIM = 64
ATTN_GROUP = ATTN_HEADS // ATTN_KV_HEADS
ATTN_W = ATTN_HEADS * ATTN_HEAD_DIM
ATTN_KV_W = ATTN_KV_HEADS * ATTN_HEAD_DIM
WINDOW = 128
ATTN_BLOCK = 128
ROPE_BASE = 10000.0
DN_HEADS = 6
DN_HEAD_DIM = 128
DN_W = DN_HEADS * DN_HEAD_DIM
DN_CONV = 5
DN_CHUNK = 64
S5_W = D_MODEL - ATTN_W - DN_W
S5_GROUP_CH = 16
S5_GROUPS = S5_W // S5_GROUP_CH
S5_STATE = 64
N_GROUPS = 4
EXPERTS_PER_GROUP = 8
N_EXPERTS = N_GROUPS * EXPERTS_PER_GROUP
TOP_K = 2
EXPERT_HIDDEN = 512

Z_DNQ, Z_DNK, Z_DNV, Z_DNG = 0, 768, 1536, 2304
Z_Q, Z_K, Z_V = 3072, 3840, 4096
Z_GATES = 4352
Z_BETA, Z_ALPHA = Z_GATES, Z_GATES + 2 * DN_HEADS
Z_S5 = 4608
Z_W = 5120

VMEM_LIMIT = 56 * 1024 * 1024


def _cparams(sem, vmem=VMEM_LIMIT):
    return pltpu.CompilerParams(dimension_semantics=sem, vmem_limit_bytes=vmem)


def _adaln_kernel(c_ref, w_ref, b_ref, o_ref):
    cv = c_ref[...]
    s = cv * jax.nn.sigmoid(cv)
    o_ref[0] = jnp.dot(s, w_ref[0], preferred_element_type=F32, precision=HIGHEST) + b_ref[0]


def adaln(cvec, w_ada, b_ada):
    L = w_ada.shape[0]
    tn = 1024
    n6 = 6 * D_MODEL
    return pl.pallas_call(
        _adaln_kernel,
        grid=(L, n6 // tn),
        in_specs=[
            pl.BlockSpec((8, D_MODEL), lambda l, j: (0, 0)),
            pl.BlockSpec((1, D_MODEL, tn), lambda l, j: (l, 0, j)),
            pl.BlockSpec((1, 1, tn), lambda l, j: (l, 0, j)),
        ],
        out_specs=pl.BlockSpec((1, 8, tn), lambda l, j: (l, 0, j)),
        out_shape=jax.ShapeDtypeStruct((L, 8, n6), F32),
        compiler_params=_cparams(("parallel", "parallel")),
        name="adaln",
    )(cvec, w_ada, b_ada.reshape(L, 1, n6))


NORM_ROWS = 64


def _is_ctx_rows(base, rows):
    return base + lax.broadcasted_iota(jnp.int32, (rows, 1), 0) < CTX_LEN


def _row_is_ctx(tm):
    return _is_ctx_rows(pl.program_id(0) * tm, tm)


def _norm_mod(x, nw, mod_ref, shift_i, scale_i, is_ctx):
    ms = jnp.mean(x * x, axis=-1, keepdims=True)
    h = x * lax.rsqrt(ms + EPS) * nw
    sc = jnp.where(is_ctx, mod_ref[0, scale_i:scale_i + 1, :], mod_ref[1, scale_i:scale_i + 1, :])
    sh = jnp.where(is_ctx, mod_ref[0, shift_i:shift_i + 1, :], mod_ref[1, shift_i:shift_i + 1, :])
    return h * (1.0 + sc) + sh


def _in_proj_kernel(x_ref, nw_ref, mod_ref, w_ref, o_ref, h_scr, *, tm):
    @pl.when(pl.program_id(1) == 0)
    def _():
        def chunk(c, carry):
            r0 = pl.multiple_of(c * NORM_ROWS, NORM_ROWS)
            h = _norm_mod(x_ref[pl.ds(r0, NORM_ROWS), :], nw_ref[...], mod_ref, 0, 1,
                          _is_ctx_rows(pl.program_id(0) * tm + r0, NORM_ROWS))
            h_scr[pl.ds(r0, NORM_ROWS), :] = h.astype(BF16)
            return carry
        lax.fori_loop(0, tm // NORM_ROWS, chunk, 0)

    o_ref[...] = jnp.dot(h_scr[...], w_ref[...], preferred_element_type=F32)


def in_proj(xa, norm_w, mod, w_in_p):
    tm, tn = 512, 1024
    n = xa.shape[0]
    return pl.pallas_call(
        functools.partial(_in_proj_kernel, tm=tm),
        grid=(pl.cdiv(n, tm), Z_W // tn),
        in_specs=[
            pl.BlockSpec((tm, D_MODEL), lambda i, j: (i, 0)),
            pl.BlockSpec((1, D_MODEL), lambda i, j: (0, 0)),
            pl.BlockSpec((2, 8, D_MODEL), lambda i, j: (0, 0, 0)),
            pl.BlockSpec((D_MODEL, tn), lambda i, j: (0, j)),
        ],
        out_specs=pl.BlockSpec((tm, tn), lambda i, j: (i, j)),
        out_shape=jax.ShapeDtypeStruct((n, Z_W), F32),
        scratch_shapes=[pltpu.VMEM((tm, D_MODEL), BF16)],
        compiler_params=_cparams(("parallel", "arbitrary")),
        name="in_proj",
    )(xa, norm_w.reshape(1, D_MODEL), mod, w_in_p)


def _out_proj_kernel(a_ref, of_ref, or_ref, zg_ref, onw_ref, zu_ref, yf_ref, yr_ref, dsk_ref, wg_ref,
                     x_ref, mod_ref, w_ref, nw_ref, wr_ref, br_ref, xo_ref, h_ref, rt_ref, *, tm):
    is_ctx = _row_is_ctx(tm)
    b_parts = []
    for hd in range(DN_HEADS):
        o = of_ref[hd] + or_ref[hd]
        o = o * lax.rsqrt(jnp.mean(o * o, axis=-1, keepdims=True) + EPS) * onw_ref[...]
        g = zg_ref[:, hd * DN_HEAD_DIM:(hd + 1) * DN_HEAD_DIM]
        b_parts.append((o * (g * jax.nn.sigmoid(g))).astype(BF16))
    b = jnp.concatenate(b_parts, axis=1)
    ys = jax.nn.gelu(zu_ref[...] * dsk_ref[...] + yf_ref[...] + yr_ref[...])
    s = (ys * jax.nn.sigmoid(jnp.dot(ys.astype(BF16), wg_ref[...], preferred_element_type=F32))).astype(BF16)
    y = (jnp.dot(a_ref[...], w_ref[0:ATTN_W, :], preferred_element_type=F32)
         + jnp.dot(b, w_ref[ATTN_W:ATTN_W + DN_W, :], preferred_element_type=F32)
         + jnp.dot(s, w_ref[ATTN_W + DN_W:, :], preferred_element_type=F32))
    gate = jnp.where(is_ctx, mod_ref[0, 2:3, :], mod_ref[1, 2:3, :])
    xn = x_ref[...] + gate * y
    xo_ref[...] = xn
    h = _norm_mod(xn, nw_ref[...], mod_ref, 3, 4, is_ctx)
    h_ref[...] = h
    wr = wr_ref[...]
    h_hi = h.astype(BF16)
    h_lo = (h - h_hi.astype(F32)).astype(BF16)
    w_hi = wr.astype(BF16)
    w_lo = (wr - w_hi.astype(F32)).astype(BF16)
    lg = (jnp.dot(h_hi, w_hi, preferred_element_type=F32) + jnp.dot(h_hi, w_lo, preferred_element_type=F32)
          + jnp.dot(h_lo, w_hi, preferred_element_type=F32) + br_ref[...])
    rt_ref[...] = _route(lg)


def out_proj(mix_a, o_f, o_r, z, yf, yr, xa, mod, w_out_b, norm2_w, w_router, b_router, o_norm_w, d_skip, w_glu_b):
    tm = 256
    n = xa.shape[0]
    row = lambda i: (i, 0)
    const2 = lambda i: (0, 0)
    hm = pl.BlockSpec((DN_HEADS, tm, DN_HEAD_DIM), lambda i: (0, i, 0))
    return pl.pallas_call(
        functools.partial(_out_proj_kernel, tm=tm),
        grid=(n // tm,),
        in_specs=[
            pl.BlockSpec((tm, ATTN_W), row),
            hm, hm,
            pl.BlockSpec((tm, DN_W), lambda i: (i, Z_DNG // DN_W)),
            pl.BlockSpec((1, DN_HEAD_DIM), const2),
            pl.BlockSpec((tm, S5_W), lambda i: (i, Z_S5 // S5_W)),
            pl.BlockSpec((tm, S5_W), row),
            pl.BlockSpec((tm, S5_W), row),
            pl.BlockSpec((1, S5_W), const2),
            pl.BlockSpec((S5_W, S5_W), const2),
            pl.BlockSpec((tm, D_MODEL), row),
            pl.BlockSpec((2, 8, D_MODEL), lambda i: (0, 0, 0)),
            pl.BlockSpec((D_MODEL, D_MODEL), const2),
            pl.BlockSpec((1, D_MODEL), const2),
            pl.BlockSpec((D_MODEL, 128), const2),
            pl.BlockSpec((1, 128), const2),
        ],
        out_specs=[
            pl.BlockSpec((tm, D_MODEL), row),
            pl.BlockSpec((tm, D_MODEL), row),
            pl.BlockSpec((tm, 128), row),
        ],
        out_shape=[
            jax.ShapeDtypeStruct((n, D_MODEL), F32),
            jax.ShapeDtypeStruct((n, D_MODEL), F32),
            jax.ShapeDtypeStruct((n, 128), F32),
        ],
        compiler_params=_cparams(("parallel",)),
        name="out_proj",
    )(mix_a, o_f, o_r, z, o_norm_w.reshape(1, DN_HEAD_DIM), z, yf, yr, d_skip.reshape(1, S5_W), w_glu_b,
      xa, mod, w_out_b, norm2_w.reshape(1, D_MODEL), w_router, b_router)


def _route(lg):
    tm = lg.shape[0]
    lane = lax.broadcasted_iota(jnp.int32, (tm, 128), 1)
    is_g = lane < N_GROUPS
    gl = jnp.where(is_g, lg, NEG_INF)
    gmax = jnp.max(gl, axis=-1, keepdims=True)
    gidx = jnp.min(jnp.where((gl == gmax) & is_g, lane, 128), axis=-1, keepdims=True)
    gsum = jnp.sum(jnp.where(is_g, jnp.exp(gl - gmax), 0.0), axis=-1, keepdims=True)
    g_w = 1.0 / gsum
    e_lane = lane - N_GROUPS
    in_grp = (e_lane >= gidx * EXPERTS_PER_GROUP) & (e_lane < (gidx + 1) * EXPERTS_PER_GROUP)
    el = jnp.where(in_grp, lg, NEG_INF)
    v1 = jnp.max(el, axis=-1, keepdims=True)
    i1 = jnp.min(jnp.where((el == v1) & in_grp, e_lane, 128), axis=-1, keepdims=True)
    el2 = jnp.where(e_lane == i1, NEG_INF, el)
    in2 = in_grp & (e_lane != i1)
    v2 = jnp.max(el2, axis=-1, keepdims=True)
    i2 = jnp.min(jnp.where((el2 == v2) & in2, e_lane, 128), axis=-1, keepdims=True)
    e2 = jnp.exp(v2 - v1)
    w1 = g_w / (1.0 + e2)
    w2 = g_w * e2 / (1.0 + e2)
    out = jnp.where(lane == 0, i1.astype(F32), 0.0)
    out = jnp.where(lane == 1, i2.astype(F32), out)
    out = jnp.where(lane == 2, w1, out)
    out = jnp.where(lane == 3, w2, out)
    return out


MOE_TILE = 256


def _moe_tiles(n):
    return TOP_K * n // MOE_TILE + N_EXPERTS


def _expert_kernel(te_ref, ne_ref, ts_ref, tok_ref, nt_ref, h_hbm, w1_hbm, w3_hbm, w2_hbm, y_ref,
                   xbuf0, xbuf1, sem, w1s, w3s, w2s, wsem, w1b, w3b, w2b):
    i = pl.program_id(0)
    n_tiles = nt_ref[0]
    last = tok_ref.shape[0] - 1
    bufs = (xbuf0, xbuf1)

    def weight_copies(e):
        return (pltpu.make_async_copy(w1_hbm.at[e], w1s, wsem.at[0]),
                pltpu.make_async_copy(w3_hbm.at[e], w3s, wsem.at[1]),
                pltpu.make_async_copy(w2_hbm.at[e], w2s, wsem.at[2]))

    def gather_start(tile, slot):
        base = ts_ref[tile]
        for r in range(MOE_TILE):
            tok = tok_ref[jnp.minimum(base + r, last)]
            pltpu.make_async_copy(h_hbm.at[pl.ds(tok, 1)], bufs[slot].at[pl.ds(r, 1)], sem.at[slot]).start()

    def gather_wait(slot):
        pltpu.make_async_copy(bufs[slot], bufs[slot], sem.at[slot]).wait()

    @pl.when(i == 0)
    def _():
        gather_start(0, 0)
        for cp in weight_copies(te_ref[0]):
            cp.start()

    prev_e = te_ref[jnp.maximum(i - 1, 0)]
    new_e = (i == 0) | (te_ref[i] != prev_e)

    @pl.when((i < n_tiles) & new_e)
    def _():
        for cp in weight_copies(te_ref[i]):
            cp.wait()
        w1b[...] = w1s[...].astype(BF16)
        w3b[...] = w3s[...].astype(BF16)
        w2b[...] = w2s[...].astype(BF16)

        @pl.when(ne_ref[i] >= 0)
        def _():
            for cp in weight_copies(ne_ref[i]):
                cp.start()

    for slot in range(2):
        @pl.when((i < n_tiles) & (i % 2 == slot))
        def _():
            gather_wait(slot)
            gather_start(jnp.minimum(i + 1, n_tiles - 1), 1 - slot)
            xt = bufs[slot][...].astype(BF16)
            a = jnp.dot(xt, w1b[...], preferred_element_type=F32)
            u = jnp.dot(xt, w3b[...], preferred_element_type=F32)
            act = (a * jax.nn.sigmoid(a)) * u
            y_ref[...] = jnp.dot(act.astype(BF16), w2b[...], preferred_element_type=F32)

            @pl.when(i == n_tiles - 1)
            def _():
                gather_wait(1 - slot)

    @pl.when(i >= n_tiles)
    def _():
        y_ref[...] = jnp.zeros_like(y_ref)


def expert_mlp(tile_expert, next_expert, tile_start, sorted_tok, n_tiles, h2, w1, w3, w2):
    moe_tiles = tile_expert.shape[0]
    any_spec = pl.BlockSpec(memory_space=pl.ANY)
    grid_spec = pltpu.PrefetchScalarGridSpec(
        num_scalar_prefetch=5,
        grid=(moe_tiles,),
        in_specs=[any_spec, any_spec, any_spec, any_spec],
        out_specs=pl.BlockSpec((MOE_TILE, D_MODEL), lambda i, te, ne, ts, tok, nt: (i, 0)),
        scratch_shapes=[
            pltpu.VMEM((MOE_TILE, D_MODEL), F32),
            pltpu.VMEM((MOE_TILE, D_MODEL), F32),
            pltpu.SemaphoreType.DMA((2,)),
            pltpu.VMEM((D_MODEL, EXPERT_HIDDEN), F32),
            pltpu.VMEM((D_MODEL, EXPERT_HIDDEN), F32),
            pltpu.VMEM((EXPERT_HIDDEN, D_MODEL), F32),
            pltpu.SemaphoreType.DMA((3,)),
            pltpu.VMEM((D_MODEL, EXPERT_HIDDEN), BF16),
            pltpu.VMEM((D_MODEL, EXPERT_HIDDEN), BF16),
            pltpu.VMEM((EXPERT_HIDDEN, D_MODEL), BF16),
        ],
    )
    return pl.pallas_call(
        _expert_kernel,
        grid_spec=grid_spec,
        out_shape=jax.ShapeDtypeStruct((moe_tiles * MOE_TILE, D_MODEL), F32),
        compiler_params=_cparams(("arbitrary",)),
        name="expert_mlp",
    )(tile_expert, next_expert, tile_start, sorted_tok, n_tiles, h2, w1, w3, w2)


CMB_TILE = 256


def _combine_kernel(pos_ref, y_hbm, x_ref, route_ref, mod_ref, o_ref, ybuf0, ybuf1, sem, *, first_tile):
    i = pl.program_id(0)
    nt = pl.num_programs(0)
    bufs = (ybuf0, ybuf1)

    def gather_start(tile, slot):
        base = (tile + first_tile) * (CMB_TILE * TOP_K)
        for r in range(CMB_TILE):
            for k in range(TOP_K):
                p = pos_ref[base + r * TOP_K + k]
                pltpu.make_async_copy(y_hbm.at[pl.ds(p, 1)], bufs[slot].at[k, pl.ds(r, 1)], sem.at[slot]).start()

    def gather_wait(slot):
        pltpu.make_async_copy(bufs[slot], bufs[slot], sem.at[slot]).wait()

    @pl.when(i == 0)
    def _():
        gather_start(0, 0)

    for slot in range(2):
        @pl.when(i % 2 == slot)
        def _():
            gather_wait(slot)
            gather_start(jnp.minimum(i + 1, nt - 1), 1 - slot)
            is_ctx = _is_ctx_rows((i + first_tile) * CMB_TILE, CMB_TILE)
            gate = jnp.where(is_ctx, mod_ref[0, 5:6, :], mod_ref[1, 5:6, :])
            w0 = route_ref[:, TOP_K:TOP_K + 1]
            w1 = route_ref[:, TOP_K + 1:TOP_K + 2]
            o_ref[...] = x_ref[...] + gate * (w0 * bufs[slot][0] + w1 * bufs[slot][1])

            @pl.when(i == nt - 1)
            def _():
                gather_wait(1 - slot)


def moe_combine(pos, y_sorted, xa, route, mod, skip_ctx):
    first_tile = CTX_LEN // CMB_TILE if skip_ctx else 0
    n = xa.shape[0] - first_tile * CMB_TILE
    grid_spec = pltpu.PrefetchScalarGridSpec(
        num_scalar_prefetch=1,
        grid=(n // CMB_TILE,),
        in_specs=[
            pl.BlockSpec(memory_space=pl.ANY),
            pl.BlockSpec((CMB_TILE, D_MODEL), lambda i, pos: (i + first_tile, 0)),
            pl.BlockSpec((CMB_TILE, 128), lambda i, pos: (i + first_tile, 0)),
            pl.BlockSpec((2, 8, D_MODEL), lambda i, pos: (0, 0, 0)),
        ],
        out_specs=pl.BlockSpec((CMB_TILE, D_MODEL), lambda i, pos: (i, 0)),
        scratch_shapes=[
            pltpu.VMEM((TOP_K, CMB_TILE, D_MODEL), F32),
            pltpu.VMEM((TOP_K, CMB_TILE, D_MODEL), F32),
            pltpu.SemaphoreType.DMA((2,)),
        ],
    )
    return pl.pallas_call(
        functools.partial(_combine_kernel, first_tile=first_tile),
        grid_spec=grid_spec,
        out_shape=jax.ShapeDtypeStruct((n, D_MODEL), F32),
        compiler_params=_cparams(("arbitrary",)),
        name="moe_combine",
    )(pos, y_sorted, xa, route, mod)


def moe_dispatch_plan(route):
    n = route.shape[0]
    eid = route[:, 0:TOP_K].astype(jnp.int32).reshape(-1)
    p_total = eid.shape[0]
    experts = jnp.arange(N_EXPERTS, dtype=jnp.int32)
    counts = jnp.sum((eid[:, None] == experts[None, :]).astype(jnp.int32), axis=0)
    tiles_per = (counts + MOE_TILE - 1) // MOE_TILE
    tile_off = jnp.cumsum(tiles_per) - tiles_per
    off = jnp.cumsum(counts) - counts
    pair_ids = jnp.arange(p_total, dtype=jnp.int32)
    e_sorted, sorted_pair = lax.sort((eid, pair_ids), num_keys=1, is_stable=True)
    sorted_tok = sorted_pair // TOP_K
    s_onehot = (e_sorted[:, None] == experts[None, :]).astype(jnp.int32)
    pos_sorted = pair_ids + jnp.sum(s_onehot * (tile_off * MOE_TILE - off)[None, :], axis=1)
    _, pos = lax.sort((sorted_pair, pos_sorted), num_keys=1)
    moe_tiles = _moe_tiles(n)
    n_tiles = jnp.sum(tiles_per).astype(jnp.int32)
    tile_ids = jnp.arange(moe_tiles, dtype=jnp.int32)
    tile_expert = jnp.sum((tile_ids[:, None] >= (tile_off + tiles_per)[None, :]).astype(jnp.int32), axis=1)
    tile_expert = jnp.minimum(tile_expert, N_EXPERTS - 1)
    t_onehot = (tile_expert[:, None] == experts[None, :]).astype(jnp.int32)
    tile_start = (jnp.sum(t_onehot * off[None, :], axis=1)
                  + (tile_ids - jnp.sum(t_onehot * tile_off[None, :], axis=1)) * MOE_TILE)
    group_end = jnp.sum(t_onehot * (tile_off + tiles_per)[None, :], axis=1)
    next_expert = jnp.sum((tile_ids[None, :] == group_end[:, None]).astype(jnp.int32) * tile_expert[None, :], axis=1)
    next_expert = jnp.where(group_end < n_tiles, next_expert, -1)
    return tile_expert, next_expert, tile_start, sorted_tok, n_tiles.reshape(1), pos


def moe_block(xa, h2, route, mod, w1, w3, w2, layer, skip_ctx):
    tile_expert, next_expert, tile_start, sorted_tok, n_tiles, pos = moe_dispatch_plan(route)
    base = layer * N_EXPERTS
    next_expert = jnp.where(next_expert >= 0, next_expert + base, -1)
    y_sorted = expert_mlp(tile_expert + base, next_expert, tile_start, sorted_tok, n_tiles, h2, w1, w3, w2)
    return moe_combine(pos, y_sorted, xa, route, mod, skip_ctx)


QK_W = ATTN_W + ATTN_KV_W
QKV_W = QK_W + ATTN_KV_W
ROPE_F = ATTN_HEAD_DIM // 4


def _norm_rope_heads(x, nw, cos, sin, out_ref, n_heads, scale):
    tm, width = x.shape
    xw = x * nw
    lane = lax.broadcasted_iota(jnp.int32, (tm, width), 1)
    odd = (lane // ROPE_F) % 2 == 1
    xs = jnp.where(odd, pltpu.roll(xw, ROPE_F, 1), pltpu.roll(xw, width - ROPE_F, 1))
    for h in range(n_heads):
        sl = slice(h * ATTN_HEAD_DIM, (h + 1) * ATTN_HEAD_DIM)
        xh = x[:, sl]
        inv = lax.rsqrt(jnp.mean(xh * xh, axis=-1, keepdims=True) + EPS) * scale
        out_ref[h] = ((xw[:, sl] * cos + xs[:, sl] * sin) * inv).astype(out_ref.dtype)


def _attn_prep_kernel(zq_ref, zk_ref, zv_ref, qw_ref, kw_ref, cos_ref, sin_ref, q_ref, k_ref, v_ref):
    cos = cos_ref[...]
    sin = sin_ref[...]
    _norm_rope_heads(zq_ref[...], qw_ref[...], cos, sin, q_ref, ATTN_HEADS, ATTN_HEAD_DIM ** -0.5)
    _norm_rope_heads(zk_ref[...], kw_ref[...], cos, sin, k_ref, ATTN_KV_HEADS, 1.0)
    for h in range(ATTN_KV_HEADS):
        v_ref[h] = zv_ref[:, h * ATTN_HEAD_DIM:(h + 1) * ATTN_HEAD_DIM].astype(BF16)


def attn_prep(z, q_norm_w, k_norm_w, cos_t, sin_t):
    tm = 256
    n = z.shape[0]
    qw = jnp.tile(q_norm_w, ATTN_HEADS).reshape(1, ATTN_W)
    kw = jnp.tile(k_norm_w, ATTN_KV_HEADS).reshape(1, ATTN_KV_W)
    hm = lambda h: pl.BlockSpec((h, tm, ATTN_HEAD_DIM), lambda i: (0, i, 0))
    sds = lambda h: jax.ShapeDtypeStruct((h, n, ATTN_HEAD_DIM), BF16)
    return pl.pallas_call(
        _attn_prep_kernel,
        grid=(n // tm,),
        in_specs=[
            pl.BlockSpec((tm, ATTN_W), lambda i: (i, Z_Q // ATTN_W)),
            pl.BlockSpec((tm, ATTN_KV_W), lambda i: (i, Z_K // ATTN_KV_W)),
            pl.BlockSpec((tm, ATTN_KV_W), lambda i: (i, Z_V // ATTN_KV_W)),
            pl.BlockSpec((1, ATTN_W), lambda i: (0, 0)),
            pl.BlockSpec((1, ATTN_KV_W), lambda i: (0, 0)),
            pl.BlockSpec((tm, ATTN_HEAD_DIM), lambda i: (i, 0)),
            pl.BlockSpec((tm, ATTN_HEAD_DIM), lambda i: (i, 0)),
        ],
        out_specs=[hm(ATTN_HEADS), hm(ATTN_KV_HEADS), hm(ATTN_KV_HEADS)],
        out_shape=[sds(ATTN_HEADS), sds(ATTN_KV_HEADS), sds(ATTN_KV_HEADS)],
        compiler_params=_cparams(("parallel",)),
        name="attn_prep",
    )(z, z, z, qw, kw, cos_t, sin_t)


def _attn_kernel(sink_ref, q_ref, kp_ref, ko_ref, kn_ref, kc_ref, vp_ref, vo_ref, vn_ref, vc_ref, o_ref, *, n_blocks):
    i = pl.program_id(0)
    ctx_blocks = CTX_LEN // ATTN_BLOCK
    B = ATTN_BLOCK
    rows = ATTN_GROUP * B
    ncol = 3 * B + CTX_LEN
    r = lax.broadcasted_iota(jnp.int32, (rows, ncol), 0) % B
    c = lax.broadcasted_iota(jnp.int32, (rows, ncol), 1)
    lo = jnp.where(i > ctx_blocks, 0, B)
    hi = jnp.where(i < n_blocks - 1, 3 * B, 2 * B)
    hi = jnp.where(i < ctx_blocks, 0, hi)
    band = (c >= r) & (c <= r + 2 * WINDOW) & (c >= lo) & (c < hi)
    mask = band | (c >= 3 * B)
    grp = lax.broadcasted_iota(jnp.int32, (rows, 1), 0) // B
    for kh in range(ATTN_KV_HEADS):
        q3 = jnp.concatenate([q_ref[kh * ATTN_GROUP + g] for g in range(ATTN_GROUP)], axis=0)
        kcat = jnp.concatenate([kp_ref[kh], ko_ref[kh], kn_ref[kh], kc_ref[kh]], axis=0)
        vcat = jnp.concatenate([vp_ref[kh], vo_ref[kh], vn_ref[kh], vc_ref[kh]], axis=0)
        s = lax.dot_general(q3, kcat, (((1,), (1,)), ((), ())), preferred_element_type=F32)
        s = jnp.where(mask, s, NEG_INF)
        sink = jnp.zeros((rows, 1), F32)
        for g in range(ATTN_GROUP):
            sink = jnp.where(grp == g, sink_ref[kh * ATTN_GROUP + g], sink)
        m = jnp.maximum(jnp.max(s, axis=-1, keepdims=True), sink)
        p = jnp.exp(s - m)
        den = jnp.sum(p, axis=-1, keepdims=True) + jnp.exp(sink - m)
        o = jnp.dot(p.astype(BF16), vcat, preferred_element_type=F32) / den
        for g in range(ATTN_GROUP):
            h = kh * ATTN_GROUP + g
            o_ref[:, h * ATTN_HEAD_DIM:(h + 1) * ATTN_HEAD_DIM] = o[g * B:(g + 1) * B].astype(o_ref.dtype)


def attention(qh, kh, vh, sink):
    n = qh.shape[1]
    B = ATTN_BLOCK
    nblk = n // B
    cb = CTX_LEN // B
    prev = lambda i: (0, jnp.clip(i - 1, cb, nblk - 1), 0)
    own = lambda i: (0, i, 0)
    nxt = lambda i: (0, jnp.clip(i + 1, cb, nblk - 1), 0)
    ctx = lambda i: (0, 0, 0)
    kv = lambda m: pl.BlockSpec((ATTN_KV_HEADS, B, ATTN_HEAD_DIM), m)
    kvc = pl.BlockSpec((ATTN_KV_HEADS, CTX_LEN, ATTN_HEAD_DIM), ctx)
    return pl.pallas_call(
        functools.partial(_attn_kernel, n_blocks=nblk),
        grid=(nblk,),
        in_specs=[
            pl.BlockSpec(memory_space=pltpu.SMEM),
            pl.BlockSpec((ATTN_HEADS, B, ATTN_HEAD_DIM), own),
            kv(prev), kv(own), kv(nxt), kvc,
            kv(prev), kv(own), kv(nxt), kvc,
        ],
        out_specs=pl.BlockSpec((B, ATTN_W), lambda i: (i, 0)),
        out_shape=jax.ShapeDtypeStruct((n, ATTN_W), BF16),
        compiler_params=_cparams(("parallel",)),
        name="attention",
    )(sink, qh, kh, kh, kh, kh, vh, vh, vh, vh)


def rope_tables(n_lat):
    rows = n_lat // GRID_W
    row = np.repeat(np.arange(rows), GRID_W).astype(np.float32)
    col = np.tile(np.arange(GRID_W), rows).astype(np.float32)
    inv = (ROPE_BASE ** (-np.arange(ROPE_F, dtype=np.float32) / ROPE_F)).astype(np.float32)
    ar, ac = row[:, None] * inv, col[:, None] * inv
    cos = np.concatenate([np.cos(ar), np.cos(ar), np.cos(ac), np.cos(ac)], axis=1)
    sin = np.concatenate([-np.sin(ar), np.sin(ar), -np.sin(ac), np.sin(ac)], axis=1)
    cos = np.concatenate([np.ones((CTX_LEN, ATTN_HEAD_DIM), np.float32), cos], axis=0)
    sin = np.concatenate([np.zeros((CTX_LEN, ATTN_HEAD_DIM), np.float32), sin], axis=0)
    return jnp.asarray(cos, F32), jnp.asarray(sin, F32)


def attn_mixer(z, q_norm_w, k_norm_w, sink, cos_t, sin_t):
    qh, kh, vh = attn_prep(z, q_norm_w, k_norm_w, cos_t, sin_t)
    return attention(qh, kh, vh, sink)


DN_QKV = 3 * DN_W
DN_HALO = 8
DN_DH = 2 * DN_HEADS
DN_C = DN_CHUNK
DN_CPS = 4


def _dn_prep_kernel(zm_ref, zp_ref, zn_ref, zg_ref, cw_ref, al_ref, dtb_ref, q_ref, k_ref, v_ref, g_ref, *, tm, n_tiles):
    i = pl.program_id(0)
    ctx_tiles = CTX_LEN // tm
    has_prev = (i != 0) & (i != ctx_tiles)
    has_next = (i != ctx_tiles - 1) & (i != n_tiles - 1)
    prev = jnp.where(has_prev, zp_ref[...], 0.0)
    nxt = jnp.where(has_next, zn_ref[...], 0.0)
    xcat = jnp.concatenate([prev, zm_ref[...], nxt], axis=0)
    half = DN_CONV // 2
    acc = None
    for t in range(DN_CONV):
        off = DN_HALO - half + t
        term = xcat[off:off + tm, :] * cw_ref[t:t + 1, :]
        acc = term if acc is None else acc + term
    y = acc * jax.nn.sigmoid(acc)
    for h in range(DN_HEADS):
        qh = y[:, h * DN_HEAD_DIM:(h + 1) * DN_HEAD_DIM]
        kh = y[:, DN_W + h * DN_HEAD_DIM:DN_W + (h + 1) * DN_HEAD_DIM]
        q_ref[h] = qh * (lax.rsqrt(jnp.sum(qh * qh, axis=-1, keepdims=True) + EPS) * (DN_HEAD_DIM ** -0.5))
        k_ref[h] = kh * lax.rsqrt(jnp.sum(kh * kh, axis=-1, keepdims=True) + EPS)
        v_ref[h] = y[:, 2 * DN_W + h * DN_HEAD_DIM:2 * DN_W + (h + 1) * DN_HEAD_DIM]
    zg = zg_ref[...]
    lane = lax.broadcasted_iota(jnp.int32, zg.shape, 1)
    beta = jax.nn.sigmoid(zg)
    alpha = pltpu.roll(zg, 128 - DN_DH, 1)
    gl = -jnp.exp(al_ref[...]) * jax.nn.softplus(alpha + dtb_ref[...])
    g_ref[...] = jnp.where(lane < DN_DH, beta, pltpu.roll(gl, DN_DH, 1))


def dn_prep(z, conv_w, a_log, dt_bias):
    tm = 256
    n = z.shape[0]
    n_tiles = n // tm
    hb = tm // DN_HALO
    cw = jnp.zeros((8, DN_QKV), F32).at[0:DN_CONV].set(conv_w)
    pad_row = lambda t: jnp.zeros((1, 128), F32).at[0, 0:DN_DH].set(t.reshape(-1))
    hm = pl.BlockSpec((DN_HEADS, tm, DN_HEAD_DIM), lambda i: (0, i, 0))
    sds = jax.ShapeDtypeStruct((DN_HEADS, n, DN_HEAD_DIM), F32)
    return pl.pallas_call(
        functools.partial(_dn_prep_kernel, tm=tm, n_tiles=n_tiles),
        grid=(n_tiles,),
        in_specs=[
            pl.BlockSpec((tm, DN_QKV), lambda i: (i, 0)),
            pl.BlockSpec((DN_HALO, DN_QKV), lambda i: (jnp.maximum(i * hb - 1, 0), 0)),
            pl.BlockSpec((DN_HALO, DN_QKV), lambda i: (jnp.minimum((i + 1) * hb, n // DN_HALO - 1), 0)),
            pl.BlockSpec((tm, 128), lambda i: (i, Z_GATES // 128)),
            pl.BlockSpec((8, DN_QKV), lambda i: (0, 0)),
            pl.BlockSpec((1, 128), lambda i: (0, 0)),
            pl.BlockSpec((1, 128), lambda i: (0, 0)),
        ],
        out_specs=[hm, hm, hm, pl.BlockSpec((tm, 128), lambda i: (i, 0))],
        out_shape=[sds, sds, sds, jax.ShapeDtypeStruct((n, 128), F32)],
        compiler_params=_cparams(("parallel",)),
        name="dn_prep",
    )(z, z, z, z, cw, pad_row(a_log), pad_row(dt_bias))


def _bdot(a, b):
    return jnp.dot(a.astype(BF16), b.astype(BF16), preferred_element_type=F32)


def _unit_tri_inverse_many(nmats, row, col):
    eye = (row == col).astype(F32)
    same = lambda b: (row // b) == (col // b)
    d1 = [jnp.where(same(8), m, 0.0) for m in nmats]
    d2 = [_bdot(a, a) for a in d1]
    d3 = [_bdot(a, b) for a, b in zip(d1, d2)]
    d4 = [_bdot(b, b) for b in d2]
    x = [eye + a + b + c for a, b, c in zip(d1, d2, d3)]
    t = [a + _bdot(a, b) for a, b in zip(x, d4)]
    for b in (8, 16, 32):
        sel = same(2 * b) & jnp.logical_not(same(b))
        tmp = [_bdot(jnp.where(sel, m, 0.0), a) for m, a in zip(nmats, t)]
        t = [a + _bdot(a, c) for a, c in zip(t, tmp)]
    return t


def _dn_chunk_kernel(q_ref, k_ref, v_ref, g_ref, u_ref, wq_ref, kgt_ref, aqk_ref, gl_ref):
    C = DN_C
    row = lax.broadcasted_iota(jnp.int32, (C, C), 0)
    col = lax.broadcasted_iota(jnp.int32, (C, C), 1)
    lane = lax.broadcasted_iota(jnp.int32, (C, 128), 1)
    lower = (row >= col).astype(F32)
    upper = (row <= col).astype(F32)
    inst = [(c, d, h) for c in range(DN_CPS) for d in range(2) for h in range(DN_HEADS)]
    rows = lambda c: slice(c * C, (c + 1) * C)
    g_all, gc_all, gc_t, tot = [], [], [], []
    for c in range(DN_CPS):
        g = g_ref[rows(c), :]
        gpart = jnp.where((lane >= DN_DH) & (lane < 2 * DN_DH), g, 0.0)
        csum_f = jnp.dot(lower, gpart, preferred_element_type=F32, precision=HIGHEST)
        csum_r = jnp.dot(upper, gpart, preferred_element_type=F32, precision=HIGHEST)
        gc = jnp.where(lane < DN_DH + DN_HEADS, csum_f, csum_r)
        t = jnp.sum(gpart, axis=0, keepdims=True)
        gl_ref[c] = jnp.broadcast_to(jnp.exp(t), (8, 128))
        g_all.append(g)
        gc_all.append(gc)
        gc_t.append(jnp.transpose(gc))
        tot.append(t)
    qs = {(c, h): q_ref[h, rows(c), :] for c in range(DN_CPS) for h in range(DN_HEADS)}
    ks = {(c, h): k_ref[h, rows(c), :] for c in range(DN_CPS) for h in range(DN_HEADS)}
    kbs, egcs, decays, kks = [], [], [], []
    for c, d, h in inst:
        j = d * DN_HEADS + h
        incl = (row >= col) if d == 0 else (row <= col)
        gc_col = gc_all[c][:, DN_DH + j:DN_DH + j + 1]
        gc_row = gc_t[c][DN_DH + j:DN_DH + j + 1, :]
        decays.append(jnp.where(incl, jnp.exp(jnp.where(incl, gc_col - gc_row, 0.0)), 0.0))
        egcs.append(jnp.exp(gc_col))
        kbs.append(ks[c, h] * g_all[c][:, j:j + 1])
    for (c, d, h), kb in zip(inst, kbs):
        kks.append(lax.dot_general(jnp.concatenate([kb, qs[c, h]], axis=0).astype(BF16), ks[c, h].astype(BF16),
                                   (((1,), (1,)), ((), ())), preferred_element_type=F32))
    nmats = []
    for (c, d, h), kk, decay in zip(inst, kks, decays):
        strict = (row > col) if d == 0 else (row < col)
        nmats.append(jnp.where(strict, -(kk[:C] * decay), 0.0))
    tinv = _unit_tri_inverse_many(nmats, row, col)
    sols = []
    for (c, d, h), t, kb, egc in zip(inst, tinv, kbs, egcs):
        j = d * DN_HEADS + h
        rhs = jnp.concatenate([v_ref[h, rows(c), :] * g_all[c][:, j:j + 1], kb * egc], axis=1)
        sols.append(_bdot(t, rhs))
    for (c, d, h), sol, kk, decay, egc in zip(inst, sols, kks, decays, egcs):
        j = d * DN_HEADS + h
        incl = (row >= col) if d == 0 else (row <= col)
        gc_col = gc_all[c][:, DN_DH + j:DN_DH + j + 1]
        tot_j = tot[c][:, DN_DH + j:DN_DH + j + 1]
        u_ref[j, rows(c), :] = sol[:, :DN_HEAD_DIM]
        wq_ref[j, c * 2 * C:(c + 1) * 2 * C, :] = jnp.concatenate([sol[:, DN_HEAD_DIM:], qs[c, h] * egc],
                                                                  axis=0).astype(BF16)
        kgt_ref[j, c * DN_HEAD_DIM:(c + 1) * DN_HEAD_DIM, :] = jnp.transpose(
            ks[c, h] * jnp.exp(tot_j - gc_col)).astype(BF16)
        aqk_ref[j, rows(c), :] = jnp.where(incl, kk[C:] * decay, 0.0).astype(BF16)


def dn_chunk(qn, kn, vn, gates):
    n = qn.shape[1]
    nc = n // DN_C
    rows = DN_CPS * DN_C
    hm = pl.BlockSpec((DN_HEADS, rows, DN_HEAD_DIM), lambda c: (0, c, 0))
    sds = jax.ShapeDtypeStruct
    return pl.pallas_call(
        _dn_chunk_kernel,
        grid=(nc // DN_CPS,),
        in_specs=[hm, hm, hm, pl.BlockSpec((rows, 128), lambda c: (c, 0))],
        out_specs=[
            pl.BlockSpec((DN_DH, rows, DN_HEAD_DIM), lambda c: (0, c, 0)),
            pl.BlockSpec((DN_DH, 2 * rows, DN_HEAD_DIM), lambda c: (0, c, 0)),
            pl.BlockSpec((DN_DH, DN_CPS * DN_HEAD_DIM, DN_C), lambda c: (0, c, 0)),
            pl.BlockSpec((DN_DH, rows, DN_C), lambda c: (0, c, 0)),
            pl.BlockSpec((DN_CPS, 8, 128), lambda c: (c, 0, 0)),
        ],
        out_shape=[
            sds((DN_DH, n, DN_HEAD_DIM), F32),
            sds((DN_DH, 2 * n, DN_HEAD_DIM), BF16),
            sds((DN_DH, nc * DN_HEAD_DIM, DN_C), BF16),
            sds((DN_DH, n, DN_C), BF16),
            sds((nc, 8, 128), F32),
        ],
        compiler_params=_cparams(("parallel",)),
        name="dn_chunk",
    )(qn, kn, vn, gates)


def _dn_scan_kernel(uf_ref, wqf_ref, kgf_ref, aqf_ref, glf_ref, ur_ref, wqr_ref, kgr_ref, aqr_ref, glr_ref,
                    of_ref, or_ref, state):
    @pl.when(pl.program_id(0) == 0)
    def _():
        state[...] = jnp.zeros_like(state)

    C = DN_C
    sets = ((uf_ref, wqf_ref, kgf_ref, aqf_ref, glf_ref, of_ref), (ur_ref, wqr_ref, kgr_ref, aqr_ref, glr_ref, or_ref))
    chains = [(d, h) + sets[d] for d in range(2) for h in range(DN_HEADS)]
    ss = [state[d * DN_HEADS + h] for d, h, *_ in chains]
    ts = [jnp.dot(wq_ref[h], s.astype(BF16), preferred_element_type=F32)
          for (d, h, u_ref, wq_ref, *_), s in zip(chains, ss)]
    vbs = [(u_ref[h] - t[:C]).astype(BF16) for (d, h, u_ref, *_), t in zip(chains, ts)]
    os_ = [t[C:] + jnp.dot(aq_ref[h], vb, preferred_element_type=F32)
           for (d, h, u_ref, wq_ref, kg_ref, aq_ref, *_), t, vb in zip(chains, ts, vbs)]
    ns = [s * gl_ref[0][0:1, DN_DH + d * DN_HEADS + h:DN_DH + d * DN_HEADS + h + 1]
          + jnp.dot(kg_ref[h], vb, preferred_element_type=F32)
          for (d, h, u_ref, wq_ref, kg_ref, aq_ref, gl_ref, o_ref), s, vb in zip(chains, ss, vbs)]
    for (d, h, *_, o_ref), o, s_new in zip(chains, os_, ns):
        o_ref[h] = o
        state[d * DN_HEADS + h] = s_new


def dn_scan(u, wq, kgt, aqk, gl):
    n = u.shape[1]
    nc = n // DN_C
    cc = CTX_LEN // DN_C
    fwd = lambda s: s
    rev = lambda s: jnp.where(s < cc, cc - 1 - s, nc + cc - 1 - s)
    specs = []
    for d, cm in enumerate((fwd, rev)):
        specs += [
            pl.BlockSpec((DN_HEADS, DN_C, DN_HEAD_DIM), lambda s, d=d, cm=cm: (d, cm(s), 0)),
            pl.BlockSpec((DN_HEADS, 2 * DN_C, DN_HEAD_DIM), lambda s, d=d, cm=cm: (d, cm(s), 0)),
            pl.BlockSpec((DN_HEADS, DN_HEAD_DIM, DN_C), lambda s, d=d, cm=cm: (d, cm(s), 0)),
            pl.BlockSpec((DN_HEADS, DN_C, DN_C), lambda s, d=d, cm=cm: (d, cm(s), 0)),
            pl.BlockSpec((1, 8, 128), lambda s, cm=cm: (cm(s), 0, 0)),
        ]
    osd = jax.ShapeDtypeStruct((DN_HEADS, n, DN_HEAD_DIM), F32)
    return pl.pallas_call(
        _dn_scan_kernel,
        grid=(nc,),
        in_specs=specs,
        out_specs=[pl.BlockSpec((DN_HEADS, DN_C, DN_HEAD_DIM), lambda s: (0, fwd(s), 0)),
                   pl.BlockSpec((DN_HEADS, DN_C, DN_HEAD_DIM), lambda s: (0, rev(s), 0))],
        out_shape=[osd, osd],
        scratch_shapes=[pltpu.VMEM((DN_DH, DN_HEAD_DIM, DN_HEAD_DIM), F32)],
        compiler_params=_cparams(("arbitrary",)),
        name="dn_scan",
    )(u, wq, kgt, aqk, gl, u, wq, kgt, aqk, gl)


def dn_mixer(z, conv_w, a_log, dt_bias):
    qn, kn, vn, gates = dn_prep(z, conv_w, a_log, dt_bias)
    u, wq, kgt, aqk, gl = dn_chunk(qn, kn, vn, gates)
    return dn_scan(u, wq, kgt, aqk, gl)


S5_SUB = 8
S5_BLK = S5_GROUPS * S5_STATE // S5_SUB
S5_UB = S5_W // S5_SUB
S5_T = 128


def _s5_param_kernel(lr_ref, li_ref, ls_ref, br_ref, bi_ref, ar_ref, ai_ref, bbr_ref, bbi_ref):
    lr = jnp.minimum(lr_ref[...], -1e-4)
    li = li_ref[...]
    dt = jnp.exp(ls_ref[...])
    mag = jnp.exp(lr * dt)
    ar = mag * jnp.cos(li * dt)
    ai = mag * jnp.sin(li * dt)
    den = lr * lr + li * li
    nr, ni = ar - 1.0, ai
    fr = (nr * lr + ni * li) / den
    fi = (ni * lr - nr * li) / den
    ar_ref[...] = ar
    ai_ref[...] = ai
    for h in range(S5_GROUP_CH):
        bbr_ref[h] = fr * br_ref[h] - fi * bi_ref[h]
        bbi_ref[h] = fr * bi_ref[h] + fi * br_ref[h]


def s5_params(lam_re, lam_im, log_step, b_re, b_im):
    r = 2 * S5_GROUPS
    ls = jnp.broadcast_to(log_step.reshape(r, 1), (r, S5_STATE))
    bt = lambda t: jnp.transpose(t.reshape(r, S5_STATE, S5_GROUP_CH), (2, 0, 1))
    sds = jax.ShapeDtypeStruct
    return pl.pallas_call(
        _s5_param_kernel,
        out_shape=[sds((r, S5_STATE), F32), sds((r, S5_STATE), F32),
                   sds((S5_GROUP_CH, r, S5_STATE), F32), sds((S5_GROUP_CH, r, S5_STATE), F32)],
        name="s5_params",
    )(lam_re.reshape(r, S5_STATE), lam_im.reshape(r, S5_STATE), ls, bt(b_re), bt(b_im))


def _s5_scan_kernel(uf_ref, ur_ref, a_ref, rb_ref, cc_ref, yf_ref, yr_ref, bu, hh, st):
    rows = S5_T * S5_SUB

    @pl.when(pl.program_id(0) == 0)
    def _():
        st[...] = jnp.zeros_like(st)

    sub = lax.broadcasted_iota(jnp.int32, (rows, S5_W), 0) % S5_SUB
    blk = lax.broadcasted_iota(jnp.int32, (rows, S5_W), 1) // S5_UB
    own = sub == blk
    for d, u_ref in enumerate((uf_ref, ur_ref)):
        ue = jnp.broadcast_to(u_ref[...][:, None, :], (S5_T, S5_SUB, S5_W)).reshape(rows, S5_W)
        lhs = jnp.where(own, ue, 0.0).astype(BF16)
        bu[d, 0] = jnp.dot(lhs, rb_ref[d, 0], preferred_element_type=F32)
        bu[d, 1] = jnp.dot(lhs, rb_ref[d, 1], preferred_element_type=F32)

    a = [[a_ref[d, c] for c in range(2)] for d in range(2)]

    def body(t, carry):
        fr, fi, rr, ri = carry
        rf = pl.multiple_of(t * S5_SUB, S5_SUB)
        rv = pl.multiple_of((S5_T - 1 - t) * S5_SUB, S5_SUB)
        nfr = a[0][0] * fr - a[0][1] * fi + bu[0, 0, pl.ds(rf, S5_SUB), :]
        nfi = a[0][0] * fi + a[0][1] * fr + bu[0, 1, pl.ds(rf, S5_SUB), :]
        nrr = a[1][0] * rr - a[1][1] * ri + bu[1, 0, pl.ds(rv, S5_SUB), :]
        nri = a[1][0] * ri + a[1][1] * rr + bu[1, 1, pl.ds(rv, S5_SUB), :]
        hh[0, 0, pl.ds(rf, S5_SUB), :] = nfr
        hh[0, 1, pl.ds(rf, S5_SUB), :] = nfi
        hh[1, 0, pl.ds(rv, S5_SUB), :] = nrr
        hh[1, 1, pl.ds(rv, S5_SUB), :] = nri
        return nfr, nfi, nrr, nri

    fin = lax.fori_loop(0, S5_T, body, (st[0, 0], st[0, 1], st[1, 0], st[1, 1]), unroll=8)
    st[0, 0], st[0, 1], st[1, 0], st[1, 1] = fin

    for d, y_ref in enumerate((yf_ref, yr_ref)):
        ye = (jnp.dot(hh[d, 0].astype(BF16), cc_ref[d, 0], preferred_element_type=F32)
              + jnp.dot(hh[d, 1].astype(BF16), cc_ref[d, 1], preferred_element_type=F32))
        ye = jnp.where(own, ye, 0.0)
        y_ref[...] = jnp.sum(ye.reshape(S5_T, S5_SUB, S5_W), axis=1)


def s5_scan(z, a8, rb, cc):
    rows = S5_T * S5_SUB
    n = z.shape[0]
    nt = n // S5_T
    ct = CTX_LEN // S5_T
    ucol = Z_S5 // S5_W

    def rev_tile(i):
        return jnp.where(i < ct, ct - 1 - i, nt + ct - 1 - i)

    full = lambda shape: pl.BlockSpec(shape, lambda i: (0,) * len(shape))
    sds = jax.ShapeDtypeStruct((n, S5_W), F32)
    return pl.pallas_call(
        _s5_scan_kernel,
        grid=(nt,),
        in_specs=[
            pl.BlockSpec((S5_T, S5_W), lambda i: (i, ucol)),
            pl.BlockSpec((S5_T, S5_W), lambda i: (rev_tile(i), ucol)),
            full((2, 2, S5_SUB, S5_BLK)),
            full((2, 2, S5_W, S5_BLK)),
            full((2, 2, S5_BLK, S5_W)),
        ],
        out_specs=[pl.BlockSpec((S5_T, S5_W), lambda i: (i, 0)),
                   pl.BlockSpec((S5_T, S5_W), lambda i: (rev_tile(i), 0))],
        out_shape=[sds, sds],
        scratch_shapes=[pltpu.VMEM((2, 2, rows, S5_BLK), F32), pltpu.VMEM((2, 2, rows, S5_BLK), F32),
                        pltpu.VMEM((2, 2, S5_SUB, S5_BLK), F32)],
        compiler_params=_cparams(("arbitrary",)),
        name="s5_scan",
    )(z, z, a8, rb, cc)


def s5_mixer(z, lam_re, lam_im, log_step, b_re, b_im, c_re, c_im):
    ar, ai, bbr, bbi = s5_params(lam_re, lam_im, log_step, b_re, b_im)
    g_blk = jax.nn.one_hot(jnp.arange(S5_GROUPS) % (S5_GROUPS // S5_SUB), S5_GROUPS // S5_SUB, dtype=F32)

    def place_b(bb):
        bb = jnp.transpose(bb.reshape(S5_GROUP_CH, 2, S5_GROUPS, S5_STATE), (1, 2, 0, 3))
        return jnp.einsum('dghp,gj->dghjp', bb, g_blk).reshape(2, S5_W, S5_BLK)

    def place_c(cm):
        return jnp.einsum('dghp,gj->djpgh', cm, g_blk).reshape(2, S5_BLK, S5_W)

    rb = jnp.stack([place_b(bbr), place_b(bbi)], axis=1).astype(BF16)
    cc = jnp.stack([place_c(c_re), -place_c(c_im)], axis=1).astype(BF16)
    a8 = jnp.stack([ar.reshape(2, S5_SUB, S5_BLK), ai.reshape(2, S5_SUB, S5_BLK)], axis=1)
    return s5_scan(z, a8, rb, cc)


def _permute_w_in(w):
    n_gate = 4 * DN_HEADS
    attn = w[:, :QKV_W]
    dn = w[:, QKV_W:QKV_W + 4 * DN_W]
    gates = w[:, QKV_W + 4 * DN_W:QKV_W + 4 * DN_W + n_gate]
    s5 = w[:, QKV_W + 4 * DN_W + n_gate:]
    pad = jnp.zeros((w.shape[0], Z_S5 - Z_GATES - n_gate), w.dtype)
    return jnp.concatenate([dn, attn, gates, pad, s5], axis=1).astype(BF16)


def kernel(x, c, ctx, c_ctx, w_ada, b_ada, norm1_w, norm2_w, w_in, w_out, attn_q_norm, attn_k_norm, attn_sink,
           dn_conv, dn_a_log, dn_dt_bias, dn_o_norm, s5_lam_re, s5_lam_im, s5_log_step, s5_b_re, s5_b_im,
           s5_c_re, s5_c_im, s5_d, s5_w_glu, moe_w_grp, moe_b_grp, moe_w_rt, moe_b_rt, moe_w1, moe_w3, moe_w2):
    b, n, d = x.shape
    assert b == 1 and ctx.shape[1] == CTX_LEN and d == D_MODEL
    lc = CTX_LEN
    xa = jnp.concatenate([ctx[0], x[0]], axis=0)
    cvec = jnp.zeros((8, D_MODEL), F32).at[0].set(c_ctx).at[1].set(c[0])
    mods = adaln(cvec, w_ada, b_ada)
    cos_t, sin_t = rope_tables(n)
    w1_all = moe_w1.reshape(DEPTH * N_EXPERTS, D_MODEL, EXPERT_HIDDEN)
    w3_all = moe_w3.reshape(DEPTH * N_EXPERTS, D_MODEL, EXPERT_HIDDEN)
    w2_all = moe_w2.reshape(DEPTH * N_EXPERTS, EXPERT_HIDDEN, D_MODEL)
    for layer in range(DEPTH):
        mod = mods[layer, 0:2].reshape(2, 6, D_MODEL)
        mod = jnp.concatenate([mod, jnp.zeros((2, 2, D_MODEL), F32)], axis=1)
        z = in_proj(xa, norm1_w[layer], mod, _permute_w_in(w_in[layer]))
        mix_a = attn_mixer(z, attn_q_norm[layer], attn_k_norm[layer], attn_sink[layer], cos_t, sin_t)
        o_f, o_r = dn_mixer(z, dn_conv[layer], dn_a_log[layer], dn_dt_bias[layer])
        yf, yr = s5_mixer(z, s5_lam_re[layer], s5_lam_im[layer], s5_log_step[layer], s5_b_re[layer],
                          s5_b_im[layer], s5_c_re[layer], s5_c_im[layer])
        w_router = jnp.concatenate([moe_w_grp[layer], moe_w_rt[layer],
                                    jnp.zeros((D_MODEL, 128 - N_GROUPS - N_EXPERTS), F32)], axis=1)
        b_router = jnp.concatenate([moe_b_grp[layer], moe_b_rt[layer],
                                    jnp.zeros((128 - N_GROUPS - N_EXPERTS,), F32)]).reshape(1, 128)
        xa, h2, route = out_proj(mix_a, o_f, o_r, z, yf, yr, xa, mod, w_out[layer].astype(BF16), norm2_w[layer],
                                 w_router, b_router, dn_o_norm[layer], s5_d[layer], s5_w_glu[layer].astype(BF16))
        xa = moe_block(xa, h2, route, mod, w1_all, w3_all, w2_all, layer, skip_ctx=(layer == DEPTH - 1))
    return xa[None]
```

```python
import functools

import jax
import jax.numpy as jnp
import numpy as np
from jax import lax
from jax.experimental import pallas as pl
from jax.experimental.pallas import tpu as pltpu

F32 = jnp.float32
BF16 = jnp.bfloat16
HIGHEST = lax.Precision.HIGHEST

D_MODEL = 2048
SEQ = 8192
DEPTH = 2
GRID_W = 64
CTX_LEN = 256
N_ALL = CTX_LEN + SEQ
EPS = 1e-6
NEG_INF = -1e30

ATTN_HEADS = 12
ATTN_KV_HEADS = 4
ATTN_HEAD_DIM = 64
ATTN_GROUP = ATTN_HEADS // ATTN_KV_HEADS
ATTN_W = ATTN_HEADS * ATTN_HEAD_DIM
ATTN_KV_W = ATTN_KV_HEADS * ATTN_HEAD_DIM
WINDOW = 128
ATTN_BLOCK = 128
ROPE_BASE = 10000.0
DN_HEADS = 6
DN_HEAD_DIM = 128
DN_W = DN_HEADS * DN_HEAD_DIM
DN_CONV = 5
DN_CHUNK = 64
S5_W = D_MODEL - ATTN_W - DN_W
S5_GROUP_CH = 16
S5_GROUPS = S5_W // S5_GROUP_CH
S5_STATE = 64
N_GROUPS = 4
EXPERTS_PER_GROUP = 8
N_EXPERTS = N_GROUPS * EXPERTS_PER_GROUP
TOP_K = 2
EXPERT_HIDDEN = 512

Z_DNQ, Z_DNK, Z_DNV, Z_DNG = 0, 768, 1536, 2304
Z_Q, Z_K, Z_V = 3072, 3840, 4096
Z_GATES = 4352
Z_BETA, Z_ALPHA = Z_GATES, Z_GATES + 2 * DN_HEADS
Z_S5 = 4608
Z_W = 5120

VMEM_LIMIT = 56 * 1024 * 1024


def _cparams(sem, vmem=VMEM_LIMIT):
    return pltpu.CompilerParams(dimension_semantics=sem, vmem_limit_bytes=vmem)


ADALN_TN = 1024


def _adaln_kernel(c_ref, w_ref, b_ref, o_ref):
    reps = ADALN_TN // 128

    def body(kb, acc):
        r0 = pl.multiple_of(kb * 8, 8)
        w = w_ref[0, pl.ds(r0, 8), :]
        out = []
        for r in range(2):
            cv = c_ref[r, pl.ds(r0, 8), :]
            s = cv * jax.nn.sigmoid(cv)
            out.append(acc[r] + w * jnp.concatenate([s] * reps, axis=1))
        return tuple(out)

    zero = jnp.zeros((8, ADALN_TN), F32)
    acc = lax.fori_loop(0, D_MODEL // 8, body, (zero, zero), unroll=4)
    rows = [jnp.sum(a, axis=0, keepdims=True) + b_ref[0] for a in acc]
    o_ref[0] = jnp.concatenate(rows + [jnp.zeros((6, ADALN_TN), F32)], axis=0)


def adaln(c_ctx, c, w_ada, b_ada):
    L = w_ada.shape[0]
    n6 = 6 * D_MODEL
    cb = jnp.broadcast_to(jnp.stack([c_ctx, c])[:, :, None], (2, D_MODEL, 128))
    return pl.pallas_call(
        _adaln_kernel,
        grid=(L, n6 // ADALN_TN),
        in_specs=[
            pl.BlockSpec((2, D_MODEL, 128), lambda l, j: (0, 0, 0)),
            pl.BlockSpec((1, D_MODEL, ADALN_TN), lambda l, j: (l, 0, j)),
            pl.BlockSpec((1, 1, ADALN_TN), lambda l, j: (l, 0, j)),
        ],
        out_specs=pl.BlockSpec((1, 8, ADALN_TN), lambda l, j: (l, 0, j)),
        out_shape=jax.ShapeDtypeStruct((L, 8, n6), F32),
        compiler_params=_cparams(("parallel", "parallel")),
        name="adaln",
    )(cb, w_ada, b_ada.reshape(L, 1, n6))


NORM_ROWS = 64


def _is_ctx_rows(base, rows):
    return base + lax.broadcasted_iota(jnp.int32, (rows, 1), 0) < CTX_LEN


def _row_is_ctx(tm):
    return _is_ctx_rows(pl.program_id(0) * tm, tm)


def _norm_mod(x, nw, mod_ref, shift_i, scale_i, is_ctx):
    ms = jnp.mean(x * x, axis=-1, keepdims=True)
    h = x * lax.rsqrt(ms + EPS) * nw
    sc = jnp.where(is_ctx, mod_ref[0, scale_i:scale_i + 1, :], mod_ref[1, scale_i:scale_i + 1, :])
    sh = jnp.where(is_ctx, mod_ref[0, shift_i:shift_i + 1, :], mod_ref[1, shift_i:shift_i + 1, :])
    return h * (1.0 + sc) + sh


def _in_proj_kernel(x_ref, nw_ref, mod_ref, w_ref, o_ref, h_scr, *, tm):
    @pl.when(pl.program_id(1) == 0)
    def _():
        def chunk(c, carry):
            r0 = pl.multiple_of(c * NORM_ROWS, NORM_ROWS)
            h = _norm_mod(x_ref[pl.ds(r0, NORM_ROWS), :], nw_ref[...], mod_ref, 0, 1,
                          _is_ctx_rows(pl.program_id(0) * tm + r0, NORM_ROWS))
            h_scr[pl.ds(r0, NORM_ROWS), :] = h.astype(BF16)
            return carry
        lax.fori_loop(0, tm // NORM_ROWS, chunk, 0)

    o_ref[...] = jnp.dot(h_scr[...], w_ref[...], preferred_element_type=F32)


def in_proj(xa, norm_w, mod, w_in_p):
    tm, tn = 512, 1024
    n = xa.shape[0]
    return pl.pallas_call(
        functools.partial(_in_proj_kernel, tm=tm),
        grid=(pl.cdiv(n, tm), Z_W // tn),
        in_specs=[
            pl.BlockSpec((tm, D_MODEL), lambda i, j: (i, 0)),
            pl.BlockSpec((1, D_MODEL), lambda i, j: (0, 0)),
            pl.BlockSpec((2, 8, D_MODEL), lambda i, j: (0, 0, 0)),
            pl.BlockSpec((D_MODEL, tn), lambda i, j: (0, j)),
        ],
        out_specs=pl.BlockSpec((tm, tn), lambda i, j: (i, j)),
        out_shape=jax.ShapeDtypeStruct((n, Z_W), F32),
        scratch_shapes=[pltpu.VMEM((tm, D_MODEL), BF16)],
        compiler_params=_cparams(("parallel", "arbitrary")),
        name="in_proj",
    )(xa, norm_w.reshape(1, D_MODEL), mod, w_in_p)


def _out_proj_kernel(a_ref, of_ref, or_ref, zg_ref, onw_ref, zu_ref, yf_ref, yr_ref, dsk_ref, wg_ref,
                     x_ref, mod_ref, w_ref, nw_ref, wr_ref, br_ref, xo_ref, h_ref, rt_ref, *, tm):
    is_ctx = _row_is_ctx(tm)
    b_parts = []
    for hd in range(DN_HEADS):
        o = of_ref[hd] + or_ref[hd]
        o = o * lax.rsqrt(jnp.mean(o * o, axis=-1, keepdims=True) + EPS) * onw_ref[...]
        g = zg_ref[:, hd * DN_HEAD_DIM:(hd + 1) * DN_HEAD_DIM]
        b_parts.append((o * (g * jax.nn.sigmoid(g))).astype(BF16))
    b = jnp.concatenate(b_parts, axis=1)
    ys = jax.nn.gelu(zu_ref[...] * dsk_ref[...] + yf_ref[...] + yr_ref[...])
    s = (ys * jax.nn.sigmoid(jnp.dot(ys.astype(BF16), wg_ref[...], preferred_element_type=F32))).astype(BF16)
    y = (jnp.dot(a_ref[...], w_ref[0:ATTN_W, :], preferred_element_type=F32)
         + jnp.dot(b, w_ref[ATTN_W:ATTN_W + DN_W, :], preferred_element_type=F32)
         + jnp.dot(s, w_ref[ATTN_W + DN_W:, :], preferred_element_type=F32))
    gate = jnp.where(is_ctx, mod_ref[0, 2:3, :], mod_ref[1, 2:3, :])
    xn = x_ref[...] + gate * y
    xo_ref[...] = xn
    h = _norm_mod(xn, nw_ref[...], mod_ref, 3, 4, is_ctx)
    h_ref[...] = h
    wr = wr_ref[...]
    h_hi = h.astype(BF16)
    h_lo = (h - h_hi.astype(F32)).astype(BF16)
    w_hi = wr.astype(BF16)
    w_lo = (wr - w_hi.astype(F32)).astype(BF16)
    lg = (jnp.dot(h_hi, w_hi, preferred_element_type=F32) + jnp.dot(h_hi, w_lo, preferred_element_type=F32)
          + jnp.dot(h_lo, w_hi, preferred_element_type=F32) + br_ref[...])
    rt_ref[...] = _route(lg)


def out_proj(mix_a, o_f, o_r, z, yf, yr, xa, mod, w_out_b, norm2_w, w_router, b_router, o_norm_w, d_skip, w_glu_b):
    tm = 256
    n = xa.shape[0]
    row = lambda i: (i, 0)
    const2 = lambda i: (0, 0)
    hm = pl.BlockSpec((DN_HEADS, tm, DN_HEAD_DIM), lambda i: (0, i, 0))
    return pl.pallas_call(
        functools.partial(_out_proj_kernel, tm=tm),
        grid=(n // tm,),
        in_specs=[
            pl.BlockSpec((tm, ATTN_W), row),
            hm, hm,
            pl.BlockSpec((tm, DN_W), lambda i: (i, Z_DNG // DN_W)),
            pl.BlockSpec((1, DN_HEAD_DIM), const2),
            pl.BlockSpec((tm, S5_W), lambda i: (i, Z_S5 // S5_W)),
            pl.BlockSpec((tm, S5_W), row),
            pl.BlockSpec((tm, S5_W), row),
            pl.BlockSpec((1, S5_W), const2),
            pl.BlockSpec((S5_W, S5_W), const2),
            pl.BlockSpec((tm, D_MODEL), row),
            pl.BlockSpec((2, 8, D_MODEL), lambda i: (0, 0, 0)),
            pl.BlockSpec((D_MODEL, D_MODEL), const2),
            pl.BlockSpec((1, D_MODEL), const2),
            pl.BlockSpec((D_MODEL, 128), const2),
            pl.BlockSpec((1, 128), const2),
        ],
        out_specs=[
            pl.BlockSpec((tm, D_MODEL), row),
            pl.BlockSpec((tm, D_MODEL), row),
            pl.BlockSpec((tm, 128), row),
        ],
        out_shape=[
            jax.ShapeDtypeStruct((n, D_MODEL), F32),
            jax.ShapeDtypeStruct((n, D_MODEL), F32),
            jax.ShapeDtypeStruct((n, 128), F32),
        ],
        compiler_params=_cparams(("parallel",)),
        name="out_proj",
    )(mix_a, o_f, o_r, z, o_norm_w.reshape(1, DN_HEAD_DIM), z, yf, yr, d_skip.reshape(1, S5_W), w_glu_b,
      xa, mod, w_out_b, norm2_w.reshape(1, D_MODEL), w_router, b_router)


def _route(lg):
    tm = lg.shape[0]
    lane = lax.broadcasted_iota(jnp.int32, (tm, 128), 1)
    is_g = lane < N_GROUPS
    gl = jnp.where(is_g, lg, NEG_INF)
    gmax = jnp.max(gl, axis=-1, keepdims=True)
    gidx = jnp.min(jnp.where((gl == gmax) & is_g, lane, 128), axis=-1, keepdims=True)
    gsum = jnp.sum(jnp.where(is_g, jnp.exp(gl - gmax), 0.0), axis=-1, keepdims=True)
    g_w = 1.0 / gsum
    e_lane = lane - N_GROUPS
    in_grp = (e_lane >= gidx * EXPERTS_PER_GROUP) & (e_lane < (gidx + 1) * EXPERTS_PER_GROUP)
    el = jnp.where(in_grp, lg, NEG_INF)
    v1 = jnp.max(el, axis=-1, keepdims=True)
    i1 = jnp.min(jnp.where((el == v1) & in_grp, e_lane, 128), axis=-1, keepdims=True)
    el2 = jnp.where(e_lane == i1, NEG_INF, el)
    in2 = in_grp & (e_lane != i1)
    v2 = jnp.max(el2, axis=-1, keepdims=True)
    i2 = jnp.min(jnp.where((el2 == v2) & in2, e_lane, 128), axis=-1, keepdims=True)
    e2 = jnp.exp(v2 - v1)
    w1 = g_w / (1.0 + e2)
    w2 = g_w * e2 / (1.0 + e2)
    out = jnp.where(lane == 0, i1.astype(F32), 0.0)
    out = jnp.where(lane == 1, i2.astype(F32), out)
    out = jnp.where(lane == 2, w1, out)
    out = jnp.where(lane == 3, w2, out)
    return out


MOE_TILE = 256


def _moe_tiles(n):
    return TOP_K * n // MOE_TILE + N_EXPERTS


def _expert_kernel(te_ref, ts_ref, tok_ref, nt_ref, h_hbm, w1_ref, w3_ref, w2_ref, y_ref,
                   xbuf0, xbuf1, sem, w1b, w3b, w2b):
    i = pl.program_id(0)
    n_tiles = nt_ref[0]
    last = tok_ref.shape[0] - 1
    bufs = (xbuf0, xbuf1)

    def gather_start(tile, slot):
        base = ts_ref[tile]
        for r in range(MOE_TILE):
            tok = tok_ref[jnp.minimum(base + r, last)]
            pltpu.make_async_copy(h_hbm.at[pl.ds(tok, 1)], bufs[slot].at[pl.ds(r, 1)], sem.at[slot]).start()

    def gather_wait(slot):
        pltpu.make_async_copy(bufs[slot], bufs[slot], sem.at[slot]).wait()

    @pl.when(i == 0)
    def _():
        gather_start(0, 0)

    prev_e = te_ref[jnp.maximum(i - 1, 0)]
    new_e = (i == 0) | (te_ref[i] != prev_e)

    @pl.when((i < n_tiles) & new_e)
    def _():
        w1b[...] = w1_ref[0].astype(BF16)
        w3b[...] = w3_ref[0].astype(BF16)
        w2b[...] = w2_ref[0].astype(BF16)

    for slot in range(2):
        @pl.when((i < n_tiles) & (i % 2 == slot))
        def _():
            gather_wait(slot)
            gather_start(jnp.minimum(i + 1, n_tiles - 1), 1 - slot)
            xt = bufs[slot][...].astype(BF16)
            a = jnp.dot(xt, w1b[...], preferred_element_type=F32)
            u = jnp.dot(xt, w3b[...], preferred_element_type=F32)
            act = (a * jax.nn.sigmoid(a)) * u
            y_ref[...] = jnp.dot(act.astype(BF16), w2b[...], preferred_element_type=F32)

            @pl.when(i == n_tiles - 1)
            def _():
                gather_wait(1 - slot)

    @pl.when(i >= n_tiles)
    def _():
        y_ref[...] = jnp.zeros_like(y_ref)


def expert_mlp(tile_expert, tile_start, sorted_tok, n_tiles, h2, w1, w3, w2):
    e_map = lambda i, te, ts, tok, nt: (te[i], 0, 0)
    moe_tiles = tile_expert.shape[0]
    grid_spec = pltpu.PrefetchScalarGridSpec(
        num_scalar_prefetch=4,
        grid=(moe_tiles,),
        in_specs=[
            pl.BlockSpec(memory_space=pl.ANY),
            pl.BlockSpec((1, D_MODEL, EXPERT_HIDDEN), e_map),
            pl.BlockSpec((1, D_MODEL, EXPERT_HIDDEN), e_map),
            pl.BlockSpec((1, EXPERT_HIDDEN, D_MODEL), e_map),
        ],
        out_specs=pl.BlockSpec((MOE_TILE, D_MODEL), lambda i, te, ts, tok, nt: (i, 0)),
        scratch_shapes=[
            pltpu.VMEM((MOE_TILE, D_MODEL), F32),
            pltpu.VMEM((MOE_TILE, D_MODEL), F32),
            pltpu.SemaphoreType.DMA((2,)),
            pltpu.VMEM((D_MODEL, EXPERT_HIDDEN), BF16),
            pltpu.VMEM((D_MODEL, EXPERT_HIDDEN), BF16),
            pltpu.VMEM((EXPERT_HIDDEN, D_MODEL), BF16),
        ],
    )
    return pl.pallas_call(
        _expert_kernel,
        grid_spec=grid_spec,
        out_shape=jax.ShapeDtypeStruct((moe_tiles * MOE_TILE, D_MODEL), F32),
        compiler_params=_cparams(("arbitrary",)),
        name="expert_mlp",
    )(tile_expert, tile_start, sorted_tok, n_tiles, h2, w1, w3, w2)


CMB_TILE = 256


def _combine_kernel(pos_ref, y_hbm, x_ref, route_ref, mod_ref, o_ref, ybuf0, ybuf1, sem, *, first_tile):
    i = pl.program_id(0)
    nt = pl.num_programs(0)
    bufs = (ybuf0, ybuf1)

    def gather_start(tile, slot):
        base = (tile + first_tile) * (CMB_TILE * TOP_K)
        for r in range(CMB_TILE):
            for k in range(TOP_K):
                p = pos_ref[base + r * TOP_K + k]
                pltpu.make_async_copy(y_hbm.at[pl.ds(p, 1)], bufs[slot].at[k, pl.ds(r, 1)], sem.at[slot]).start()

    def gather_wait(slot):
        pltpu.make_async_copy(bufs[slot], bufs[slot], sem.at[slot]).wait()

    @pl.when(i == 0)
    def _():
        gather_start(0, 0)

    for slot in range(2):
        @pl.when(i % 2 == slot)
        def _():
            gather_wait(slot)
            gather_start(jnp.minimum(i + 1, nt - 1), 1 - slot)
            is_ctx = _is_ctx_rows((i + first_tile) * CMB_TILE, CMB_TILE)
            gate = jnp.where(is_ctx, mod_ref[0, 5:6, :], mod_ref[1, 5:6, :])
            w0 = route_ref[:, TOP_K:TOP_K + 1]
            w1 = route_ref[:, TOP_K + 1:TOP_K + 2]
            o_ref[...] = x_ref[...] + gate * (w0 * bufs[slot][0] + w1 * bufs[slot][1])

            @pl.when(i == nt - 1)
            def _():
                gather_wait(1 - slot)


def moe_combine(pos, y_sorted, xa, route, mod, skip_ctx):
    first_tile = CTX_LEN // CMB_TILE if skip_ctx else 0
    n = xa.shape[0] - first_tile * CMB_TILE
    grid_spec = pltpu.PrefetchScalarGridSpec(
        num_scalar_prefetch=1,
        grid=(n // CMB_TILE,),
        in_specs=[
            pl.BlockSpec(memory_space=pl.ANY),
            pl.BlockSpec((CMB_TILE, D_MODEL), lambda i, pos: (i + first_tile, 0)),
            pl.BlockSpec((CMB_TILE, 128), lambda i, pos: (i + first_tile, 0)),
            pl.BlockSpec((2, 8, D_MODEL), lambda i, pos: (0, 0, 0)),
        ],
        out_specs=pl.BlockSpec((CMB_TILE, D_MODEL), lambda i, pos: (i, 0)),
        scratch_shapes=[
            pltpu.VMEM((TOP_K, CMB_TILE, D_MODEL), F32),
            pltpu.VMEM((TOP_K, CMB_TILE, D_MODEL), F32),
            pltpu.SemaphoreType.DMA((2,)),
        ],
    )
    return pl.pallas_call(
        functools.partial(_combine_kernel, first_tile=first_tile),
        grid_spec=grid_spec,
        out_shape=jax.ShapeDtypeStruct((n, D_MODEL), F32),
        compiler_params=_cparams(("arbitrary",)),
        name="moe_combine",
    )(pos, y_sorted, xa, route, mod)


def moe_dispatch_plan(route):
    n = route.shape[0]
    eid = route[:, 0:TOP_K].astype(jnp.int32).reshape(-1)
    p_total = eid.shape[0]
    experts = jnp.arange(N_EXPERTS, dtype=jnp.int32)
    counts = jnp.sum((eid[:, None] == experts[None, :]).astype(jnp.int32), axis=0)
    tiles_per = (counts + MOE_TILE - 1) // MOE_TILE
    tile_off = jnp.cumsum(tiles_per) - tiles_per
    off = jnp.cumsum(counts) - counts
    pair_ids = jnp.arange(p_total, dtype=jnp.int32)
    e_sorted, sorted_pair = lax.sort((eid, pair_ids), num_keys=1, is_stable=True)
    sorted_tok = sorted_pair // TOP_K
    s_onehot = (e_sorted[:, None] == experts[None, :]).astype(jnp.int32)
    pos_sorted = pair_ids + jnp.sum(s_onehot * (tile_off * MOE_TILE - off)[None, :], axis=1)
    _, pos = lax.sort((sorted_pair, pos_sorted), num_keys=1)
    moe_tiles = _moe_tiles(n)
    n_tiles = jnp.sum(tiles_per).astype(jnp.int32)
    tile_ids = jnp.arange(moe_tiles, dtype=jnp.int32)
    tile_expert = jnp.sum((tile_ids[:, None] >= (tile_off + tiles_per)[None, :]).astype(jnp.int32), axis=1)
    tile_expert = jnp.minimum(tile_expert, N_EXPERTS - 1)
    t_onehot = (tile_expert[:, None] == experts[None, :]).astype(jnp.int32)
    tile_start = (jnp.sum(t_onehot * off[None, :], axis=1)
                  + (tile_ids - jnp.sum(t_onehot * tile_off[None, :], axis=1)) * MOE_TILE)
    return tile_expert, tile_start, sorted_tok, n_tiles.reshape(1), pos


def moe_block(xa, h2, route, mod, w1, w3, w2, layer, skip_ctx):
    tile_expert, tile_start, sorted_tok, n_tiles, pos = moe_dispatch_plan(route)
    y_sorted = expert_mlp(tile_expert + layer * N_EXPERTS, tile_start, sorted_tok, n_tiles, h2, w1, w3, w2)
    return moe_combine(pos, y_sorted, xa, route, mod, skip_ctx)


QK_W = ATTN_W + ATTN_KV_W
QKV_W = QK_W + ATTN_KV_W
ROPE_F = ATTN_HEAD_DIM // 4


def _norm_rope_heads(x, nw, cos, sin, out_ref, n_heads, scale):
    tm, width = x.shape
    xw = x * nw
    lane = lax.broadcasted_iota(jnp.int32, (tm, width), 1)
    odd = (lane // ROPE_F) % 2 == 1
    xs = jnp.where(odd, pltpu.roll(xw, ROPE_F, 1), pltpu.roll(xw, width - ROPE_F, 1))
    heads = [slice(h * ATTN_HEAD_DIM, (h + 1) * ATTN_HEAD_DIM) for h in range(n_heads)]
    invs = [lax.rsqrt(jnp.mean(x[:, sl] * x[:, sl], axis=-1, keepdims=True) + EPS) * scale for sl in heads]
    for h, (sl, inv) in enumerate(zip(heads, invs)):
        out_ref[h] = ((xw[:, sl] * cos + xs[:, sl] * sin) * inv).astype(out_ref.dtype)


def _attn_prep_kernel(zq_ref, zk_ref, zv_ref, qw_ref, kw_ref, cos_ref, sin_ref, q_ref, k_ref, v_ref):
    cos = cos_ref[...]
    sin = sin_ref[...]
    _norm_rope_heads(zq_ref[...], qw_ref[...], cos, sin, q_ref, ATTN_HEADS, ATTN_HEAD_DIM ** -0.5)
    _norm_rope_heads(zk_ref[...], kw_ref[...], cos, sin, k_ref, ATTN_KV_HEADS, 1.0)
    ones = jnp.ones((zv_ref.shape[0], ATTN_HEAD_DIM), BF16)
    for h in range(ATTN_KV_HEADS):
        v_ref[h] = jnp.concatenate([zv_ref[:, h * ATTN_HEAD_DIM:(h + 1) * ATTN_HEAD_DIM].astype(BF16), ones], axis=1)


def attn_prep(z, q_norm_w, k_norm_w, cos_t, sin_t):
    tm = 256
    n = z.shape[0]
    qw = jnp.tile(q_norm_w, ATTN_HEADS).reshape(1, ATTN_W)
    kw = jnp.tile(k_norm_w, ATTN_KV_HEADS).reshape(1, ATTN_KV_W)
    hm = lambda h, w=ATTN_HEAD_DIM: pl.BlockSpec((h, tm, w), lambda i: (0, i, 0))
    sds = lambda h, w=ATTN_HEAD_DIM: jax.ShapeDtypeStruct((h, n, w), BF16)
    return pl.pallas_call(
        _attn_prep_kernel,
        grid=(n // tm,),
        in_specs=[
            pl.BlockSpec((tm, ATTN_W), lambda i: (i, Z_Q // ATTN_W)),
            pl.BlockSpec((tm, ATTN_KV_W), lambda i: (i, Z_K // ATTN_KV_W)),
            pl.BlockSpec((tm, ATTN_KV_W), lambda i: (i, Z_V // ATTN_KV_W)),
            pl.BlockSpec((1, ATTN_W), lambda i: (0, 0)),
            pl.BlockSpec((1, ATTN_KV_W), lambda i: (0, 0)),
            pl.BlockSpec((tm, ATTN_HEAD_DIM), lambda i: (i, 0)),
            pl.BlockSpec((tm, ATTN_HEAD_DIM), lambda i: (i, 0)),
        ],
        out_specs=[hm(ATTN_HEADS), hm(ATTN_KV_HEADS), hm(ATTN_KV_HEADS, 2 * ATTN_HEAD_DIM)],
        out_shape=[sds(ATTN_HEADS), sds(ATTN_KV_HEADS), sds(ATTN_KV_HEADS, 2 * ATTN_HEAD_DIM)],
        compiler_params=_cparams(("parallel",)),
        name="attn_prep",
    )(z, z, z, qw, kw, cos_t, sin_t)


def _attn_kernel(sink_ref, q_ref, kp_ref, ko_ref, kn_ref, kc_ref, vp_ref, vo_ref, vn_ref, vc_ref, o_ref, *, n_blocks):
    i = pl.program_id(0)
    ctx_blocks = CTX_LEN // ATTN_BLOCK
    B = ATTN_BLOCK
    rows = ATTN_GROUP * B
    ncol = 3 * B + CTX_LEN
    r = lax.broadcasted_iota(jnp.int32, (rows, ncol), 0) % B
    c = lax.broadcasted_iota(jnp.int32, (rows, ncol), 1)
    lo = jnp.where(i > ctx_blocks, 0, B)
    hi = jnp.where(i < n_blocks - 1, 3 * B, 2 * B)
    hi = jnp.where(i < ctx_blocks, 0, hi)
    band = (c >= r) & (c <= r + 2 * WINDOW) & (c >= lo) & (c < hi)
    mask = band | (c >= 3 * B)
    grp = lax.broadcasted_iota(jnp.int32, (rows, 1), 0) // B
    for kh in range(ATTN_KV_HEADS):
        q3 = jnp.concatenate([q_ref[kh * ATTN_GROUP + g] for g in range(ATTN_GROUP)], axis=0)
        kcat = jnp.concatenate([kp_ref[kh], ko_ref[kh], kn_ref[kh], kc_ref[kh]], axis=0)
        vcat = jnp.concatenate([vp_ref[kh], vo_ref[kh], vn_ref[kh], vc_ref[kh]], axis=0)
        s = lax.dot_general(q3, kcat, (((1,), (1,)), ((), ())), preferred_element_type=F32)
        s = jnp.where(mask, s, NEG_INF)
        sink = jnp.zeros((rows, 1), F32)
        for g in range(ATTN_GROUP):
            sink = jnp.where(grp == g, sink_ref[kh * ATTN_GROUP + g], sink)
        m = jnp.maximum(jnp.max(s, axis=-1, keepdims=True), sink)
        p = jnp.exp(s - m)
        pv = jnp.dot(p.astype(BF16), vcat, preferred_element_type=F32)
        den = pv[:, ATTN_HEAD_DIM:ATTN_HEAD_DIM + 1] + jnp.exp(sink - m)
        o = pv[:, :ATTN_HEAD_DIM] / den
        for g in range(ATTN_GROUP):
            h = kh * ATTN_GROUP + g
            o_ref[:, h * ATTN_HEAD_DIM:(h + 1) * ATTN_HEAD_DIM] = o[g * B:(g + 1) * B].astype(o_ref.dtype)


def attention(qh, kh, vh, sink):
    n = qh.shape[1]
    B = ATTN_BLOCK
    nblk = n // B
    cb = CTX_LEN // B
    prev = lambda i: (0, jnp.clip(i - 1, cb, nblk - 1), 0)
    own = lambda i: (0, i, 0)
    nxt = lambda i: (0, jnp.clip(i + 1, cb, nblk - 1), 0)
    ctx = lambda i: (0, 0, 0)
    kv = lambda m, w=ATTN_HEAD_DIM: pl.BlockSpec((ATTN_KV_HEADS, B, w), m)
    kvc = lambda w=ATTN_HEAD_DIM: pl.BlockSpec((ATTN_KV_HEADS, CTX_LEN, w), ctx)
    vw = 2 * ATTN_HEAD_DIM
    return pl.pallas_call(
        functools.partial(_attn_kernel, n_blocks=nblk),
        grid=(nblk,),
        in_specs=[
            pl.BlockSpec(memory_space=pltpu.SMEM),
            pl.BlockSpec((ATTN_HEADS, B, ATTN_HEAD_DIM), own),
            kv(prev), kv(own), kv(nxt), kvc(),
            kv(prev, vw), kv(own, vw), kv(nxt, vw), kvc(vw),
        ],
        out_specs=pl.BlockSpec((B, ATTN_W), lambda i: (i, 0)),
        out_shape=jax.ShapeDtypeStruct((n, ATTN_W), BF16),
        compiler_params=_cparams(("parallel",)),
        name="attention",
    )(sink, qh, kh, kh, kh, kh, vh, vh, vh, vh)


def rope_tables(n_lat):
    rows = n_lat // GRID_W
    row = np.repeat(np.arange(rows), GRID_W).astype(np.float32)
    col = np.tile(np.arange(GRID_W), rows).astype(np.float32)
    inv = (ROPE_BASE ** (-np.arange(ROPE_F, dtype=np.float32) / ROPE_F)).astype(np.float32)
    ar, ac = row[:, None] * inv, col[:, None] * inv
    cos = np.concatenate([np.cos(ar), np.cos(ar), np.cos(ac), np.cos(ac)], axis=1)
    sin = np.concatenate([-np.sin(ar), np.sin(ar), -np.sin(ac), np.sin(ac)], axis=1)
    cos = np.concatenate([np.ones((CTX_LEN, ATTN_HEAD_DIM), np.float32), cos], axis=0)
    sin = np.concatenate([np.zeros((CTX_LEN, ATTN_HEAD_DIM), np.float32), sin], axis=0)
    return jnp.asarray(cos, F32), jnp.asarray(sin, F32)


def attn_mixer(z, q_norm_w, k_norm_w, sink, cos_t, sin_t):
    qh, kh, vh = attn_prep(z, q_norm_w, k_norm_w, cos_t, sin_t)
    return attention(qh, kh, vh, sink)


DN_QKV = 3 * DN_W
DN_HALO = 8
DN_DH = 2 * DN_HEADS
DN_C = DN_CHUNK
DN_CPS = 4


def _dn_prep_kernel(zm_ref, zp_ref, zn_ref, zg_ref, cw_ref, al_ref, dtb_ref, q_ref, k_ref, v_ref, g_ref, *, tm, n_tiles):
    i = pl.program_id(0)
    ctx_tiles = CTX_LEN // tm
    has_prev = (i != 0) & (i != ctx_tiles)
    has_next = (i != ctx_tiles - 1) & (i != n_tiles - 1)
    prev = jnp.where(has_prev, zp_ref[...], 0.0)
    nxt = jnp.where(has_next, zn_ref[...], 0.0)
    xcat = jnp.concatenate([prev, zm_ref[...], nxt], axis=0)
    half = DN_CONV // 2
    acc = None
    for t in range(DN_CONV):
        off = DN_HALO - half + t
        term = xcat[off:off + tm, :] * cw_ref[t:t + 1, :]
        acc = term if acc is None else acc + term
    y = acc * jax.nn.sigmoid(acc)
    for h in range(DN_HEADS):
        qh = y[:, h * DN_HEAD_DIM:(h + 1) * DN_HEAD_DIM]
        kh = y[:, DN_W + h * DN_HEAD_DIM:DN_W + (h + 1) * DN_HEAD_DIM]
        q_ref[h] = qh * (lax.rsqrt(jnp.sum(qh * qh, axis=-1, keepdims=True) + EPS) * (DN_HEAD_DIM ** -0.5))
        k_ref[h] = kh * lax.rsqrt(jnp.sum(kh * kh, axis=-1, keepdims=True) + EPS)
        v_ref[h] = y[:, 2 * DN_W + h * DN_HEAD_DIM:2 * DN_W + (h + 1) * DN_HEAD_DIM]
    zg = zg_ref[...]
    lane = lax.broadcasted_iota(jnp.int32, zg.shape, 1)
    beta = jax.nn.sigmoid(zg)
    alpha = pltpu.roll(zg, 128 - DN_DH, 1)
    gl = -jnp.exp(al_ref[...]) * jax.nn.softplus(alpha + dtb_ref[...])
    g_ref[...] = jnp.where(lane < DN_DH, beta, pltpu.roll(gl, DN_DH, 1))


def dn_prep(z, conv_w, a_log, dt_bias):
    tm = 256
    n = z.shape[0]
    n_tiles = n // tm
    hb = tm // DN_HALO
    cw = jnp.zeros((8, DN_QKV), F32).at[0:DN_CONV].set(conv_w)
    pad_row = lambda t: jnp.zeros((1, 128), F32).at[0, 0:DN_DH].set(t.reshape(-1))
    hm = pl.BlockSpec((DN_HEADS, tm, DN_HEAD_DIM), lambda i: (0, i, 0))
    sds = jax.ShapeDtypeStruct((DN_HEADS, n, DN_HEAD_DIM), F32)
    return pl.pallas_call(
        functools.partial(_dn_prep_kernel, tm=tm, n_tiles=n_tiles),
        grid=(n_tiles,),
        in_specs=[
            pl.BlockSpec((tm, DN_QKV), lambda i: (i, 0)),
            pl.BlockSpec((DN_HALO, DN_QKV), lambda i: (jnp.maximum(i * hb - 1, 0), 0)),
            pl.BlockSpec((DN_HALO, DN_QKV), lambda i: (jnp.minimum((i + 1) * hb, n // DN_HALO - 1), 0)),
            pl.BlockSpec((tm, 128), lambda i: (i, Z_GATES // 128)),
            pl.BlockSpec((8, DN_QKV), lambda i: (0, 0)),
            pl.BlockSpec((1, 128), lambda i: (0, 0)),
            pl.BlockSpec((1, 128), lambda i: (0, 0)),
        ],
        out_specs=[hm, hm, hm, pl.BlockSpec((tm, 128), lambda i: (i, 0))],
        out_shape=[sds, sds, sds, jax.ShapeDtypeStruct((n, 128), F32)],
        compiler_params=_cparams(("parallel",)),
        name="dn_prep",
    )(z, z, z, z, cw, pad_row(a_log), pad_row(dt_bias))


def _bdot(a, b):
    return jnp.dot(a.astype(BF16), b.astype(BF16), preferred_element_type=F32)


def _unit_tri_inverse_many(nmats, row, col):
    eye = (row == col).astype(F32)
    same = lambda b: (row // b) == (col // b)
    d1 = [jnp.where(same(8), m, 0.0) for m in nmats]
    d2 = [_bdot(a, a) for a in d1]
    d3 = [_bdot(a, b) for a, b in zip(d1, d2)]
    d4 = [_bdot(b, b) for b in d2]
    x = [eye + a + b + c for a, b, c in zip(d1, d2, d3)]
    t = [a + _bdot(a, b) for a, b in zip(x, d4)]
    for b in (8, 16, 32):
        sel = same(2 * b) & jnp.logical_not(same(b))
        tmp = [_bdot(jnp.where(sel, m, 0.0), a) for m, a in zip(nmats, t)]
        t = [a + _bdot(a, c) for a, c in zip(t, tmp)]
    return t


def _dn_chunk_kernel(q_ref, k_ref, v_ref, g_ref, u_ref, wq_ref, kgt_ref, aqk_ref, gl_ref):
    C = DN_C
    row = lax.broadcasted_iota(jnp.int32, (C, C), 0)
    col = lax.broadcasted_iota(jnp.int32, (C, C), 1)
    lane = lax.broadcasted_iota(jnp.int32, (C, 128), 1)
    lower = (row >= col).astype(F32)
    upper = (row <= col).astype(F32)
    inst = [(c, d, h) for c in range(DN_CPS) for d in range(2) for h in range(DN_HEADS)]
    rows = lambda c: slice(c * C, (c + 1) * C)
    g_all, gc_all, gc_t, tot = [], [], [], []
    for c in range(DN_CPS):
        g = g_ref[rows(c), :]
        gpart = jnp.where((lane >= DN_DH) & (lane < 2 * DN_DH), g, 0.0)
        csum_f = jnp.dot(lower, gpart, preferred_element_type=F32, precision=HIGHEST)
        csum_r = jnp.dot(upper, gpart, preferred_element_type=F32, precision=HIGHEST)
        gc = jnp.where(lane < DN_DH + DN_HEADS, csum_f, csum_r)
        t = jnp.sum(gpart, axis=0, keepdims=True)
        gl_ref[c] = jnp.broadcast_to(jnp.exp(t), (8, 128))
        g_all.append(g)
        gc_all.append(gc)
        gc_t.append(jnp.transpose(gc))
        tot.append(t)
    qs = {(c, h): q_ref[h, rows(c), :] for c in range(DN_CPS) for h in range(DN_HEADS)}
    ks = {(c, h): k_ref[h, rows(c), :] for c in range(DN_CPS) for h in range(DN_HEADS)}
    kbs, egcs, decays, kks = [], [], [], []
    for c, d, h in inst:
        j = d * DN_HEADS + h
        incl = (row >= col) if d == 0 else (row <= col)
        gc_col = gc_all[c][:, DN_DH + j:DN_DH + j + 1]
        gc_row = gc_t[c][DN_DH + j:DN_DH + j + 1, :]
        decays.append(jnp.where(incl, jnp.exp(jnp.where(incl, gc_col - gc_row, 0.0)), 0.0))
        egcs.append(jnp.exp(gc_col))
        kbs.append(ks[c, h] * g_all[c][:, j:j + 1])
    for (c, d, h), kb in zip(inst, kbs):
        kks.append(lax.dot_general(jnp.concatenate([kb, qs[c, h]], axis=0).astype(BF16), ks[c, h].astype(BF16),
                                   (((1,), (1,)), ((), ())), preferred_element_type=F32))
    nmats = []
    for (c, d, h), kk, decay in zip(inst, kks, decays):
        strict = (row > col) if d == 0 else (row < col)
        nmats.append(jnp.where(strict, -(kk[:C] * decay), 0.0))
    tinv = _unit_tri_inverse_many(nmats, row, col)
    sols = []
    for (c, d, h), t, kb, egc in zip(inst, tinv, kbs, egcs):
        j = d * DN_HEADS + h
        rhs = jnp.concatenate([v_ref[h, rows(c), :] * g_all[c][:, j:j + 1], kb * egc], axis=1)
        sols.append(_bdot(t, rhs))
    for (c, d, h), sol, kk, decay, egc in zip(inst, sols, kks, decays, egcs):
        j = d * DN_HEADS + h
        incl = (row >= col) if d == 0 else (row <= col)
        gc_col = gc_all[c][:, DN_DH + j:DN_DH + j + 1]
        tot_j = tot[c][:, DN_DH + j:DN_DH + j + 1]
        u_ref[j, rows(c), :] = sol[:, :DN_HEAD_DIM]
        wq_ref[j, c * 2 * C:(c + 1) * 2 * C, :] = jnp.concatenate([sol[:, DN_HEAD_DIM:], qs[c, h] * egc],
                                                                  axis=0).astype(BF16)
        kgt_ref[j, c * DN_HEAD_DIM:(c + 1) * DN_HEAD_DIM, :] = jnp.transpose(
            ks[c, h] * jnp.exp(tot_j - gc_col)).astype(BF16)
        aqk_ref[j, rows(c), :] = jnp.where(incl, kk[C:] * decay, 0.0).astype(BF16)


def dn_chunk(qn, kn, vn, gates):
    n = qn.shape[1]
    nc = n // DN_C
    rows = DN_CPS * DN_C
    hm = pl.BlockSpec((DN_HEADS, rows, DN_HEAD_DIM), lambda c: (0, c, 0))
    sds = jax.ShapeDtypeStruct
    return pl.pallas_call(
        _dn_chunk_kernel,
        grid=(nc // DN_CPS,),
        in_specs=[hm, hm, hm, pl.BlockSpec((rows, 128), lambda c: (c, 0))],
        out_specs=[
            pl.BlockSpec((DN_DH, rows, DN_HEAD_DIM), lambda c: (0, c, 0)),
            pl.BlockSpec((DN_DH, 2 * rows, DN_HEAD_DIM), lambda c: (0, c, 0)),
            pl.BlockSpec((DN_DH, DN_CPS * DN_HEAD_DIM, DN_C), lambda c: (0, c, 0)),
            pl.BlockSpec((DN_DH, rows, DN_C), lambda c: (0, c, 0)),
            pl.BlockSpec((DN_CPS, 8, 128), lambda c: (c, 0, 0)),
        ],
        out_shape=[
            sds((DN_DH, n, DN_HEAD_DIM), F32),
            sds((DN_DH, 2 * n, DN_HEAD_DIM), BF16),
            sds((DN_DH, nc * DN_HEAD_DIM, DN_C), BF16),
            sds((DN_DH, n, DN_C), BF16),
            sds((nc, 8, 128), F32),
        ],
        compiler_params=_cparams(("parallel",)),
        name="dn_chunk",
    )(qn, kn, vn, gates)


def _dn_scan_kernel(uf_ref, wqf_ref, kgf_ref, aqf_ref, glf_ref, ur_ref, wqr_ref, kgr_ref, aqr_ref, glr_ref,
                    of_ref, or_ref, state):
    @pl.when(pl.program_id(0) == 0)
    def _():
        state[...] = jnp.zeros_like(state)

    C = DN_C
    sets = ((uf_ref, wqf_ref, kgf_ref, aqf_ref, glf_ref, of_ref), (ur_ref, wqr_ref, kgr_ref, aqr_ref, glr_ref, or_ref))
    chains = [(d, h) + sets[d] for d in range(2) for h in range(DN_HEADS)]
    ss = [state[d * DN_HEADS + h] for d, h, *_ in chains]
    ts = [jnp.dot(wq_ref[h], s.astype(BF16), preferred_element_type=F32)
          for (d, h, u_ref, wq_ref, *_), s in zip(chains, ss)]
    vbs = [(u_ref[h] - t[:C]).astype(BF16) for (d, h, u_ref, *_), t in zip(chains, ts)]
    os_ = [t[C:] + jnp.dot(aq_ref[h], vb, preferred_element_type=F32)
           for (d, h, u_ref, wq_ref, kg_ref, aq_ref, *_), t, vb in zip(chains, ts, vbs)]
    ns = [s * gl_ref[0][0:1, DN_DH + d * DN_HEADS + h:DN_DH + d * DN_HEADS + h + 1]
          + jnp.dot(kg_ref[h], vb, preferred_element_type=F32)
          for (d, h, u_ref, wq_ref, kg_ref, aq_ref, gl_ref, o_ref), s, vb in zip(chains, ss, vbs)]
    for (d, h, *_, o_ref), o, s_new in zip(chains, os_, ns):
        o_ref[h] = o
        state[d * DN_HEADS + h] = s_new


def dn_scan(u, wq, kgt, aqk, gl):
    n = u.shape[1]
    nc = n // DN_C
    cc = CTX_LEN // DN_C
    fwd = lambda s: s
    rev = lambda s: jnp.where(s < cc, cc - 1 - s, nc + cc - 1 - s)
    specs = []
    for d, cm in enumerate((fwd, rev)):
        specs += [
            pl.BlockSpec((DN_HEADS, DN_C, DN_HEAD_DIM), lambda s, d=d, cm=cm: (d, cm(s), 0)),
            pl.BlockSpec((DN_HEADS, 2 * DN_C, DN_HEAD_DIM), lambda s, d=d, cm=cm: (d, cm(s), 0)),
            pl.BlockSpec((DN_HEADS, DN_HEAD_DIM, DN_C), lambda s, d=d, cm=cm: (d, cm(s), 0)),
            pl.BlockSpec((DN_HEADS, DN_C, DN_C), lambda s, d=d, cm=cm: (d, cm(s), 0)),
            pl.BlockSpec((1, 8, 128), lambda s, cm=cm: (cm(s), 0, 0)),
        ]
    osd = jax.ShapeDtypeStruct((DN_HEADS, n, DN_HEAD_DIM), F32)
    return pl.pallas_call(
        _dn_scan_kernel,
        grid=(nc,),
        in_specs=specs,
        out_specs=[pl.BlockSpec((DN_HEADS, DN_C, DN_HEAD_DIM), lambda s: (0, fwd(s), 0)),
                   pl.BlockSpec((DN_HEADS, DN_C, DN_HEAD_DIM), lambda s: (0, rev(s), 0))],
        out_shape=[osd, osd],
        scratch_shapes=[pltpu.VMEM((DN_DH, DN_HEAD_DIM, DN_HEAD_DIM), F32)],
        compiler_params=_cparams(("arbitrary",)),
        name="dn_scan",
    )(u, wq, kgt, aqk, gl, u, wq, kgt, aqk, gl)


def dn_mixer(z, conv_w, a_log, dt_bias):
    qn, kn, vn, gates = dn_prep(z, conv_w, a_log, dt_bias)
    u, wq, kgt, aqk, gl = dn_chunk(qn, kn, vn, gates)
    return dn_scan(u, wq, kgt, aqk, gl)


S5_SUB = 8
S5_BLK = S5_GROUPS * S5_STATE // S5_SUB
S5_UB = S5_W // S5_SUB
S5_T = 128


def _s5_param_kernel(lr_ref, li_ref, ls_ref, br_ref, bi_ref, ar_ref, ai_ref, bbr_ref, bbi_ref):
    lr = jnp.minimum(lr_ref[...], -1e-4)
    li = li_ref[...]
    dt = jnp.exp(ls_ref[...])
    mag = jnp.exp(lr * dt)
    ar = mag * jnp.cos(li * dt)
    ai = mag * jnp.sin(li * dt)
    den = lr * lr + li * li
    nr, ni = ar - 1.0, ai
    fr = (nr * lr + ni * li) / den
    fi = (ni * lr - nr * li) / den
    ar_ref[...] = ar
    ai_ref[...] = ai
    for h in range(S5_GROUP_CH):
        bbr_ref[h] = fr * br_ref[h] - fi * bi_ref[h]
        bbi_ref[h] = fr * bi_ref[h] + fi * br_ref[h]


def s5_params(lam_re, lam_im, log_step, b_re, b_im):
    r = 2 * S5_GROUPS
    ls = jnp.broadcast_to(log_step.reshape(r, 1), (r, S5_STATE))
    bt = lambda t: jnp.transpose(t.reshape(r, S5_STATE, S5_GROUP_CH), (2, 0, 1))
    sds = jax.ShapeDtypeStruct
    return pl.pallas_call(
        _s5_param_kernel,
        out_shape=[sds((r, S5_STATE), F32), sds((r, S5_STATE), F32),
                   sds((S5_GROUP_CH, r, S5_STATE), F32), sds((S5_GROUP_CH, r, S5_STATE), F32)],
        name="s5_params",
    )(lam_re.reshape(r, S5_STATE), lam_im.reshape(r, S5_STATE), ls, bt(b_re), bt(b_im))


def _s5_scan_kernel(uf_ref, ur_ref, a_ref, rb_ref, cc_ref, yf_ref, yr_ref, bu, hh, st):
    rows = S5_T * S5_SUB

    @pl.when(pl.program_id(0) == 0)
    def _():
        st[...] = jnp.zeros_like(st)

    sub = lax.broadcasted_iota(jnp.int32, (rows, S5_W), 0) % S5_SUB
    blk = lax.broadcasted_iota(jnp.int32, (rows, S5_W), 1) // S5_UB
    own = sub == blk
    for d, u_ref in enumerate((uf_ref, ur_ref)):
        ue = jnp.broadcast_to(u_ref[...][:, None, :], (S5_T, S5_SUB, S5_W)).reshape(rows, S5_W)
        lhs = jnp.where(own, ue, 0.0).astype(BF16)
        bu[d, 0] = jnp.dot(lhs, rb_ref[d, 0], preferred_element_type=F32)
        bu[d, 1] = jnp.dot(lhs, rb_ref[d, 1], preferred_element_type=F32)

    a = [[a_ref[d, c] for c in range(2)] for d in range(2)]

    def body(t, carry):
        fr, fi, rr, ri = carry
        rf = pl.multiple_of(t * S5_SUB, S5_SUB)
        rv = pl.multiple_of((S5_T - 1 - t) * S5_SUB, S5_SUB)
        nfr = a[0][0] * fr - a[0][1] * fi + bu[0, 0, pl.ds(rf, S5_SUB), :]
        nfi = a[0][0] * fi + a[0][1] * fr + bu[0, 1, pl.ds(rf, S5_SUB), :]
        nrr = a[1][0] * rr - a[1][1] * ri + bu[1, 0, pl.ds(rv, S5_SUB), :]
        nri = a[1][0] * ri + a[1][1] * rr + bu[1, 1, pl.ds(rv, S5_SUB), :]
        hh[0, 0, pl.ds(rf, S5_SUB), :] = nfr
        hh[0, 1, pl.ds(rf, S5_SUB), :] = nfi
        hh[1, 0, pl.ds(rv, S5_SUB), :] = nrr
        hh[1, 1, pl.ds(rv, S5_SUB), :] = nri
        return nfr, nfi, nrr, nri

    fin = lax.fori_loop(0, S5_T, body, (st[0, 0], st[0, 1], st[1, 0], st[1, 1]), unroll=8)
    st[0, 0], st[0, 1], st[1, 0], st[1, 1] = fin

    for d, y_ref in enumerate((yf_ref, yr_ref)):
        ye = (jnp.dot(hh[d, 0].astype(BF16), cc_ref[d, 0], preferred_element_type=F32)
              + jnp.dot(hh[d, 1].astype(BF16), cc_ref[d, 1], preferred_element_type=F32))
        ye = jnp.where(own, ye, 0.0)
        y_ref[...] = jnp.sum(ye.reshape(S5_T, S5_SUB, S5_W), axis=1)


def s5_scan(z, a8, rb, cc):
    rows = S5_T * S5_SUB
    n = z.shape[0]
    nt = n // S5_T
    ct = CTX_LEN // S5_T
    ucol = Z_S5 // S5_W

    def rev_tile(i):
        return jnp.where(i < ct, ct - 1 - i, nt + ct - 1 - i)

    full = lambda shape: pl.BlockSpec(shape, lambda i: (0,) * len(shape))
    sds = jax.ShapeDtypeStruct((n, S5_W), F32)
    return pl.pallas_call(
        _s5_scan_kernel,
        grid=(nt,),
        in_specs=[
            pl.BlockSpec((S5_T, S5_W), lambda i: (i, ucol)),
            pl.BlockSpec((S5_T, S5_W), lambda i: (rev_tile(i), ucol)),
            full((2, 2, S5_SUB, S5_BLK)),
            full((2, 2, S5_W, S5_BLK)),
            full((2, 2, S5_BLK, S5_W)),
        ],
        out_specs=[pl.BlockSpec((S5_T, S5_W), lambda i: (i, 0)),
                   pl.BlockSpec((S5_T, S5_W), lambda i: (rev_tile(i), 0))],
        out_shape=[sds, sds],
        scratch_shapes=[pltpu.VMEM((2, 2, rows, S5_BLK), F32), pltpu.VMEM((2, 2, rows, S5_BLK), F32),
                        pltpu.VMEM((2, 2, S5_SUB, S5_BLK), F32)],
        compiler_params=_cparams(("arbitrary",)),
        name="s5_scan",
    )(z, z, a8, rb, cc)


def s5_mixer(z, lam_re, lam_im, log_step, b_re, b_im, c_re, c_im):
    ar, ai, bbr, bbi = s5_params(lam_re, lam_im, log_step, b_re, b_im)
    g_blk = jax.nn.one_hot(jnp.arange(S5_GROUPS) % (S5_GROUPS // S5_SUB), S5_GROUPS // S5_SUB, dtype=F32)

    def place_b(bb):
        bb = jnp.transpose(bb.reshape(S5_GROUP_CH, 2, S5_GROUPS, S5_STATE), (1, 2, 0, 3))
        return jnp.einsum('dghp,gj->dghjp', bb, g_blk).reshape(2, S5_W, S5_BLK)

    def place_c(cm):
        return jnp.einsum('dghp,gj->djpgh', cm, g_blk).reshape(2, S5_BLK, S5_W)

    rb = jnp.stack([place_b(bbr), place_b(bbi)], axis=1).astype(BF16)
    cc = jnp.stack([place_c(c_re), -place_c(c_im)], axis=1).astype(BF16)
    a8 = jnp.stack([ar.reshape(2, S5_SUB, S5_BLK), ai.reshape(2, S5_SUB, S5_BLK)], axis=1)
    return s5_scan(z, a8, rb, cc)


def _permute_w_in(w):
    n_gate = 4 * DN_HEADS
    attn = w[:, :QKV_W]
    dn = w[:, QKV_W:QKV_W + 4 * DN_W]
    gates = w[:, QKV_W + 4 * DN_W:QKV_W + 4 * DN_W + n_gate]
    s5 = w[:, QKV_W + 4 * DN_W + n_gate:]
    pad = jnp.zeros((w.shape[0], Z_S5 - Z_GATES - n_gate), w.dtype)
    return jnp.concatenate([dn, attn, gates, pad, s5], axis=1).astype(BF16)


def kernel(x, c, ctx, c_ctx, w_ada, b_ada, norm1_w, norm2_w, w_in, w_out, attn_q_norm, attn_k_norm, attn_sink,
           dn_conv, dn_a_log, dn_dt_bias, dn_o_norm, s5_lam_re, s5_lam_im, s5_log_step, s5_b_re, s5_b_im,
           s5_c_re, s5_c_im, s5_d, s5_w_glu, moe_w_grp, moe_b_grp, moe_w_rt, moe_b_rt, moe_w1, moe_w3, moe_w2):
    b, n, d = x.shape
    assert b == 1 and ctx.shape[1] == CTX_LEN and d == D_MODEL
    lc = CTX_LEN
    xa = jnp.concatenate([ctx[0], x[0]], axis=0)
    mods = adaln(c_ctx, c[0], w_ada, b_ada)
    cos_t, sin_t = rope_tables(n)
    w1_all = moe_w1.reshape(DEPTH * N_EXPERTS, D_MODEL, EXPERT_HIDDEN)
    w3_all = moe_w3.reshape(DEPTH * N_EXPERTS, D_MODEL, EXPERT_HIDDEN)
    w2_all = moe_w2.reshape(DEPTH * N_EXPERTS, EXPERT_HIDDEN, D_MODEL)
    for layer in range(DEPTH):
        mod = mods[layer, 0:2].reshape(2, 6, D_MODEL)
        mod = jnp.concatenate([mod, jnp.zeros((2, 2, D_MODEL), F32)], axis=1)
        z = in_proj(xa, norm1_w[layer], mod, _permute_w_in(w_in[layer]))
        mix_a = attn_mixer(z, attn_q_norm[layer], attn_k_norm[layer], attn_sink[layer], cos_t, sin_t)
        o_f, o_r = dn_mixer(z, dn_conv[layer], dn_a_log[layer], dn_dt_bias[layer])
        yf, yr = s5_mixer(z, s5_lam_re[layer], s5_lam_im[layer], s5_log_step[layer], s5_b_re[layer],
                          s5_b_im[layer], s5_c_re[layer], s5_c_im[layer])
        w_router = jnp.concatenate([moe_w_grp[layer], moe_w_rt[layer],
                                    jnp.zeros((D_MODEL, 128 - N_GROUPS - N_EXPERTS), F32)], axis=1)
        b_router = jnp.concatenate([moe_b_grp[layer], moe_b_rt[layer],
                                    jnp.zeros((128 - N_GROUPS - N_EXPERTS,), F32)]).reshape(1, 128)
        xa, h2, route = out_proj(mix_a, o_f, o_r, z, yf, yr, xa, mod, w_out[layer].astype(BF16), norm2_w[layer],
                                 w_router, b_router, dn_o_norm[layer], s5_d[layer], s5_w_glu[layer].astype(BF16))
        xa = moe_block(xa, h2, route, mod, w1_all, w3_all, w2_all, layer, skip_ctx=(layer == DEPTH - 1))
    return xa[None]
```

```python
import functools

import jax
import jax.numpy as jnp
import numpy as np
from jax import lax
from jax.experimental import pallas as pl
from jax.experimental.pallas import tpu as pltpu

F32 = jnp.float32
BF16 = jnp.bfloat16
HIGHEST = lax.Precision.HIGHEST

D_MODEL = 2048
SEQ = 8192
DEPTH = 2
GRID_W = 64
CTX_LEN = 256
N_ALL = CTX_LEN + SEQ
EPS = 1e-6
NEG_INF = -1e30

ATTN_HEADS = 12
ATTN_KV_HEADS = 4
ATTN_HEAD_DIM = 64
ATTN_GROUP = ATTN_HEADS // ATTN_KV_HEADS
ATTN_W = ATTN_HEADS * ATTN_HEAD_DIM
ATTN_KV_W = ATTN_KV_HEADS * ATTN_HEAD_DIM
WINDOW = 128
ATTN_BLOCK = 128
ROPE_BASE = 10000.0
DN_HEADS = 6
DN_HEAD_DIM = 128
DN_W = DN_HEADS * DN_HEAD_DIM
DN_CONV = 5
DN_CHUNK = 64
S5_W = D_MODEL - ATTN_W - DN_W
S5_GROUP_CH = 16
S5_GROUPS = S5_W // S5_GROUP_CH
S5_STATE = 64
N_GROUPS = 4
EXPERTS_PER_GROUP = 8
N_EXPERTS = N_GROUPS * EXPERTS_PER_GROUP
TOP_K = 2
EXPERT_HIDDEN = 512

Z_DNQ, Z_DNK, Z_DNV, Z_DNG = 0, 768, 1536, 2304
Z_Q, Z_K, Z_V = 3072, 3840, 4096
Z_GATES = 4352
Z_BETA, Z_ALPHA = Z_GATES, Z_GATES + 2 * DN_HEADS
Z_S5 = 4608
Z_W = 5120

VMEM_LIMIT = 56 * 1024 * 1024


def _cparams(sem, vmem=VMEM_LIMIT):
    return pltpu.CompilerParams(dimension_semantics=sem, vmem_limit_bytes=vmem)


ADALN_TN = 1024


def _adaln_kernel(c_ref, w_ref, b_ref, o_ref):
    reps = ADALN_TN // 128

    def body(kb, acc):
        r0 = pl.multiple_of(kb * 8, 8)
        w = w_ref[0, pl.ds(r0, 8), :]
        out = []
        for r in range(2):
            cv = c_ref[r, pl.ds(r0, 8), :]
            s = cv * jax.nn.sigmoid(cv)
            out.append(acc[r] + w * jnp.concatenate([s] * reps, axis=1))
        return tuple(out)

    zero = jnp.zeros((8, ADALN_TN), F32)
    acc = lax.fori_loop(0, D_MODEL // 8, body, (zero, zero), unroll=4)
    rows = [jnp.sum(a, axis=0, keepdims=True) + b_ref[0] for a in acc]
    o_ref[0] = jnp.concatenate(rows + [jnp.zeros((6, ADALN_TN), F32)], axis=0)


def adaln(c_ctx, c, w_ada, b_ada):
    L = w_ada.shape[0]
    n6 = 6 * D_MODEL
    cb = jnp.broadcast_to(jnp.stack([c_ctx, c])[:, :, None], (2, D_MODEL, 128))
    return pl.pallas_call(
        _adaln_kernel,
        grid=(L, n6 // ADALN_TN),
        in_specs=[
            pl.BlockSpec((2, D_MODEL, 128), lambda l, j: (0, 0, 0)),
            pl.BlockSpec((1, D_MODEL, ADALN_TN), lambda l, j: (l, 0, j)),
            pl.BlockSpec((1, 1, ADALN_TN), lambda l, j: (l, 0, j)),
        ],
        out_specs=pl.BlockSpec((1, 8, ADALN_TN), lambda l, j: (l, 0, j)),
        out_shape=jax.ShapeDtypeStruct((L, 8, n6), F32),
        compiler_params=_cparams(("parallel", "parallel")),
        name="adaln",
    )(cb, w_ada, b_ada.reshape(L, 1, n6))


NORM_ROWS = 64


def _is_ctx_rows(base, rows):
    return base + lax.broadcasted_iota(jnp.int32, (rows, 1), 0) < CTX_LEN


def _row_is_ctx(tm):
    return _is_ctx_rows(pl.program_id(0) * tm, tm)


def _norm_mod(x, nw, mod_ref, shift_i, scale_i, is_ctx):
    ms = jnp.mean(x * x, axis=-1, keepdims=True)
    h = x * lax.rsqrt(ms + EPS) * nw
    sc = jnp.where(is_ctx, mod_ref[0, scale_i:scale_i + 1, :], mod_ref[1, scale_i:scale_i + 1, :])
    sh = jnp.where(is_ctx, mod_ref[0, shift_i:shift_i + 1, :], mod_ref[1, shift_i:shift_i + 1, :])
    return h * (1.0 + sc) + sh


def _in_proj_kernel(x_ref, nw_ref, mod_ref, w_ref, o_ref, h_scr, *, tm):
    @pl.when(pl.program_id(1) == 0)
    def _():
        def chunk(c, carry):
            r0 = pl.multiple_of(c * NORM_ROWS, NORM_ROWS)
            h = _norm_mod(x_ref[pl.ds(r0, NORM_ROWS), :], nw_ref[...], mod_ref, 0, 1,
                          _is_ctx_rows(pl.program_id(0) * tm + r0, NORM_ROWS))
            h_scr[pl.ds(r0, NORM_ROWS), :] = h.astype(BF16)
            return carry
        lax.fori_loop(0, tm // NORM_ROWS, chunk, 0)

    o_ref[...] = jnp.dot(h_scr[...], w_ref[...], preferred_element_type=F32).astype(o_ref.dtype)


def in_proj(xa, norm_w, mod, w_in_p):
    tm, tn = 1024, 1024
    n = xa.shape[0]
    return pl.pallas_call(
        functools.partial(_in_proj_kernel, tm=tm),
        grid=(pl.cdiv(n, tm), Z_W // tn),
        in_specs=[
            pl.BlockSpec((tm, D_MODEL), lambda i, j: (i, 0)),
            pl.BlockSpec((1, D_MODEL), lambda i, j: (0, 0)),
            pl.BlockSpec((2, 8, D_MODEL), lambda i, j: (0, 0, 0)),
            pl.BlockSpec((D_MODEL, tn), lambda i, j: (0, j)),
        ],
        out_specs=pl.BlockSpec((tm, tn), lambda i, j: (i, j)),
        out_shape=jax.ShapeDtypeStruct((n, Z_W), BF16),
        scratch_shapes=[pltpu.VMEM((tm, D_MODEL), BF16)],
        compiler_params=_cparams(("parallel", "arbitrary")),
        name="in_proj",
    )(xa, norm_w.reshape(1, D_MODEL), mod, w_in_p)


def _out_proj_kernel(a_ref, of_ref, or_ref, zg_ref, onw_ref, zu_ref, yf_ref, yr_ref, dsk_ref, wg_ref,
                     x_ref, mod_ref, w_ref, nw_ref, wr_ref, br_ref, xo_ref, h_ref, rt_ref, *, tm):
    is_ctx = _row_is_ctx(tm)
    b_parts = []
    for hd in range(DN_HEADS):
        o = of_ref[hd].astype(F32) + or_ref[hd].astype(F32)
        o = o * lax.rsqrt(jnp.mean(o * o, axis=-1, keepdims=True) + EPS) * onw_ref[...]
        g = zg_ref[:, hd * DN_HEAD_DIM:(hd + 1) * DN_HEAD_DIM].astype(F32)
        b_parts.append((o * (g * jax.nn.sigmoid(g))).astype(BF16))
    b = jnp.concatenate(b_parts, axis=1)
    ys = jax.nn.gelu(zu_ref[...].astype(F32) * dsk_ref[...] + yf_ref[...] + yr_ref[...])
    s = (ys * jax.nn.sigmoid(jnp.dot(ys.astype(BF16), wg_ref[...], preferred_element_type=F32))).astype(BF16)
    y = (jnp.dot(a_ref[...], w_ref[0:ATTN_W, :], preferred_element_type=F32)
         + jnp.dot(b, w_ref[ATTN_W:ATTN_W + DN_W, :], preferred_element_type=F32)
         + jnp.dot(s, w_ref[ATTN_W + DN_W:, :], preferred_element_type=F32))
    gate = jnp.where(is_ctx, mod_ref[0, 2:3, :], mod_ref[1, 2:3, :])
    xn = x_ref[...] + gate * y
    xo_ref[...] = xn
    h = _norm_mod(xn, nw_ref[...], mod_ref, 3, 4, is_ctx)
    h_ref[...] = h
    wr = wr_ref[...]
    h_hi = h.astype(BF16)
    h_lo = (h - h_hi.astype(F32)).astype(BF16)
    w_hi = wr.astype(BF16)
    w_lo = (wr - w_hi.astype(F32)).astype(BF16)
    lg = (jnp.dot(h_hi, w_hi, preferred_element_type=F32) + jnp.dot(h_hi, w_lo, preferred_element_type=F32)
          + jnp.dot(h_lo, w_hi, preferred_element_type=F32) + br_ref[...])
    rt_ref[...] = _route(lg)


def out_proj(mix_a, o_f, o_r, z, yf, yr, xa, mod, w_out_b, norm2_w, w_router, b_router, o_norm_w, d_skip, w_glu_b):
    tm = 256
    n = xa.shape[0]
    row = lambda i: (i, 0)
    const2 = lambda i: (0, 0)
    hm = pl.BlockSpec((DN_HEADS, tm, DN_HEAD_DIM), lambda i: (0, i, 0))
    return pl.pallas_call(
        functools.partial(_out_proj_kernel, tm=tm),
        grid=(n // tm,),
        in_specs=[
            pl.BlockSpec((tm, ATTN_W), row),
            hm, hm,
            pl.BlockSpec((tm, DN_W), lambda i: (i, Z_DNG // DN_W)),
            pl.BlockSpec((1, DN_HEAD_DIM), const2),
            pl.BlockSpec((tm, S5_W), lambda i: (i, Z_S5 // S5_W)),
            pl.BlockSpec((tm, S5_W), row),
            pl.BlockSpec((tm, S5_W), row),
            pl.BlockSpec((1, S5_W), const2),
            pl.BlockSpec((S5_W, S5_W), const2),
            pl.BlockSpec((tm, D_MODEL), row),
            pl.BlockSpec((2, 8, D_MODEL), lambda i: (0, 0, 0)),
            pl.BlockSpec((D_MODEL, D_MODEL), const2),
            pl.BlockSpec((1, D_MODEL), const2),
            pl.BlockSpec((D_MODEL, 128), const2),
            pl.BlockSpec((1, 128), const2),
        ],
        out_specs=[
            pl.BlockSpec((tm, D_MODEL), row),
            pl.BlockSpec((tm, D_MODEL), row),
            pl.BlockSpec((tm, 128), row),
        ],
        out_shape=[
            jax.ShapeDtypeStruct((n, D_MODEL), F32),
            jax.ShapeDtypeStruct((n, D_MODEL), F32),
            jax.ShapeDtypeStruct((n, 128), F32),
        ],
        compiler_params=_cparams(("parallel",)),
        name="out_proj",
    )(mix_a, o_f, o_r, z, o_norm_w.reshape(1, DN_HEAD_DIM), z, yf, yr, d_skip.reshape(1, S5_W), w_glu_b,
      xa, mod, w_out_b, norm2_w.reshape(1, D_MODEL), w_router, b_router)


def _route(lg):
    tm = lg.shape[0]
    lane = lax.broadcasted_iota(jnp.int32, (tm, 128), 1)
    is_g = lane < N_GROUPS
    gl = jnp.where(is_g, lg, NEG_INF)
    gmax = jnp.max(gl, axis=-1, keepdims=True)
    gidx = jnp.min(jnp.where((gl == gmax) & is_g, lane, 128), axis=-1, keepdims=True)
    gsum = jnp.sum(jnp.where(is_g, jnp.exp(gl - gmax), 0.0), axis=-1, keepdims=True)
    g_w = 1.0 / gsum
    e_lane = lane - N_GROUPS
    in_grp = (e_lane >= gidx * EXPERTS_PER_GROUP) & (e_lane < (gidx + 1) * EXPERTS_PER_GROUP)
    el = jnp.where(in_grp, lg, NEG_INF)
    v1 = jnp.max(el, axis=-1, keepdims=True)
    i1 = jnp.min(jnp.where((el == v1) & in_grp, e_lane, 128), axis=-1, keepdims=True)
    el2 = jnp.where(e_lane == i1, NEG_INF, el)
    in2 = in_grp & (e_lane != i1)
    v2 = jnp.max(el2, axis=-1, keepdims=True)
    i2 = jnp.min(jnp.where((el2 == v2) & in2, e_lane, 128), axis=-1, keepdims=True)
    e2 = jnp.exp(v2 - v1)
    w1 = g_w / (1.0 + e2)
    w2 = g_w * e2 / (1.0 + e2)
    out = jnp.where(lane == 0, i1.astype(F32), 0.0)
    out = jnp.where(lane == 1, i2.astype(F32), out)
    out = jnp.where(lane == 2, w1, out)
    out = jnp.where(lane == 3, w2, out)
    return out


MOE_TILE = 256


def _moe_tiles(n):
    return TOP_K * n // MOE_TILE + N_EXPERTS


def _expert_kernel(te_ref, ts_ref, tok_ref, nt_ref, h_hbm, w1_ref, w3_ref, w2_ref, y_ref,
                   xbuf0, xbuf1, sem, w1b, w3b, w2b):
    i = pl.program_id(0)
    n_tiles = nt_ref[0]
    last = tok_ref.shape[0] - 1
    bufs = (xbuf0, xbuf1)

    def gather_start(tile, slot):
        base = ts_ref[tile]
        for r in range(MOE_TILE):
            tok = tok_ref[jnp.minimum(base + r, last)]
            pltpu.make_async_copy(h_hbm.at[pl.ds(tok, 1)], bufs[slot].at[pl.ds(r, 1)], sem.at[slot]).start()

    def gather_wait(slot):
        pltpu.make_async_copy(bufs[slot], bufs[slot], sem.at[slot]).wait()

    @pl.when(i == 0)
    def _():
        gather_start(0, 0)

    prev_e = te_ref[jnp.maximum(i - 1, 0)]
    new_e = (i == 0) | (te_ref[i] != prev_e)

    @pl.when((i < n_tiles) & new_e)
    def _():
        w1b[...] = w1_ref[0].astype(BF16)
        w3b[...] = w3_ref[0].astype(BF16)
        w2b[...] = w2_ref[0].astype(BF16)

    for slot in range(2):
        @pl.when((i < n_tiles) & (i % 2 == slot))
        def _():
            gather_wait(slot)
            gather_start(jnp.minimum(i + 1, n_tiles - 1), 1 - slot)
            xt = bufs[slot][...].astype(BF16)
            a = jnp.dot(xt, w1b[...], preferred_element_type=F32)
            u = jnp.dot(xt, w3b[...], preferred_element_type=F32)
            act = (a * jax.nn.sigmoid(a)) * u
            y_ref[...] = jnp.dot(act.astype(BF16), w2b[...], preferred_element_type=F32)

            @pl.when(i == n_tiles - 1)
            def _():
                gather_wait(1 - slot)

    @pl.when(i >= n_tiles)
    def _():
        y_ref[...] = jnp.zeros_like(y_ref)


def expert_mlp(tile_expert, tile_start, sorted_tok, n_tiles, h2, w1, w3, w2):
    e_map = lambda i, te, ts, tok, nt: (te[i], 0, 0)
    moe_tiles = tile_expert.shape[0]
    grid_spec = pltpu.PrefetchScalarGridSpec(
        num_scalar_prefetch=4,
        grid=(moe_tiles,),
        in_specs=[
            pl.BlockSpec(memory_space=pl.ANY),
            pl.BlockSpec((1, D_MODEL, EXPERT_HIDDEN), e_map),
            pl.BlockSpec((1, D_MODEL, EXPERT_HIDDEN), e_map),
            pl.BlockSpec((1, EXPERT_HIDDEN, D_MODEL), e_map),
        ],
        out_specs=pl.BlockSpec((MOE_TILE, D_MODEL), lambda i, te, ts, tok, nt: (i, 0)),
        scratch_shapes=[
            pltpu.VMEM((MOE_TILE, D_MODEL), F32),
            pltpu.VMEM((MOE_TILE, D_MODEL), F32),
            pltpu.SemaphoreType.DMA((2,)),
            pltpu.VMEM((D_MODEL, EXPERT_HIDDEN), BF16),
            pltpu.VMEM((D_MODEL, EXPERT_HIDDEN), BF16),
            pltpu.VMEM((EXPERT_HIDDEN, D_MODEL), BF16),
        ],
    )
    return pl.pallas_call(
        _expert_kernel,
        grid_spec=grid_spec,
        out_shape=jax.ShapeDtypeStruct((moe_tiles * MOE_TILE, D_MODEL), F32),
        compiler_params=_cparams(("arbitrary",)),
        name="expert_mlp",
    )(tile_expert, tile_start, sorted_tok, n_tiles, h2, w1, w3, w2)


CMB_TILE = 256


def _combine_kernel(pos_ref, y_hbm, x_ref, route_ref, mod_ref, o_ref, ybuf0, ybuf1, sem, *, first_tile):
    i = pl.program_id(0)
    nt = pl.num_programs(0)
    bufs = (ybuf0, ybuf1)

    def gather_start(tile, slot):
        base = (tile + first_tile) * (CMB_TILE * TOP_K)
        for r in range(CMB_TILE):
            for k in range(TOP_K):
                p = pos_ref[base + r * TOP_K + k]
                pltpu.make_async_copy(y_hbm.at[pl.ds(p, 1)], bufs[slot].at[k, pl.ds(r, 1)], sem.at[slot]).start()

    def gather_wait(slot):
        pltpu.make_async_copy(bufs[slot], bufs[slot], sem.at[slot]).wait()

    @pl.when(i == 0)
    def _():
        gather_start(0, 0)

    for slot in range(2):
        @pl.when(i % 2 == slot)
        def _():
            gather_wait(slot)
            gather_start(jnp.minimum(i + 1, nt - 1), 1 - slot)
            is_ctx = _is_ctx_rows((i + first_tile) * CMB_TILE, CMB_TILE)
            gate = jnp.where(is_ctx, mod_ref[0, 5:6, :], mod_ref[1, 5:6, :])
            w0 = route_ref[:, TOP_K:TOP_K + 1]
            w1 = route_ref[:, TOP_K + 1:TOP_K + 2]
            o_ref[...] = x_ref[...] + gate * (w0 * bufs[slot][0] + w1 * bufs[slot][1])

            @pl.when(i == nt - 1)
            def _():
                gather_wait(1 - slot)


def moe_combine(pos, y_sorted, xa, route, mod, skip_ctx):
    first_tile = CTX_LEN // CMB_TILE if skip_ctx else 0
    n = xa.shape[0] - first_tile * CMB_TILE
    grid_spec = pltpu.PrefetchScalarGridSpec(
        num_scalar_prefetch=1,
        grid=(n // CMB_TILE,),
        in_specs=[
            pl.BlockSpec(memory_space=pl.ANY),
            pl.BlockSpec((CMB_TILE, D_MODEL), lambda i, pos: (i + first_tile, 0)),
            pl.BlockSpec((CMB_TILE, 128), lambda i, pos: (i + first_tile, 0)),
            pl.BlockSpec((2, 8, D_MODEL), lambda i, pos: (0, 0, 0)),
        ],
        out_specs=pl.BlockSpec((CMB_TILE, D_MODEL), lambda i, pos: (i, 0)),
        scratch_shapes=[
            pltpu.VMEM((TOP_K, CMB_TILE, D_MODEL), F32),
            pltpu.VMEM((TOP_K, CMB_TILE, D_MODEL), F32),
            pltpu.SemaphoreType.DMA((2,)),
        ],
    )
    return pl.pallas_call(
        functools.partial(_combine_kernel, first_tile=first_tile),
        grid_spec=grid_spec,
        out_shape=jax.ShapeDtypeStruct((n, D_MODEL), F32),
        compiler_params=_cparams(("arbitrary",)),
        name="moe_combine",
    )(pos, y_sorted, xa, route, mod)


def moe_dispatch_plan(route):
    n = route.shape[0]
    eid = route[:, 0:TOP_K].astype(jnp.int32).reshape(-1)
    p_total = eid.shape[0]
    experts = jnp.arange(N_EXPERTS, dtype=jnp.int32)
    counts = jnp.sum((eid[:, None] == experts[None, :]).astype(jnp.int32), axis=0)
    tiles_per = (counts + MOE_TILE - 1) // MOE_TILE
    tile_off = jnp.cumsum(tiles_per) - tiles_per
    off = jnp.cumsum(counts) - counts
    pair_ids = jnp.arange(p_total, dtype=jnp.int32)
    e_sorted, sorted_pair = lax.sort((eid, pair_ids), num_keys=1, is_stable=True)
    sorted_tok = sorted_pair // TOP_K
    s_onehot = (e_sorted[:, None] == experts[None, :]).astype(jnp.int32)
    pos_sorted = pair_ids + jnp.sum(s_onehot * (tile_off * MOE_TILE - off)[None, :], axis=1)
    _, pos = lax.sort((sorted_pair, pos_sorted), num_keys=1)
    moe_tiles = _moe_tiles(n)
    n_tiles = jnp.sum(tiles_per).astype(jnp.int32)
    tile_ids = jnp.arange(moe_tiles, dtype=jnp.int32)
    tile_expert = jnp.sum((tile_ids[:, None] >= (tile_off + tiles_per)[None, :]).astype(jnp.int32), axis=1)
    tile_expert = jnp.minimum(tile_expert, N_EXPERTS - 1)
    t_onehot = (tile_expert[:, None] == experts[None, :]).astype(jnp.int32)
    tile_start = (jnp.sum(t_onehot * off[None, :], axis=1)
                  + (tile_ids - jnp.sum(t_onehot * tile_off[None, :], axis=1)) * MOE_TILE)
    return tile_expert, tile_start, sorted_tok, n_tiles.reshape(1), pos


def moe_block(xa, h2, route, mod, w1, w3, w2, layer, skip_ctx):
    tile_expert, tile_start, sorted_tok, n_tiles, pos = moe_dispatch_plan(route)
    y_sorted = expert_mlp(tile_expert + layer * N_EXPERTS, tile_start, sorted_tok, n_tiles, h2, w1, w3, w2)
    return moe_combine(pos, y_sorted, xa, route, mod, skip_ctx)


QK_W = ATTN_W + ATTN_KV_W
QKV_W = QK_W + ATTN_KV_W
ROPE_F = ATTN_HEAD_DIM // 4


def _norm_rope_heads(x, nw, cos, sin, out_ref, n_heads, scale):
    tm, width = x.shape
    xw = x * nw
    lane = lax.broadcasted_iota(jnp.int32, (tm, width), 1)
    odd = (lane // ROPE_F) % 2 == 1
    xs = jnp.where(odd, pltpu.roll(xw, ROPE_F, 1), pltpu.roll(xw, width - ROPE_F, 1))
    heads = [slice(h * ATTN_HEAD_DIM, (h + 1) * ATTN_HEAD_DIM) for h in range(n_heads)]
    invs = [lax.rsqrt(jnp.mean(x[:, sl] * x[:, sl], axis=-1, keepdims=True) + EPS) * scale for sl in heads]
    for h, (sl, inv) in enumerate(zip(heads, invs)):
        out_ref[h] = ((xw[:, sl] * cos + xs[:, sl] * sin) * inv).astype(out_ref.dtype)


def _attn_prep_kernel(zq_ref, zk_ref, zv_ref, qw_ref, kw_ref, cos_ref, sin_ref, q_ref, k_ref, v_ref):
    cos = cos_ref[...]
    sin = sin_ref[...]
    _norm_rope_heads(zq_ref[...].astype(F32), qw_ref[...], cos, sin, q_ref, ATTN_HEADS, ATTN_HEAD_DIM ** -0.5)
    _norm_rope_heads(zk_ref[...].astype(F32), kw_ref[...], cos, sin, k_ref, ATTN_KV_HEADS, 1.0)
    ones = jnp.ones((zv_ref.shape[0], ATTN_HEAD_DIM), BF16)
    for h in range(ATTN_KV_HEADS):
        v_ref[h] = jnp.concatenate([zv_ref[:, h * ATTN_HEAD_DIM:(h + 1) * ATTN_HEAD_DIM].astype(BF16), ones], axis=1)


def attn_prep(z, q_norm_w, k_norm_w, cos_t, sin_t):
    tm = 256
    n = z.shape[0]
    qw = jnp.tile(q_norm_w, ATTN_HEADS).reshape(1, ATTN_W)
    kw = jnp.tile(k_norm_w, ATTN_KV_HEADS).reshape(1, ATTN_KV_W)
    hm = lambda h, w=ATTN_HEAD_DIM: pl.BlockSpec((h, tm, w), lambda i: (0, i, 0))
    sds = lambda h, w=ATTN_HEAD_DIM: jax.ShapeDtypeStruct((h, n, w), BF16)
    return pl.pallas_call(
        _attn_prep_kernel,
        grid=(n // tm,),
        in_specs=[
            pl.BlockSpec((tm, ATTN_W), lambda i: (i, Z_Q // ATTN_W)),
            pl.BlockSpec((tm, ATTN_KV_W), lambda i: (i, Z_K // ATTN_KV_W)),
            pl.BlockSpec((tm, ATTN_KV_W), lambda i: (i, Z_V // ATTN_KV_W)),
            pl.BlockSpec((1, ATTN_W), lambda i: (0, 0)),
            pl.BlockSpec((1, ATTN_KV_W), lambda i: (0, 0)),
            pl.BlockSpec((tm, ATTN_HEAD_DIM), lambda i: (i, 0)),
            pl.BlockSpec((tm, ATTN_HEAD_DIM), lambda i: (i, 0)),
        ],
        out_specs=[hm(ATTN_HEADS), hm(ATTN_KV_HEADS), hm(ATTN_KV_HEADS, 2 * ATTN_HEAD_DIM)],
        out_shape=[sds(ATTN_HEADS), sds(ATTN_KV_HEADS), sds(ATTN_KV_HEADS, 2 * ATTN_HEAD_DIM)],
        compiler_params=_cparams(("parallel",)),
        name="attn_prep",
    )(z, z, z, qw, kw, cos_t, sin_t)


def _attn_kernel(sink_ref, q_ref, kp_ref, ko_ref, kn_ref, kc_ref, vp_ref, vo_ref, vn_ref, vc_ref, o_ref, *, n_blocks):
    i = pl.program_id(0)
    ctx_blocks = CTX_LEN // ATTN_BLOCK
    B = ATTN_BLOCK
    rows = ATTN_GROUP * B
    ncol = 3 * B + CTX_LEN
    r = lax.broadcasted_iota(jnp.int32, (rows, ncol), 0) % B
    c = lax.broadcasted_iota(jnp.int32, (rows, ncol), 1)
    lo = jnp.where(i > ctx_blocks, 0, B)
    hi = jnp.where(i < n_blocks - 1, 3 * B, 2 * B)
    hi = jnp.where(i < ctx_blocks, 0, hi)
    band = (c >= r) & (c <= r + 2 * WINDOW) & (c >= lo) & (c < hi)
    mask = band | (c >= 3 * B)
    grp = lax.broadcasted_iota(jnp.int32, (rows, 1), 0) // B
    for kh in range(ATTN_KV_HEADS):
        q3 = jnp.concatenate([q_ref[kh * ATTN_GROUP + g] for g in range(ATTN_GROUP)], axis=0)
        kcat = jnp.concatenate([kp_ref[kh], ko_ref[kh], kn_ref[kh], kc_ref[kh]], axis=0)
        vcat = jnp.concatenate([vp_ref[kh], vo_ref[kh], vn_ref[kh], vc_ref[kh]], axis=0)
        s = lax.dot_general(q3, kcat, (((1,), (1,)), ((), ())), preferred_element_type=F32)
        s = jnp.where(mask, s, NEG_INF)
        sink = jnp.zeros((rows, 1), F32)
        for g in range(ATTN_GROUP):
            sink = jnp.where(grp == g, sink_ref[kh * ATTN_GROUP + g], sink)
        m = jnp.maximum(jnp.max(s, axis=-1, keepdims=True), sink)
        p = jnp.exp(s - m)
        pv = jnp.dot(p.astype(BF16), vcat, preferred_element_type=F32)
        den = pv[:, ATTN_HEAD_DIM:ATTN_HEAD_DIM + 1] + jnp.exp(sink - m)
        o = pv[:, :ATTN_HEAD_DIM] / den
        for g in range(ATTN_GROUP):
            h = kh * ATTN_GROUP + g
            o_ref[:, h * ATTN_HEAD_DIM:(h + 1) * ATTN_HEAD_DIM] = o[g * B:(g + 1) * B].astype(o_ref.dtype)


def attention(qh, kh, vh, sink):
    n = qh.shape[1]
    B = ATTN_BLOCK
    nblk = n // B
    cb = CTX_LEN // B
    prev = lambda i: (0, jnp.clip(i - 1, cb, nblk - 1), 0)
    own = lambda i: (0, i, 0)
    nxt = lambda i: (0, jnp.clip(i + 1, cb, nblk - 1), 0)
    ctx = lambda i: (0, 0, 0)
    kv = lambda m, w=ATTN_HEAD_DIM: pl.BlockSpec((ATTN_KV_HEADS, B, w), m)
    kvc = lambda w=ATTN_HEAD_DIM: pl.BlockSpec((ATTN_KV_HEADS, CTX_LEN, w), ctx)
    vw = 2 * ATTN_HEAD_DIM
    return pl.pallas_call(
        functools.partial(_attn_kernel, n_blocks=nblk),
        grid=(nblk,),
        in_specs=[
            pl.BlockSpec(memory_space=pltpu.SMEM),
            pl.BlockSpec((ATTN_HEADS, B, ATTN_HEAD_DIM), own),
            kv(prev), kv(own), kv(nxt), kvc(),
            kv(prev, vw), kv(own, vw), kv(nxt, vw), kvc(vw),
        ],
        out_specs=pl.BlockSpec((B, ATTN_W), lambda i: (i, 0)),
        out_shape=jax.ShapeDtypeStruct((n, ATTN_W), BF16),
        compiler_params=_cparams(("parallel",)),
        name="attention",
    )(sink, qh, kh, kh, kh, kh, vh, vh, vh, vh)


def rope_tables(n_lat):
    rows = n_lat // GRID_W
    row = np.repeat(np.arange(rows), GRID_W).astype(np.float32)
    col = np.tile(np.arange(GRID_W), rows).astype(np.float32)
    inv = (ROPE_BASE ** (-np.arange(ROPE_F, dtype=np.float32) / ROPE_F)).astype(np.float32)
    ar, ac = row[:, None] * inv, col[:, None] * inv
    cos = np.concatenate([np.cos(ar), np.cos(ar), np.cos(ac), np.cos(ac)], axis=1)
    sin = np.concatenate([-np.sin(ar), np.sin(ar), -np.sin(ac), np.sin(ac)], axis=1)
    cos = np.concatenate([np.ones((CTX_LEN, ATTN_HEAD_DIM), np.float32), cos], axis=0)
    sin = np.concatenate([np.zeros((CTX_LEN, ATTN_HEAD_DIM), np.float32), sin], axis=0)
    return jnp.asarray(cos, F32), jnp.asarray(sin, F32)


def attn_mixer(z, q_norm_w, k_norm_w, sink, cos_t, sin_t):
    qh, kh, vh = attn_prep(z, q_norm_w, k_norm_w, cos_t, sin_t)
    return attention(qh, kh, vh, sink)


DN_QKV = 3 * DN_W
DN_HALO = 16
DN_DH = 2 * DN_HEADS
DN_C = DN_CHUNK
DN_CPS = 4


def _dn_prep_kernel(zm_ref, zp_ref, zn_ref, zg_ref, cw_ref, al_ref, dtb_ref, q_ref, k_ref, v_ref, g_ref, *, tm, n_tiles):
    i = pl.program_id(0)
    ctx_tiles = CTX_LEN // tm
    has_prev = (i != 0) & (i != ctx_tiles)
    has_next = (i != ctx_tiles - 1) & (i != n_tiles - 1)
    prev = jnp.where(has_prev, zp_ref[...].astype(F32), 0.0)
    nxt = jnp.where(has_next, zn_ref[...].astype(F32), 0.0)
    xcat = jnp.concatenate([prev, zm_ref[...].astype(F32), nxt], axis=0)
    half = DN_CONV // 2
    acc = None
    for t in range(DN_CONV):
        off = DN_HALO - half + t
        term = xcat[off:off + tm, :] * cw_ref[t:t + 1, :]
        acc = term if acc is None else acc + term
    y = acc * jax.nn.sigmoid(acc)
    for h in range(DN_HEADS):
        qh = y[:, h * DN_HEAD_DIM:(h + 1) * DN_HEAD_DIM]
        kh = y[:, DN_W + h * DN_HEAD_DIM:DN_W + (h + 1) * DN_HEAD_DIM]
        qn = qh * (lax.rsqrt(jnp.sum(qh * qh, axis=-1, keepdims=True) + EPS) * (DN_HEAD_DIM ** -0.5))
        q_ref[h] = qn.astype(q_ref.dtype)
        k_ref[h] = (kh * lax.rsqrt(jnp.sum(kh * kh, axis=-1, keepdims=True) + EPS)).astype(k_ref.dtype)
        v_ref[h] = y[:, 2 * DN_W + h * DN_HEAD_DIM:2 * DN_W + (h + 1) * DN_HEAD_DIM].astype(v_ref.dtype)
    zg = zg_ref[...].astype(F32)
    lane = lax.broadcasted_iota(jnp.int32, zg.shape, 1)
    beta = jax.nn.sigmoid(zg)
    alpha = pltpu.roll(zg, 128 - DN_DH, 1)
    gl = -jnp.exp(al_ref[...]) * jax.nn.softplus(alpha + dtb_ref[...])
    g_ref[...] = jnp.where(lane < DN_DH, beta, pltpu.roll(gl, DN_DH, 1))


def dn_prep(z, conv_w, a_log, dt_bias):
    tm = 256
    n = z.shape[0]
    n_tiles = n // tm
    hb = tm // DN_HALO
    cw = jnp.zeros((8, DN_QKV), F32).at[0:DN_CONV].set(conv_w)
    pad_row = lambda t: jnp.zeros((1, 128), F32).at[0, 0:DN_DH].set(t.reshape(-1))
    hm = pl.BlockSpec((DN_HEADS, tm, DN_HEAD_DIM), lambda i: (0, i, 0))
    sds = jax.ShapeDtypeStruct((DN_HEADS, n, DN_HEAD_DIM), BF16)
    return pl.pallas_call(
        functools.partial(_dn_prep_kernel, tm=tm, n_tiles=n_tiles),
        grid=(n_tiles,),
        in_specs=[
            pl.BlockSpec((tm, DN_QKV), lambda i: (i, 0)),
            pl.BlockSpec((DN_HALO, DN_QKV), lambda i: (jnp.maximum(i * hb - 1, 0), 0)),
            pl.BlockSpec((DN_HALO, DN_QKV), lambda i: (jnp.minimum((i + 1) * hb, n // DN_HALO - 1), 0)),
            pl.BlockSpec((tm, 128), lambda i: (i, Z_GATES // 128)),
            pl.BlockSpec((8, DN_QKV), lambda i: (0, 0)),
            pl.BlockSpec((1, 128), lambda i: (0, 0)),
            pl.BlockSpec((1, 128), lambda i: (0, 0)),
        ],
        out_specs=[hm, hm, hm, pl.BlockSpec((tm, 128), lambda i: (i, 0))],
        out_shape=[sds, sds, sds, jax.ShapeDtypeStruct((n, 128), F32)],
        compiler_params=_cparams(("parallel",)),
        name="dn_prep",
    )(z, z, z, z, cw, pad_row(a_log), pad_row(dt_bias))


def _bdot(a, b):
    return jnp.dot(a.astype(BF16), b.astype(BF16), preferred_element_type=F32)


def _unit_tri_inverse_many(nmats, row, col):
    eye = (row == col).astype(F32)
    same = lambda b: (row // b) == (col // b)
    d1 = [jnp.where(same(8), m, 0.0) for m in nmats]
    d2 = [_bdot(a, a) for a in d1]
    d3 = [_bdot(a, b) for a, b in zip(d1, d2)]
    d4 = [_bdot(b, b) for b in d2]
    x = [eye + a + b + c for a, b, c in zip(d1, d2, d3)]
    t = [a + _bdot(a, b) for a, b in zip(x, d4)]
    for b in (8, 16, 32):
        sel = same(2 * b) & jnp.logical_not(same(b))
        tmp = [_bdot(jnp.where(sel, m, 0.0), a) for m, a in zip(nmats, t)]
        t = [a + _bdot(a, c) for a, c in zip(t, tmp)]
    return t


def _dn_chunk_kernel(q_ref, k_ref, v_ref, g_ref, u_ref, wq_ref, kgt_ref, aqk_ref, gl_ref):
    C = DN_C
    row = lax.broadcasted_iota(jnp.int32, (C, C), 0)
    col = lax.broadcasted_iota(jnp.int32, (C, C), 1)
    lane = lax.broadcasted_iota(jnp.int32, (C, 128), 1)
    lower = (row >= col).astype(F32)
    upper = (row <= col).astype(F32)
    inst = [(c, d, h) for c in range(DN_CPS) for d in range(2) for h in range(DN_HEADS)]
    rows = lambda c: slice(c * C, (c + 1) * C)
    g_all, gc_all, gc_t, tot = [], [], [], []
    for c in range(DN_CPS):
        g = g_ref[rows(c), :]
        gpart = jnp.where((lane >= DN_DH) & (lane < 2 * DN_DH), g, 0.0)
        csum_f = jnp.dot(lower, gpart, preferred_element_type=F32, precision=HIGHEST)
        csum_r = jnp.dot(upper, gpart, preferred_element_type=F32, precision=HIGHEST)
        gc = jnp.where(lane < DN_DH + DN_HEADS, csum_f, csum_r)
        t = jnp.sum(gpart, axis=0, keepdims=True)
        gl_ref[c] = jnp.broadcast_to(jnp.exp(t), (8, 128))
        g_all.append(g)
        gc_all.append(gc)
        gc_t.append(jnp.transpose(gc))
        tot.append(t)
    qs = {(c, h): q_ref[h, rows(c), :].astype(F32) for c in range(DN_CPS) for h in range(DN_HEADS)}
    ks = {(c, h): k_ref[h, rows(c), :].astype(F32) for c in range(DN_CPS) for h in range(DN_HEADS)}
    kbs, egcs, decays, kks = [], [], [], []
    for c, d, h in inst:
        j = d * DN_HEADS + h
        incl = (row >= col) if d == 0 else (row <= col)
        gc_col = gc_all[c][:, DN_DH + j:DN_DH + j + 1]
        gc_row = gc_t[c][DN_DH + j:DN_DH + j + 1, :]
        decays.append(jnp.where(incl, jnp.exp(jnp.where(incl, gc_col - gc_row, 0.0)), 0.0))
        egcs.append(jnp.exp(gc_col))
        kbs.append(ks[c, h] * g_all[c][:, j:j + 1])
    for (c, d, h), kb in zip(inst, kbs):
        kks.append(lax.dot_general(jnp.concatenate([kb, qs[c, h]], axis=0).astype(BF16), ks[c, h].astype(BF16),
                                   (((1,), (1,)), ((), ())), preferred_element_type=F32))
    nmats = []
    for (c, d, h), kk, decay in zip(inst, kks, decays):
        strict = (row > col) if d == 0 else (row < col)
        nmats.append(jnp.where(strict, -(kk[:C] * decay), 0.0))
    tinv = _unit_tri_inverse_many(nmats, row, col)
    sols = []
    for (c, d, h), t, kb, egc in zip(inst, tinv, kbs, egcs):
        j = d * DN_HEADS + h
        rhs = jnp.concatenate([v_ref[h, rows(c), :].astype(F32) * g_all[c][:, j:j + 1], kb * egc], axis=1)
        sols.append(_bdot(t, rhs))
    for (c, d, h), sol, kk, decay, egc in zip(inst, sols, kks, decays, egcs):
        j = d * DN_HEADS + h
        incl = (row >= col) if d == 0 else (row <= col)
        gc_col = gc_all[c][:, DN_DH + j:DN_DH + j + 1]
        tot_j = tot[c][:, DN_DH + j:DN_DH + j + 1]
        u_ref[j, rows(c), :] = sol[:, :DN_HEAD_DIM]
        wq_ref[j, c * 2 * C:(c + 1) * 2 * C, :] = jnp.concatenate([sol[:, DN_HEAD_DIM:], qs[c, h] * egc],
                                                                  axis=0).astype(BF16)
        kgt_ref[j, c * DN_HEAD_DIM:(c + 1) * DN_HEAD_DIM, :] = jnp.transpose(
            ks[c, h] * jnp.exp(tot_j - gc_col)).astype(BF16)
        aqk_ref[j, rows(c), :] = jnp.where(incl, kk[C:] * decay, 0.0).astype(BF16)


def dn_chunk(qn, kn, vn, gates):
    n = qn.shape[1]
    nc = n // DN_C
    rows = DN_CPS * DN_C
    hm = pl.BlockSpec((DN_HEADS, rows, DN_HEAD_DIM), lambda c: (0, c, 0))
    sds = jax.ShapeDtypeStruct
    return pl.pallas_call(
        _dn_chunk_kernel,
        grid=(nc // DN_CPS,),
        in_specs=[hm, hm, hm, pl.BlockSpec((rows, 128), lambda c: (c, 0))],
        out_specs=[
            pl.BlockSpec((DN_DH, rows, DN_HEAD_DIM), lambda c: (0, c, 0)),
            pl.BlockSpec((DN_DH, 2 * rows, DN_HEAD_DIM), lambda c: (0, c, 0)),
            pl.BlockSpec((DN_DH, DN_CPS * DN_HEAD_DIM, DN_C), lambda c: (0, c, 0)),
            pl.BlockSpec((DN_DH, rows, DN_C), lambda c: (0, c, 0)),
            pl.BlockSpec((DN_CPS, 8, 128), lambda c: (c, 0, 0)),
        ],
        out_shape=[
            sds((DN_DH, n, DN_HEAD_DIM), F32),
            sds((DN_DH, 2 * n, DN_HEAD_DIM), BF16),
            sds((DN_DH, nc * DN_HEAD_DIM, DN_C), BF16),
            sds((DN_DH, n, DN_C), BF16),
            sds((nc, 8, 128), F32),
        ],
        compiler_params=_cparams(("parallel",)),
        name="dn_chunk",
    )(qn, kn, vn, gates)


def _dn_scan_kernel(uf_ref, wqf_ref, kgf_ref, aqf_ref, glf_ref, ur_ref, wqr_ref, kgr_ref, aqr_ref, glr_ref,
                    of_ref, or_ref, state):
    @pl.when(pl.program_id(0) == 0)
    def _():
        state[...] = jnp.zeros_like(state)

    C = DN_C
    sets = ((uf_ref, wqf_ref, kgf_ref, aqf_ref, glf_ref, of_ref), (ur_ref, wqr_ref, kgr_ref, aqr_ref, glr_ref, or_ref))
    chains = [(d, h) + sets[d] for d in range(2) for h in range(DN_HEADS)]
    ss = [state[d * DN_HEADS + h] for d, h, *_ in chains]
    ts = [jnp.dot(wq_ref[h], s.astype(BF16), preferred_element_type=F32)
          for (d, h, u_ref, wq_ref, *_), s in zip(chains, ss)]
    vbs = [(u_ref[h] - t[:C]).astype(BF16) for (d, h, u_ref, *_), t in zip(chains, ts)]
    os_ = [t[C:] + jnp.dot(aq_ref[h], vb, preferred_element_type=F32)
           for (d, h, u_ref, wq_ref, kg_ref, aq_ref, *_), t, vb in zip(chains, ts, vbs)]
    ns = [s * gl_ref[0][0:1, DN_DH + d * DN_HEADS + h:DN_DH + d * DN_HEADS + h + 1]
          + jnp.dot(kg_ref[h], vb, preferred_element_type=F32)
          for (d, h, u_ref, wq_ref, kg_ref, aq_ref, gl_ref, o_ref), s, vb in zip(chains, ss, vbs)]
    for (d, h, *_, o_ref), o, s_new in zip(chains, os_, ns):
        o_ref[h] = o.astype(o_ref.dtype)
        state[d * DN_HEADS + h] = s_new


def dn_scan(u, wq, kgt, aqk, gl):
    n = u.shape[1]
    nc = n // DN_C
    cc = CTX_LEN // DN_C
    fwd = lambda s: s
    rev = lambda s: jnp.where(s < cc, cc - 1 - s, nc + cc - 1 - s)
    specs = []
    for d, cm in enumerate((fwd, rev)):
        specs += [
            pl.BlockSpec((DN_HEADS, DN_C, DN_HEAD_DIM), lambda s, d=d, cm=cm: (d, cm(s), 0)),
            pl.BlockSpec((DN_HEADS, 2 * DN_C, DN_HEAD_DIM), lambda s, d=d, cm=cm: (d, cm(s), 0)),
            pl.BlockSpec((DN_HEADS, DN_HEAD_DIM, DN_C), lambda s, d=d, cm=cm: (d, cm(s), 0)),
            pl.BlockSpec((DN_HEADS, DN_C, DN_C), lambda s, d=d, cm=cm: (d, cm(s), 0)),
            pl.BlockSpec((1, 8, 128), lambda s, cm=cm: (cm(s), 0, 0)),
        ]
    osd = jax.ShapeDtypeStruct((DN_HEADS, n, DN_HEAD_DIM), BF16)
    return pl.pallas_call(
        _dn_scan_kernel,
        grid=(nc,),
        in_specs=specs,
        out_specs=[pl.BlockSpec((DN_HEADS, DN_C, DN_HEAD_DIM), lambda s: (0, fwd(s), 0)),
                   pl.BlockSpec((DN_HEADS, DN_C, DN_HEAD_DIM), lambda s: (0, rev(s), 0))],
        out_shape=[osd, osd],
        scratch_shapes=[pltpu.VMEM((DN_DH, DN_HEAD_DIM, DN_HEAD_DIM), F32)],
        compiler_params=_cparams(("arbitrary",)),
        name="dn_scan",
    )(u, wq, kgt, aqk, gl, u, wq, kgt, aqk, gl)


def dn_mixer(z, conv_w, a_log, dt_bias):
    qn, kn, vn, gates = dn_prep(z, conv_w, a_log, dt_bias)
    u, wq, kgt, aqk, gl = dn_chunk(qn, kn, vn, gates)
    return dn_scan(u, wq, kgt, aqk, gl)


S5_SUB = 8
S5_BLK = S5_GROUPS * S5_STATE // S5_SUB
S5_UB = S5_W // S5_SUB
S5_T = 128


def _s5_param_kernel(lr_ref, li_ref, ls_ref, br_ref, bi_ref, ar_ref, ai_ref, bbr_ref, bbi_ref):
    lr = jnp.minimum(lr_ref[...], -1e-4)
    li = li_ref[...]
    dt = jnp.exp(ls_ref[...])
    mag = jnp.exp(lr * dt)
    ar = mag * jnp.cos(li * dt)
    ai = mag * jnp.sin(li * dt)
    den = lr * lr + li * li
    nr, ni = ar - 1.0, ai
    fr = (nr * lr + ni * li) / den
    fi = (ni * lr - nr * li) / den
    ar_ref[...] = ar
    ai_ref[...] = ai
    for h in range(S5_GROUP_CH):
        bbr_ref[h] = fr * br_ref[h] - fi * bi_ref[h]
        bbi_ref[h] = fr * bi_ref[h] + fi * br_ref[h]


def s5_params(lam_re, lam_im, log_step, b_re, b_im):
    r = 2 * S5_GROUPS
    ls = jnp.broadcast_to(log_step.reshape(r, 1), (r, S5_STATE))
    bt = lambda t: jnp.transpose(t.reshape(r, S5_STATE, S5_GROUP_CH), (2, 0, 1))
    sds = jax.ShapeDtypeStruct
    return pl.pallas_call(
        _s5_param_kernel,
        out_shape=[sds((r, S5_STATE), F32), sds((r, S5_STATE), F32),
                   sds((S5_GROUP_CH, r, S5_STATE), F32), sds((S5_GROUP_CH, r, S5_STATE), F32)],
        name="s5_params",
    )(lam_re.reshape(r, S5_STATE), lam_im.reshape(r, S5_STATE), ls, bt(b_re), bt(b_im))


def _s5_scan_kernel(uf_ref, ur_ref, a_ref, rb_ref, cc_ref, yf_ref, yr_ref, bu, hh, st):
    rows = S5_T * S5_SUB

    @pl.when(pl.program_id(0) == 0)
    def _():
        st[...] = jnp.zeros_like(st)

    sub = lax.broadcasted_iota(jnp.int32, (rows, S5_W), 0) % S5_SUB
    blk = lax.broadcasted_iota(jnp.int32, (rows, S5_W), 1) // S5_UB
    own = sub == blk
    for d, u_ref in enumerate((uf_ref, ur_ref)):
        ue = jnp.broadcast_to(u_ref[...].astype(F32)[:, None, :], (S5_T, S5_SUB, S5_W)).reshape(rows, S5_W)
        lhs = jnp.where(own, ue, 0.0).astype(BF16)
        bu[d, 0] = jnp.dot(lhs, rb_ref[d, 0], preferred_element_type=F32)
        bu[d, 1] = jnp.dot(lhs, rb_ref[d, 1], preferred_element_type=F32)

    a = [[a_ref[d, c] for c in range(2)] for d in range(2)]

    def body(t, carry):
        fr, fi, rr, ri = carry
        rf = pl.multiple_of(t * S5_SUB, S5_SUB)
        rv = pl.multiple_of((S5_T - 1 - t) * S5_SUB, S5_SUB)
        nfr = a[0][0] * fr - a[0][1] * fi + bu[0, 0, pl.ds(rf, S5_SUB), :]
        nfi = a[0][0] * fi + a[0][1] * fr + bu[0, 1, pl.ds(rf, S5_SUB), :]
        nrr = a[1][0] * rr - a[1][1] * ri + bu[1, 0, pl.ds(rv, S5_SUB), :]
        nri = a[1][0] * ri + a[1][1] * rr + bu[1, 1, pl.ds(rv, S5_SUB), :]
        hh[0, 0, pl.ds(rf, S5_SUB), :] = nfr
        hh[0, 1, pl.ds(rf, S5_SUB), :] = nfi
        hh[1, 0, pl.ds(rv, S5_SUB), :] = nrr
        hh[1, 1, pl.ds(rv, S5_SUB), :] = nri
        return nfr, nfi, nrr, nri

    fin = lax.fori_loop(0, S5_T, body, (st[0, 0], st[0, 1], st[1, 0], st[1, 1]), unroll=8)
    st[0, 0], st[0, 1], st[1, 0], st[1, 1] = fin

    for d, y_ref in enumerate((yf_ref, yr_ref)):
        ye = (jnp.dot(hh[d, 0].astype(BF16), cc_ref[d, 0], preferred_element_type=F32)
              + jnp.dot(hh[d, 1].astype(BF16), cc_ref[d, 1], preferred_element_type=F32))
        ye = jnp.where(own, ye, 0.0)
        y_ref[...] = jnp.sum(ye.reshape(S5_T, S5_SUB, S5_W), axis=1)


def s5_scan(z, a8, rb, cc):
    rows = S5_T * S5_SUB
    n = z.shape[0]
    nt = n // S5_T
    ct = CTX_LEN // S5_T
    ucol = Z_S5 // S5_W

    def rev_tile(i):
        return jnp.where(i < ct, ct - 1 - i, nt + ct - 1 - i)

    full = lambda shape: pl.BlockSpec(shape, lambda i: (0,) * len(shape))
    sds = jax.ShapeDtypeStruct((n, S5_W), F32)
    return pl.pallas_call(
        _s5_scan_kernel,
        grid=(nt,),
        in_specs=[
            pl.BlockSpec((S5_T, S5_W), lambda i: (i, ucol)),
            pl.BlockSpec((S5_T, S5_W), lambda i: (rev_tile(i), ucol)),
            full((2, 2, S5_SUB, S5_BLK)),
            full((2, 2, S5_W, S5_BLK)),
            full((2, 2, S5_BLK, S5_W)),
        ],
        out_specs=[pl.BlockSpec((S5_T, S5_W), lambda i: (i, 0)),
                   pl.BlockSpec((S5_T, S5_W), lambda i: (rev_tile(i), 0))],
        out_shape=[sds, sds],
        scratch_shapes=[pltpu.VMEM((2, 2, rows, S5_BLK), F32), pltpu.VMEM((2, 2, rows, S5_BLK), F32),
                        pltpu.VMEM((2, 2, S5_SUB, S5_BLK), F32)],
        compiler_params=_cparams(("arbitrary",)),
        name="s5_scan",
    )(z, z, a8, rb, cc)


def s5_mixer(z, lam_re, lam_im, log_step, b_re, b_im, c_re, c_im):
    ar, ai, bbr, bbi = s5_params(lam_re, lam_im, log_step, b_re, b_im)
    g_blk = jax.nn.one_hot(jnp.arange(S5_GROUPS) % (S5_GROUPS // S5_SUB), S5_GROUPS // S5_SUB, dtype=F32)

    def place_b(bb):
        bb = jnp.transpose(bb.reshape(S5_GROUP_CH, 2, S5_GROUPS, S5_STATE), (1, 2, 0, 3))
        return jnp.einsum('dghp,gj->dghjp', bb, g_blk).reshape(2, S5_W, S5_BLK)

    def place_c(cm):
        return jnp.einsum('dghp,gj->djpgh', cm, g_blk).reshape(2, S5_BLK, S5_W)

    rb = jnp.stack([place_b(bbr), place_b(bbi)], axis=1).astype(BF16)
    cc = jnp.stack([place_c(c_re), -place_c(c_im)], axis=1).astype(BF16)
    a8 = jnp.stack([ar.reshape(2, S5_SUB, S5_BLK), ai.reshape(2, S5_SUB, S5_BLK)], axis=1)
    return s5_scan(z, a8, rb, cc)


def _permute_w_in(w):
    n_gate = 4 * DN_HEADS
    attn = w[:, :QKV_W]
    dn = w[:, QKV_W:QKV_W + 4 * DN_W]
    gates = w[:, QKV_W + 4 * DN_W:QKV_W + 4 * DN_W + n_gate]
    s5 = w[:, QKV_W + 4 * DN_W + n_gate:]
    pad = jnp.zeros((w.shape[0], Z_S5 - Z_GATES - n_gate), w.dtype)
    return jnp.concatenate([dn, attn, gates, pad, s5], axis=1).astype(BF16)


def kernel(x, c, ctx, c_ctx, w_ada, b_ada, norm1_w, norm2_w, w_in, w_out, attn_q_norm, attn_k_norm, attn_sink,
           dn_conv, dn_a_log, dn_dt_bias, dn_o_norm, s5_lam_re, s5_lam_im, s5_log_step, s5_b_re, s5_b_im,
           s5_c_re, s5_c_im, s5_d, s5_w_glu, moe_w_grp, moe_b_grp, moe_w_rt, moe_b_rt, moe_w1, moe_w3, moe_w2):
    b, n, d = x.shape
    assert b == 1 and ctx.shape[1] == CTX_LEN and d == D_MODEL
    lc = CTX_LEN
    xa = jnp.concatenate([ctx[0], x[0]], axis=0)
    mods = adaln(c_ctx, c[0], w_ada, b_ada)
    cos_t, sin_t = rope_tables(n)
    w1_all = moe_w1.reshape(DEPTH * N_EXPERTS, D_MODEL, EXPERT_HIDDEN)
    w3_all = moe_w3.reshape(DEPTH * N_EXPERTS, D_MODEL, EXPERT_HIDDEN)
    w2_all = moe_w2.reshape(DEPTH * N_EXPERTS, EXPERT_HIDDEN, D_MODEL)
    for layer in range(DEPTH):
        mod = mods[layer, 0:2].reshape(2, 6, D_MODEL)
        mod = jnp.concatenate([mod, jnp.zeros((2, 2, D_MODEL), F32)], axis=1)
        z = in_proj(xa, norm1_w[layer], mod, _permute_w_in(w_in[layer]))
        mix_a = attn_mixer(z, attn_q_norm[layer], attn_k_norm[layer], attn_sink[layer], cos_t, sin_t)
        o_f, o_r = dn_mixer(z, dn_conv[layer], dn_a_log[layer], dn_dt_bias[layer])
        yf, yr = s5_mixer(z, s5_lam_re[layer], s5_lam_im[layer], s5_log_step[layer], s5_b_re[layer],
                          s5_b_im[layer], s5_c_re[layer], s5_c_im[layer])
        w_router = jnp.concatenate([moe_w_grp[layer], moe_w_rt[layer],
                                    jnp.zeros((D_MODEL, 128 - N_GROUPS - N_EXPERTS), F32)], axis=1)
        b_router = jnp.concatenate([moe_b_grp[layer], moe_b_rt[layer],
                                    jnp.zeros((128 - N_GROUPS - N_EXPERTS,), F32)]).reshape(1, 128)
        xa, h2, route = out_proj(mix_a, o_f, o_r, z, yf, yr, xa, mod, w_out[layer].astype(BF16), norm2_w[layer],
                                 w_router, b_router, dn_o_norm[layer], s5_d[layer], s5_w_glu[layer].astype(BF16))
        xa = moe_block(xa, h2, route, mod, w1_all, w3_all, w2_all, layer, skip_ctx=(layer == DEPTH - 1))
    return xa[None]
```

```python
import functools

import jax
import jax.numpy as jnp
import numpy as np
from jax import lax
from jax.experimental import pallas as pl
from jax.experimental.pallas import tpu as pltpu

F32 = jnp.float32
BF16 = jnp.bfloat16
HIGHEST = lax.Precision.HIGHEST

D_MODEL = 2048
SEQ = 8192
DEPTH = 2
GRID_W = 64
CTX_LEN = 256
N_ALL = CTX_LEN + SEQ
EPS = 1e-6
NEG_INF = -1e30

ATTN_HEADS = 12
ATTN_KV_HEADS = 4
ATTN_HEAD_DIM = 64
ATTN_GROUP = ATTN_HEADS // ATTN_KV_HEADS
ATTN_W = ATTN_HEADS * ATTN_HEAD_DIM
ATTN_KV_W = ATTN_KV_HEADS * ATTN_HEAD_DIM
WINDOW = 128
ATTN_BLOCK = 128
ROPE_BASE = 10000.0
DN_HEADS = 6
DN_HEAD_DIM = 128
DN_W = DN_HEADS * DN_HEAD_DIM
DN_CONV = 5
DN_CHUNK = 64
S5_W = D_MODEL - ATTN_W - DN_W
S5_GROUP_CH = 16
S5_GROUPS = S5_W // S5_GROUP_CH
S5_STATE = 64
N_GROUPS = 4
EXPERTS_PER_GROUP = 8
N_EXPERTS = N_GROUPS * EXPERTS_PER_GROUP
TOP_K = 2
EXPERT_HIDDEN = 512

Z_DNQ, Z_DNK, Z_DNV, Z_DNG = 0, 768, 1536, 2304
Z_Q, Z_K, Z_V = 3072, 3840, 4096
Z_GATES = 4352
Z_BETA, Z_ALPHA = Z_GATES, Z_GATES + 2 * DN_HEADS
Z_S5 = 4608
Z_W = 5120

VMEM_LIMIT = 56 * 1024 * 1024


def _cparams(sem, vmem=VMEM_LIMIT):
    return pltpu.CompilerParams(dimension_semantics=sem, vmem_limit_bytes=vmem)


ADALN_TN = 1024


def _adaln_kernel(c_ref, w_ref, b_ref, o_ref):
    reps = ADALN_TN // 128

    def body(kb, acc):
        r0 = pl.multiple_of(kb * 8, 8)
        w = w_ref[0, pl.ds(r0, 8), :]
        out = []
        for r in range(2):
            cv = c_ref[r, pl.ds(r0, 8), :]
            s = cv * jax.nn.sigmoid(cv)
            out.append(acc[r] + w * jnp.concatenate([s] * reps, axis=1))
        return tuple(out)

    zero = jnp.zeros((8, ADALN_TN), F32)
    acc = lax.fori_loop(0, D_MODEL // 8, body, (zero, zero), unroll=4)
    rows = [jnp.sum(a, axis=0, keepdims=True) + b_ref[0] for a in acc]
    o_ref[0] = jnp.concatenate(rows + [jnp.zeros((6, ADALN_TN), F32)], axis=0)


def adaln(c_ctx, c, w_ada, b_ada):
    L = w_ada.shape[0]
    n6 = 6 * D_MODEL
    cb = jnp.broadcast_to(jnp.stack([c_ctx, c])[:, :, None], (2, D_MODEL, 128))
    return pl.pallas_call(
        _adaln_kernel,
        grid=(L, n6 // ADALN_TN),
        in_specs=[
            pl.BlockSpec((2, D_MODEL, 128), lambda l, j: (0, 0, 0)),
            pl.BlockSpec((1, D_MODEL, ADALN_TN), lambda l, j: (l, 0, j)),
            pl.BlockSpec((1, 1, ADALN_TN), lambda l, j: (l, 0, j)),
        ],
        out_specs=pl.BlockSpec((1, 8, ADALN_TN), lambda l, j: (l, 0, j)),
        out_shape=jax.ShapeDtypeStruct((L, 8, n6), F32),
        compiler_params=_cparams(("parallel", "parallel")),
        name="adaln",
    )(cb, w_ada, b_ada.reshape(L, 1, n6))


NORM_ROWS = 64


def _is_ctx_rows(base, rows):
    return base + lax.broadcasted_iota(jnp.int32, (rows, 1), 0) < CTX_LEN


def _row_is_ctx(tm):
    return _is_ctx_rows(pl.program_id(0) * tm, tm)


def _norm_mod(x, nw, mod_ref, shift_i, scale_i, is_ctx):
    ms = jnp.mean(x * x, axis=-1, keepdims=True)
    h = x * lax.rsqrt(ms + EPS) * nw
    sc = jnp.where(is_ctx, mod_ref[0, scale_i:scale_i + 1, :], mod_ref[1, scale_i:scale_i + 1, :])
    sh = jnp.where(is_ctx, mod_ref[0, shift_i:shift_i + 1, :], mod_ref[1, shift_i:shift_i + 1, :])
    return h * (1.0 + sc) + sh


def _in_proj_kernel(x_ref, nw_ref, mod_ref, w_ref, o_ref, h_scr, *, tm):
    @pl.when(pl.program_id(1) == 0)
    def _():
        def chunk(c, carry):
            r0 = pl.multiple_of(c * NORM_ROWS, NORM_ROWS)
            h = _norm_mod(x_ref[pl.ds(r0, NORM_ROWS), :], nw_ref[...], mod_ref, 0, 1,
                          _is_ctx_rows(pl.program_id(0) * tm + r0, NORM_ROWS))
            h_scr[pl.ds(r0, NORM_ROWS), :] = h.astype(BF16)
            return carry
        lax.fori_loop(0, tm // NORM_ROWS, chunk, 0)

    o_ref[...] = jnp.dot(h_scr[...], w_ref[...], preferred_element_type=F32).astype(o_ref.dtype)


def in_proj(xa, norm_w, mod, w_in_p):
    tm, tn = 1024, 1024
    n = xa.shape[0]
    return pl.pallas_call(
        functools.partial(_in_proj_kernel, tm=tm),
        grid=(pl.cdiv(n, tm), Z_W // tn),
        in_specs=[
            pl.BlockSpec((tm, D_MODEL), lambda i, j: (i, 0)),
            pl.BlockSpec((1, D_MODEL), lambda i, j: (0, 0)),
            pl.BlockSpec((2, 8, D_MODEL), lambda i, j: (0, 0, 0)),
            pl.BlockSpec((D_MODEL, tn), lambda i, j: (0, j)),
        ],
        out_specs=pl.BlockSpec((tm, tn), lambda i, j: (i, j)),
        out_shape=jax.ShapeDtypeStruct((n, Z_W), BF16),
        scratch_shapes=[pltpu.VMEM((tm, D_MODEL), BF16)],
        compiler_params=_cparams(("parallel", "arbitrary")),
        name="in_proj",
    )(xa, norm_w.reshape(1, D_MODEL), mod, w_in_p)


def _out_proj_kernel(a_ref, of_ref, or_ref, zg_ref, onw_ref, zu_ref, yf_ref, yr_ref, dsk_ref, wg_ref,
                     x_ref, mod_ref, w_ref, nw_ref, wr_ref, br_ref, xo_ref, h_ref, rt_ref, *, tm):
    is_ctx = _row_is_ctx(tm)
    b_parts = []
    for hd in range(DN_HEADS):
        o = of_ref[hd].astype(F32) + or_ref[hd].astype(F32)
        o = o * lax.rsqrt(jnp.mean(o * o, axis=-1, keepdims=True) + EPS) * onw_ref[...]
        g = zg_ref[:, hd * DN_HEAD_DIM:(hd + 1) * DN_HEAD_DIM].astype(F32)
        b_parts.append((o * (g * jax.nn.sigmoid(g))).astype(BF16))
    b = jnp.concatenate(b_parts, axis=1)
    ys = jax.nn.gelu(zu_ref[...].astype(F32) * dsk_ref[...] + yf_ref[...] + yr_ref[...])
    s = (ys * jax.nn.sigmoid(jnp.dot(ys.astype(BF16), wg_ref[...], preferred_element_type=F32))).astype(BF16)
    y = (jnp.dot(a_ref[...], w_ref[0:ATTN_W, :], preferred_element_type=F32)
         + jnp.dot(b, w_ref[ATTN_W:ATTN_W + DN_W, :], preferred_element_type=F32)
         + jnp.dot(s, w_ref[ATTN_W + DN_W:, :], preferred_element_type=F32))
    gate = jnp.where(is_ctx, mod_ref[0, 2:3, :], mod_ref[1, 2:3, :])
    xn = x_ref[...] + gate * y
    xo_ref[...] = xn
    h = _norm_mod(xn, nw_ref[...], mod_ref, 3, 4, is_ctx)
    h_ref[...] = h
    wr = wr_ref[...]
    h_hi = h.astype(BF16)
    h_lo = (h - h_hi.astype(F32)).astype(BF16)
    w_hi = wr.astype(BF16)
    w_lo = (wr - w_hi.astype(F32)).astype(BF16)
    lg = (jnp.dot(h_hi, w_hi, preferred_element_type=F32) + jnp.dot(h_hi, w_lo, preferred_element_type=F32)
          + jnp.dot(h_lo, w_hi, preferred_element_type=F32) + br_ref[...])
    rt_ref[...] = _route(lg)


def out_proj(mix_a, o_f, o_r, z, yf, yr, xa, mod, w_out_b, norm2_w, w_router, b_router, o_norm_w, d_skip, w_glu_b):
    tm = 256
    n = xa.shape[0]
    row = lambda i: (i, 0)
    const2 = lambda i: (0, 0)
    hm = pl.BlockSpec((DN_HEADS, tm, DN_HEAD_DIM), lambda i: (0, i, 0))
    return pl.pallas_call(
        functools.partial(_out_proj_kernel, tm=tm),
        grid=(n // tm,),
        in_specs=[
            pl.BlockSpec((tm, ATTN_W), row),
            hm, hm,
            pl.BlockSpec((tm, DN_W), lambda i: (i, Z_DNG // DN_W)),
            pl.BlockSpec((1, DN_HEAD_DIM), const2),
            pl.BlockSpec((tm, S5_W), lambda i: (i, Z_S5 // S5_W)),
            pl.BlockSpec((tm, S5_W), row),
            pl.BlockSpec((tm, S5_W), row),
            pl.BlockSpec((1, S5_W), const2),
            pl.BlockSpec((S5_W, S5_W), const2),
            pl.BlockSpec((tm, D_MODEL), row),
            pl.BlockSpec((2, 8, D_MODEL), lambda i: (0, 0, 0)),
            pl.BlockSpec((D_MODEL, D_MODEL), const2),
            pl.BlockSpec((1, D_MODEL), const2),
            pl.BlockSpec((D_MODEL, 128), const2),
            pl.BlockSpec((1, 128), const2),
        ],
        out_specs=[
            pl.BlockSpec((tm, D_MODEL), row),
            pl.BlockSpec((tm, D_MODEL), row),
            pl.BlockSpec((tm, 128), row),
        ],
        out_shape=[
            jax.ShapeDtypeStruct((n, D_MODEL), F32),
            jax.ShapeDtypeStruct((n, D_MODEL), F32),
            jax.ShapeDtypeStruct((n, 128), F32),
        ],
        compiler_params=_cparams(("parallel",)),
        name="out_proj",
    )(mix_a, o_f, o_r, z, o_norm_w.reshape(1, DN_HEAD_DIM), z, yf, yr, d_skip.reshape(1, S5_W), w_glu_b,
      xa, mod, w_out_b, norm2_w.reshape(1, D_MODEL), w_router, b_router)


def _route(lg):
    tm = lg.shape[0]
    lane = lax.broadcasted_iota(jnp.int32, (tm, 128), 1)
    is_g = lane < N_GROUPS
    gl = jnp.where(is_g, lg, NEG_INF)
    gmax = jnp.max(gl, axis=-1, keepdims=True)
    gidx = jnp.min(jnp.where((gl == gmax) & is_g, lane, 128), axis=-1, keepdims=True)
    gsum = jnp.sum(jnp.where(is_g, jnp.exp(gl - gmax), 0.0), axis=-1, keepdims=True)
    g_w = 1.0 / gsum
    e_lane = lane - N_GROUPS
    in_grp = (e_lane >= gidx * EXPERTS_PER_GROUP) & (e_lane < (gidx + 1) * EXPERTS_PER_GROUP)
    el = jnp.where(in_grp, lg, NEG_INF)
    v1 = jnp.max(el, axis=-1, keepdims=True)
    i1 = jnp.min(jnp.where((el == v1) & in_grp, e_lane, 128), axis=-1, keepdims=True)
    el2 = jnp.where(e_lane == i1, NEG_INF, el)
    in2 = in_grp & (e_lane != i1)
    v2 = jnp.max(el2, axis=-1, keepdims=True)
    i2 = jnp.min(jnp.where((el2 == v2) & in2, e_lane, 128), axis=-1, keepdims=True)
    e2 = jnp.exp(v2 - v1)
    w1 = g_w / (1.0 + e2)
    w2 = g_w * e2 / (1.0 + e2)
    out = jnp.where(lane == 0, i1.astype(F32), 0.0)
    out = jnp.where(lane == 1, i2.astype(F32), out)
    out = jnp.where(lane == 2, w1, out)
    out = jnp.where(lane == 3, w2, out)
    return out


MOE_TILE = 256


def _moe_tiles(n):
    return TOP_K * n // MOE_TILE + N_EXPERTS


def _expert_kernel(te_ref, ts_ref, tok_ref, nt_ref, h_hbm, w1_ref, w3_ref, w2_ref, y_ref,
                   xbuf0, xbuf1, sem, w1b, w3b, w2b):
    i = pl.program_id(0)
    n_tiles = nt_ref[0]
    last = tok_ref.shape[0] - 1
    bufs = (xbuf0, xbuf1)

    def gather_start(tile, slot):
        base = ts_ref[tile]
        for r in range(MOE_TILE):
            tok = tok_ref[jnp.minimum(base + r, last)]
            pltpu.make_async_copy(h_hbm.at[pl.ds(tok, 1)], bufs[slot].at[pl.ds(r, 1)], sem.at[slot]).start()

    def gather_wait(slot):
        pltpu.make_async_copy(bufs[slot], bufs[slot], sem.at[slot]).wait()

    @pl.when(i == 0)
    def _():
        gather_start(0, 0)

    prev_e = te_ref[jnp.maximum(i - 1, 0)]
    new_e = (i == 0) | (te_ref[i] != prev_e)

    @pl.when((i < n_tiles) & new_e)
    def _():
        w1b[...] = w1_ref[0].astype(BF16)
        w3b[...] = w3_ref[0].astype(BF16)
        w2b[...] = w2_ref[0].astype(BF16)

    for slot in range(2):
        @pl.when((i < n_tiles) & (i % 2 == slot))
        def _():
            gather_wait(slot)
            gather_start(jnp.minimum(i + 1, n_tiles - 1), 1 - slot)
            xt = bufs[slot][...].astype(BF16)
            a = jnp.dot(xt, w1b[...], preferred_element_type=F32)
            u = jnp.dot(xt, w3b[...], preferred_element_type=F32)
            act = (a * jax.nn.sigmoid(a)) * u
            y_ref[...] = jnp.dot(act.astype(BF16), w2b[...], preferred_element_type=F32)

            @pl.when(i == n_tiles - 1)
            def _():
                gather_wait(1 - slot)

    @pl.when(i >= n_tiles)
    def _():
        y_ref[...] = jnp.zeros_like(y_ref)


def expert_mlp(tile_expert, tile_start, sorted_tok, n_tiles, h2, w1, w3, w2):
    e_map = lambda i, te, ts, tok, nt: (te[i], 0, 0)
    moe_tiles = tile_expert.shape[0]
    grid_spec = pltpu.PrefetchScalarGridSpec(
        num_scalar_prefetch=4,
        grid=(moe_tiles,),
        in_specs=[
            pl.BlockSpec(memory_space=pl.ANY),
            pl.BlockSpec((1, D_MODEL, EXPERT_HIDDEN), e_map),
            pl.BlockSpec((1, D_MODEL, EXPERT_HIDDEN), e_map),
            pl.BlockSpec((1, EXPERT_HIDDEN, D_MODEL), e_map),
        ],
        out_specs=pl.BlockSpec((MOE_TILE, D_MODEL), lambda i, te, ts, tok, nt: (i, 0)),
        scratch_shapes=[
            pltpu.VMEM((MOE_TILE, D_MODEL), F32),
            pltpu.VMEM((MOE_TILE, D_MODEL), F32),
            pltpu.SemaphoreType.DMA((2,)),
            pltpu.VMEM((D_MODEL, EXPERT_HIDDEN), BF16),
            pltpu.VMEM((D_MODEL, EXPERT_HIDDEN), BF16),
            pltpu.VMEM((EXPERT_HIDDEN, D_MODEL), BF16),
        ],
    )
    return pl.pallas_call(
        _expert_kernel,
        grid_spec=grid_spec,
        out_shape=jax.ShapeDtypeStruct((moe_tiles * MOE_TILE, D_MODEL), F32),
        compiler_params=_cparams(("arbitrary",)),
        name="expert_mlp",
    )(tile_expert, tile_start, sorted_tok, n_tiles, h2, w1, w3, w2)


CMB_TILE = 256


def _combine_kernel(pos_ref, y_hbm, x_ref, route_ref, mod_ref, o_ref, ybuf0, ybuf1, sem, *, first_tile):
    i = pl.program_id(0)
    nt = pl.num_programs(0)
    bufs = (ybuf0, ybuf1)

    def gather_start(tile, slot):
        base = (tile + first_tile) * (CMB_TILE * TOP_K)
        for r in range(CMB_TILE):
            for k in range(TOP_K):
                p = pos_ref[base + r * TOP_K + k]
                pltpu.make_async_copy(y_hbm.at[pl.ds(p, 1)], bufs[slot].at[k, pl.ds(r, 1)], sem.at[slot]).start()

    def gather_wait(slot):
        pltpu.make_async_copy(bufs[slot], bufs[slot], sem.at[slot]).wait()

    @pl.when(i == 0)
    def _():
        gather_start(0, 0)

    for slot in range(2):
        @pl.when(i % 2 == slot)
        def _():
            gather_wait(slot)
            gather_start(jnp.minimum(i + 1, nt - 1), 1 - slot)
            is_ctx = _is_ctx_rows((i + first_tile) * CMB_TILE, CMB_TILE)
            gate = jnp.where(is_ctx, mod_ref[0, 5:6, :], mod_ref[1, 5:6, :])
            w0 = route_ref[:, TOP_K:TOP_K + 1]
            w1 = route_ref[:, TOP_K + 1:TOP_K + 2]
            o_ref[...] = x_ref[...] + gate * (w0 * bufs[slot][0] + w1 * bufs[slot][1])

            @pl.when(i == nt - 1)
            def _():
                gather_wait(1 - slot)


def moe_combine(pos, y_sorted, xa, route, mod, skip_ctx):
    first_tile = CTX_LEN // CMB_TILE if skip_ctx else 0
    n = xa.shape[0] - first_tile * CMB_TILE
    grid_spec = pltpu.PrefetchScalarGridSpec(
        num_scalar_prefetch=1,
        grid=(n // CMB_TILE,),
        in_specs=[
            pl.BlockSpec(memory_space=pl.ANY),
            pl.BlockSpec((CMB_TILE, D_MODEL), lambda i, pos: (i + first_tile, 0)),
            pl.BlockSpec((CMB_TILE, 128), lambda i, pos: (i + first_tile, 0)),
            pl.BlockSpec((2, 8, D_MODEL), lambda i, pos: (0, 0, 0)),
        ],
        out_specs=pl.BlockSpec((CMB_TILE, D_MODEL), lambda i, pos: (i, 0)),
        scratch_shapes=[
            pltpu.VMEM((TOP_K, CMB_TILE, D_MODEL), F32),
            pltpu.VMEM((TOP_K, CMB_TILE, D_MODEL), F32),
            pltpu.SemaphoreType.DMA((2,)),
        ],
    )
    return pl.pallas_call(
        functools.partial(_combine_kernel, first_tile=first_tile),
        grid_spec=grid_spec,
        out_shape=jax.ShapeDtypeStruct((n, D_MODEL), F32),
        compiler_params=_cparams(("arbitrary",)),
        name="moe_combine",
    )(pos, y_sorted, xa, route, mod)


def moe_dispatch_plan(route):
    n = route.shape[0]
    eid = route[:, 0:TOP_K].astype(jnp.int32).reshape(-1)
    p_total = eid.shape[0]
    experts = jnp.arange(N_EXPERTS, dtype=jnp.int32)
    counts = jnp.sum((eid[:, None] == experts[None, :]).astype(jnp.int32), axis=0)
    tiles_per = (counts + MOE_TILE - 1) // MOE_TILE
    tile_off = jnp.cumsum(tiles_per) - tiles_per
    off = jnp.cumsum(counts) - counts
    pair_ids = jnp.arange(p_total, dtype=jnp.int32)
    e_sorted, sorted_pair = lax.sort((eid, pair_ids), num_keys=1, is_stable=True)
    sorted_tok = sorted_pair // TOP_K
    s_onehot = (e_sorted[:, None] == experts[None, :]).astype(jnp.int32)
    pos_sorted = pair_ids + jnp.sum(s_onehot * (tile_off * MOE_TILE - off)[None, :], axis=1)
    _, pos = lax.sort((sorted_pair, pos_sorted), num_keys=1)
    moe_tiles = _moe_tiles(n)
    n_tiles = jnp.sum(tiles_per).astype(jnp.int32)
    tile_ids = jnp.arange(moe_tiles, dtype=jnp.int32)
    tile_expert = jnp.sum((tile_ids[:, None] >= (tile_off + tiles_per)[None, :]).astype(jnp.int32), axis=1)
    tile_expert = jnp.minimum(tile_expert, N_EXPERTS - 1)
    t_onehot = (tile_expert[:, None] == experts[None, :]).astype(jnp.int32)
    tile_start = (jnp.sum(t_onehot * off[None, :], axis=1)
                  + (tile_ids - jnp.sum(t_onehot * tile_off[None, :], axis=1)) * MOE_TILE)
    return tile_expert, tile_start, sorted_tok, n_tiles.reshape(1), pos


def moe_block(xa, h2, route, mod, w1, w3, w2, layer, skip_ctx):
    tile_expert, tile_start, sorted_tok, n_tiles, pos = moe_dispatch_plan(route)
    y_sorted = expert_mlp(tile_expert + layer * N_EXPERTS, tile_start, sorted_tok, n_tiles, h2, w1, w3, w2)
    return moe_combine(pos, y_sorted, xa, route, mod, skip_ctx)


QK_W = ATTN_W + ATTN_KV_W
QKV_W = QK_W + ATTN_KV_W
ROPE_F = ATTN_HEAD_DIM // 4


def _norm_rope_heads(x, nw, cos, sin, out_ref, n_heads, scale):
    tm, width = x.shape
    xw = x * nw
    lane = lax.broadcasted_iota(jnp.int32, (tm, width), 1)
    odd = (lane // ROPE_F) % 2 == 1
    xs = jnp.where(odd, pltpu.roll(xw, ROPE_F, 1), pltpu.roll(xw, width - ROPE_F, 1))
    heads = [slice(h * ATTN_HEAD_DIM, (h + 1) * ATTN_HEAD_DIM) for h in range(n_heads)]
    invs = [lax.rsqrt(jnp.mean(x[:, sl] * x[:, sl], axis=-1, keepdims=True) + EPS) * scale for sl in heads]
    for h, (sl, inv) in enumerate(zip(heads, invs)):
        out_ref[h] = ((xw[:, sl] * cos + xs[:, sl] * sin) * inv).astype(out_ref.dtype)


def _attn_prep_kernel(zq_ref, zk_ref, zv_ref, qw_ref, kw_ref, cos_ref, sin_ref, q_ref, k_ref, v_ref):
    cos = cos_ref[...]
    sin = sin_ref[...]
    _norm_rope_heads(zq_ref[...].astype(F32), qw_ref[...], cos, sin, q_ref, ATTN_HEADS, ATTN_HEAD_DIM ** -0.5)
    _norm_rope_heads(zk_ref[...].astype(F32), kw_ref[...], cos, sin, k_ref, ATTN_KV_HEADS, 1.0)
    ones = jnp.ones((zv_ref.shape[0], ATTN_HEAD_DIM), BF16)
    for h in range(ATTN_KV_HEADS):
        v_ref[h] = jnp.concatenate([zv_ref[:, h * ATTN_HEAD_DIM:(h + 1) * ATTN_HEAD_DIM].astype(BF16), ones], axis=1)


def attn_prep(z, q_norm_w, k_norm_w, cos_t, sin_t):
    tm = 256
    n = z.shape[0]
    qw = jnp.tile(q_norm_w, ATTN_HEADS).reshape(1, ATTN_W)
    kw = jnp.tile(k_norm_w, ATTN_KV_HEADS).reshape(1, ATTN_KV_W)
    hm = lambda h, w=ATTN_HEAD_DIM: pl.BlockSpec((h, tm, w), lambda i: (0, i, 0))
    sds = lambda h, w=ATTN_HEAD_DIM: jax.ShapeDtypeStruct((h, n, w), BF16)
    return pl.pallas_call(
        _attn_prep_kernel,
        grid=(n // tm,),
        in_specs=[
            pl.BlockSpec((tm, ATTN_W), lambda i: (i, Z_Q // ATTN_W)),
            pl.BlockSpec((tm, ATTN_KV_W), lambda i: (i, Z_K // ATTN_KV_W)),
            pl.BlockSpec((tm, ATTN_KV_W), lambda i: (i, Z_V // ATTN_KV_W)),
            pl.BlockSpec((1, ATTN_W), lambda i: (0, 0)),
            pl.BlockSpec((1, ATTN_KV_W), lambda i: (0, 0)),
            pl.BlockSpec((tm, ATTN_HEAD_DIM), lambda i: (i, 0)),
            pl.BlockSpec((tm, ATTN_HEAD_DIM), lambda i: (i, 0)),
        ],
        out_specs=[hm(ATTN_HEADS), hm(ATTN_KV_HEADS), hm(ATTN_KV_HEADS, 2 * ATTN_HEAD_DIM)],
        out_shape=[sds(ATTN_HEADS), sds(ATTN_KV_HEADS), sds(ATTN_KV_HEADS, 2 * ATTN_HEAD_DIM)],
        compiler_params=_cparams(("parallel",)),
        name="attn_prep",
    )(z, z, z, qw, kw, cos_t, sin_t)


def _attn_kernel(sink_ref, q_ref, kp_ref, ko_ref, kn_ref, kc_ref, vp_ref, vo_ref, vn_ref, vc_ref, o_ref, *, n_blocks):
    i = pl.program_id(0)
    ctx_blocks = CTX_LEN // ATTN_BLOCK
    B = ATTN_BLOCK
    rows = ATTN_GROUP * B
    ncol = 3 * B + CTX_LEN
    r = lax.broadcasted_iota(jnp.int32, (rows, ncol), 0) % B
    c = lax.broadcasted_iota(jnp.int32, (rows, ncol), 1)
    lo = jnp.where(i > ctx_blocks, 0, B)
    hi = jnp.where(i < n_blocks - 1, 3 * B, 2 * B)
    hi = jnp.where(i < ctx_blocks, 0, hi)
    band = (c >= r) & (c <= r + 2 * WINDOW) & (c >= lo) & (c < hi)
    mask = band | (c >= 3 * B)
    grp = lax.broadcasted_iota(jnp.int32, (rows, 1), 0) // B
    for kh in range(ATTN_KV_HEADS):
        q3 = jnp.concatenate([q_ref[kh * ATTN_GROUP + g] for g in range(ATTN_GROUP)], axis=0)
        kcat = jnp.concatenate([kp_ref[kh], ko_ref[kh], kn_ref[kh], kc_ref[kh]], axis=0)
        vcat = jnp.concatenate([vp_ref[kh], vo_ref[kh], vn_ref[kh], vc_ref[kh]], axis=0)
        s = lax.dot_general(q3, kcat, (((1,), (1,)), ((), ())), preferred_element_type=F32)
        s = jnp.where(mask, s, NEG_INF)
        sink = jnp.zeros((rows, 1), F32)
        for g in range(ATTN_GROUP):
            sink = jnp.where(grp == g, sink_ref[kh * ATTN_GROUP + g], sink)
        m = jnp.maximum(jnp.max(s, axis=-1, keepdims=True), sink)
        p = jnp.exp(s - m)
        pv = jnp.dot(p.astype(BF16), vcat, preferred_element_type=F32)
        den = pv[:, ATTN_HEAD_DIM:ATTN_HEAD_DIM + 1] + jnp.exp(sink - m)
        o = pv[:, :ATTN_HEAD_DIM] / den
        for g in range(ATTN_GROUP):
            h = kh * ATTN_GROUP + g
            o_ref[:, h * ATTN_HEAD_DIM:(h + 1) * ATTN_HEAD_DIM] = o[g * B:(g + 1) * B].astype(o_ref.dtype)


def attention(qh, kh, vh, sink):
    n = qh.shape[1]
    B = ATTN_BLOCK
    nblk = n // B
    cb = CTX_LEN // B
    prev = lambda i: (0, jnp.clip(i - 1, cb, nblk - 1), 0)
    own = lambda i: (0, i, 0)
    nxt = lambda i: (0, jnp.clip(i + 1, cb, nblk - 1), 0)
    ctx = lambda i: (0, 0, 0)
    kv = lambda m, w=ATTN_HEAD_DIM: pl.BlockSpec((ATTN_KV_HEADS, B, w), m)
    kvc = lambda w=ATTN_HEAD_DIM: pl.BlockSpec((ATTN_KV_HEADS, CTX_LEN, w), ctx)
    vw = 2 * ATTN_HEAD_DIM
    return pl.pallas_call(
        functools.partial(_attn_kernel, n_blocks=nblk),
        grid=(nblk,),
        in_specs=[
            pl.BlockSpec(memory_space=pltpu.SMEM),
            pl.BlockSpec((ATTN_HEADS, B, ATTN_HEAD_DIM), own),
            kv(prev), kv(own), kv(nxt), kvc(),
            kv(prev, vw), kv(own, vw), kv(nxt, vw), kvc(vw),
        ],
        out_specs=pl.BlockSpec((B, ATTN_W), lambda i: (i, 0)),
        out_shape=jax.ShapeDtypeStruct((n, ATTN_W), BF16),
        compiler_params=_cparams(("parallel",)),
        name="attention",
    )(sink, qh, kh, kh, kh, kh, vh, vh, vh, vh)


def rope_tables(n_lat):
    rows = n_lat // GRID_W
    row = np.repeat(np.arange(rows), GRID_W).astype(np.float32)
    col = np.tile(np.arange(GRID_W), rows).astype(np.float32)
    inv = (ROPE_BASE ** (-np.arange(ROPE_F, dtype=np.float32) / ROPE_F)).astype(np.float32)
    ar, ac = row[:, None] * inv, col[:, None] * inv
    cos = np.concatenate([np.cos(ar), np.cos(ar), np.cos(ac), np.cos(ac)], axis=1)
    sin = np.concatenate([-np.sin(ar), np.sin(ar), -np.sin(ac), np.sin(ac)], axis=1)
    cos = np.concatenate([np.ones((CTX_LEN, ATTN_HEAD_DIM), np.float32), cos], axis=0)
    sin = np.concatenate([np.zeros((CTX_LEN, ATTN_HEAD_DIM), np.float32), sin], axis=0)
    return jnp.asarray(cos, F32), jnp.asarray(sin, F32)


def attn_mixer(z, q_norm_w, k_norm_w, sink, cos_t, sin_t):
    qh, kh, vh = attn_prep(z, q_norm_w, k_norm_w, cos_t, sin_t)
    return attention(qh, kh, vh, sink)


DN_QKV = 3 * DN_W
DN_HALO = 16
DN_DH = 2 * DN_HEADS
DN_C = DN_CHUNK
DN_SPS = 4
DN_CPS = 4


def _dn_prep_kernel(zm_ref, zp_ref, zn_ref, zg_ref, cw_ref, al_ref, dtb_ref, q_ref, k_ref, v_ref, g_ref, *, tm, n_tiles):
    i = pl.program_id(0)
    ctx_tiles = CTX_LEN // tm
    has_prev = (i != 0) & (i != ctx_tiles)
    has_next = (i != ctx_tiles - 1) & (i != n_tiles - 1)
    prev = jnp.where(has_prev, zp_ref[...].astype(F32), 0.0)
    nxt = jnp.where(has_next, zn_ref[...].astype(F32), 0.0)
    xcat = jnp.concatenate([prev, zm_ref[...].astype(F32), nxt], axis=0)
    half = DN_CONV // 2
    acc = None
    for t in range(DN_CONV):
        off = DN_HALO - half + t
        term = xcat[off:off + tm, :] * cw_ref[t:t + 1, :]
        acc = term if acc is None else acc + term
    y = acc * jax.nn.sigmoid(acc)
    for h in range(DN_HEADS):
        qh = y[:, h * DN_HEAD_DIM:(h + 1) * DN_HEAD_DIM]
        kh = y[:, DN_W + h * DN_HEAD_DIM:DN_W + (h + 1) * DN_HEAD_DIM]
        qn = qh * (lax.rsqrt(jnp.sum(qh * qh, axis=-1, keepdims=True) + EPS) * (DN_HEAD_DIM ** -0.5))
        q_ref[h] = qn.astype(q_ref.dtype)
        k_ref[h] = (kh * lax.rsqrt(jnp.sum(kh * kh, axis=-1, keepdims=True) + EPS)).astype(k_ref.dtype)
        v_ref[h] = y[:, 2 * DN_W + h * DN_HEAD_DIM:2 * DN_W + (h + 1) * DN_HEAD_DIM].astype(v_ref.dtype)
    zg = zg_ref[...].astype(F32)
    lane = lax.broadcasted_iota(jnp.int32, zg.shape, 1)
    beta = jax.nn.sigmoid(zg)
    alpha = pltpu.roll(zg, 128 - DN_DH, 1)
    gl = -jnp.exp(al_ref[...]) * jax.nn.softplus(alpha + dtb_ref[...])
    g_ref[...] = jnp.where(lane < DN_DH, beta, pltpu.roll(gl, DN_DH, 1))


def dn_prep(z, conv_w, a_log, dt_bias):
    tm = 256
    n = z.shape[0]
    n_tiles = n // tm
    hb = tm // DN_HALO
    cw = jnp.zeros((8, DN_QKV), F32).at[0:DN_CONV].set(conv_w)
    pad_row = lambda t: jnp.zeros((1, 128), F32).at[0, 0:DN_DH].set(t.reshape(-1))
    hm = pl.BlockSpec((DN_HEADS, tm, DN_HEAD_DIM), lambda i: (0, i, 0))
    sds = jax.ShapeDtypeStruct((DN_HEADS, n, DN_HEAD_DIM), BF16)
    return pl.pallas_call(
        functools.partial(_dn_prep_kernel, tm=tm, n_tiles=n_tiles),
        grid=(n_tiles,),
        in_specs=[
            pl.BlockSpec((tm, DN_QKV), lambda i: (i, 0)),
            pl.BlockSpec((DN_HALO, DN_QKV), lambda i: (jnp.maximum(i * hb - 1, 0), 0)),
            pl.BlockSpec((DN_HALO, DN_QKV), lambda i: (jnp.minimum((i + 1) * hb, n // DN_HALO - 1), 0)),
            pl.BlockSpec((tm, 128), lambda i: (i, Z_GATES // 128)),
            pl.BlockSpec((8, DN_QKV), lambda i: (0, 0)),
            pl.BlockSpec((1, 128), lambda i: (0, 0)),
            pl.BlockSpec((1, 128), lambda i: (0, 0)),
        ],
        out_specs=[hm, hm, hm, pl.BlockSpec((tm, 128), lambda i: (i, 0))],
        out_shape=[sds, sds, sds, jax.ShapeDtypeStruct((n, 128), F32)],
        compiler_params=_cparams(("parallel",)),
        name="dn_prep",
    )(z, z, z, z, cw, pad_row(a_log), pad_row(dt_bias))


def _bdot(a, b):
    return jnp.dot(a.astype(BF16), b.astype(BF16), preferred_element_type=F32)


def _unit_tri_inverse_many(nmats, row, col):
    eye = (row == col).astype(F32)
    same = lambda b: (row // b) == (col // b)
    d1 = [jnp.where(same(8), m, 0.0) for m in nmats]
    d2 = [_bdot(a, a) for a in d1]
    d3 = [_bdot(a, b) for a, b in zip(d1, d2)]
    d4 = [_bdot(b, b) for b in d2]
    x = [eye + a + b + c for a, b, c in zip(d1, d2, d3)]
    t = [a + _bdot(a, b) for a, b in zip(x, d4)]
    for b in (8, 16, 32):
        sel = same(2 * b) & jnp.logical_not(same(b))
        tmp = [_bdot(jnp.where(sel, m, 0.0), a) for m, a in zip(nmats, t)]
        t = [a + _bdot(a, c) for a, c in zip(t, tmp)]
    return t


def _dn_chunk_kernel(q_ref, k_ref, v_ref, g_ref, u_ref, wq_ref, kgt_ref, aqk_ref, gl_ref):
    C = DN_C
    row = lax.broadcasted_iota(jnp.int32, (C, C), 0)
    col = lax.broadcasted_iota(jnp.int32, (C, C), 1)
    lane = lax.broadcasted_iota(jnp.int32, (C, 128), 1)
    lower = (row >= col).astype(F32)
    upper = (row <= col).astype(F32)
    inst = [(c, d, h) for c in range(DN_CPS) for d in range(2) for h in range(DN_HEADS)]
    rows = lambda c: slice(c * C, (c + 1) * C)
    g_all, gc_all, gc_t, tot = [], [], [], []
    for c in range(DN_CPS):
        g = g_ref[rows(c), :]
        gpart = jnp.where((lane >= DN_DH) & (lane < 2 * DN_DH), g, 0.0)
        csum_f = jnp.dot(lower, gpart, preferred_element_type=F32, precision=HIGHEST)
        csum_r = jnp.dot(upper, gpart, preferred_element_type=F32, precision=HIGHEST)
        gc = jnp.where(lane < DN_DH + DN_HEADS, csum_f, csum_r)
        t = jnp.sum(gpart, axis=0, keepdims=True)
        gl_ref[c] = jnp.broadcast_to(jnp.exp(t), (8, 128))
        g_all.append(g)
        gc_all.append(gc)
        gc_t.append(jnp.transpose(gc))
        tot.append(t)
    qs = {(c, h): q_ref[h, rows(c), :].astype(F32) for c in range(DN_CPS) for h in range(DN_HEADS)}
    ks = {(c, h): k_ref[h, rows(c), :].astype(F32) for c in range(DN_CPS) for h in range(DN_HEADS)}
    kbs, egcs, decays, kks = [], [], [], []
    for c, d, h in inst:
        j = d * DN_HEADS + h
        incl = (row >= col) if d == 0 else (row <= col)
        gc_col = gc_all[c][:, DN_DH + j:DN_DH + j + 1]
        gc_row = gc_t[c][DN_DH + j:DN_DH + j + 1, :]
        decays.append(jnp.where(incl, jnp.exp(jnp.where(incl, gc_col - gc_row, 0.0)), 0.0))
        egcs.append(jnp.exp(gc_col))
        kbs.append(ks[c, h] * g_all[c][:, j:j + 1])
    for (c, d, h), kb in zip(inst, kbs):
        kks.append(lax.dot_general(jnp.concatenate([kb, qs[c, h]], axis=0).astype(BF16), ks[c, h].astype(BF16),
                                   (((1,), (1,)), ((), ())), preferred_element_type=F32))
    nmats = []
    for (c, d, h), kk, decay in zip(inst, kks, decays):
        strict = (row > col) if d == 0 else (row < col)
        nmats.append(jnp.where(strict, -(kk[:C] * decay), 0.0))
    tinv = _unit_tri_inverse_many(nmats, row, col)
    sols = []
    for (c, d, h), t, kb, egc in zip(inst, tinv, kbs, egcs):
        j = d * DN_HEADS + h
        rhs = jnp.concatenate([v_ref[h, rows(c), :].astype(F32) * g_all[c][:, j:j + 1], kb * egc], axis=1)
        sols.append(_bdot(t, rhs))
    for (c, d, h), sol, kk, decay, egc in zip(inst, sols, kks, decays, egcs):
        j = d * DN_HEADS + h
        incl = (row >= col) if d == 0 else (row <= col)
        gc_col = gc_all[c][:, DN_DH + j:DN_DH + j + 1]
        tot_j = tot[c][:, DN_DH + j:DN_DH + j + 1]
        u_ref[j, rows(c), :] = sol[:, :DN_HEAD_DIM]
        wq_ref[j, c * 2 * C:(c + 1) * 2 * C, :] = jnp.concatenate([sol[:, DN_HEAD_DIM:], qs[c, h] * egc],
                                                                  axis=0).astype(BF16)
        kgt_ref[j, c * DN_HEAD_DIM:(c + 1) * DN_HEAD_DIM, :] = jnp.transpose(
            ks[c, h] * jnp.exp(tot_j - gc_col)).astype(BF16)
        aqk_ref[j, rows(c), :] = jnp.where(incl, kk[C:] * decay, 0.0).astype(BF16)


def dn_chunk(qn, kn, vn, gates):
    n = qn.shape[1]
    nc = n // DN_C
    rows = DN_CPS * DN_C
    hm = pl.BlockSpec((DN_HEADS, rows, DN_HEAD_DIM), lambda c: (0, c, 0))
    sds = jax.ShapeDtypeStruct
    return pl.pallas_call(
        _dn_chunk_kernel,
        grid=(nc // DN_CPS,),
        in_specs=[hm, hm, hm, pl.BlockSpec((rows, 128), lambda c: (c, 0))],
        out_specs=[
            pl.BlockSpec((DN_DH, rows, DN_HEAD_DIM), lambda c: (0, c, 0)),
            pl.BlockSpec((DN_DH, 2 * rows, DN_HEAD_DIM), lambda c: (0, c, 0)),
            pl.BlockSpec((DN_DH, DN_CPS * DN_HEAD_DIM, DN_C), lambda c: (0, c, 0)),
            pl.BlockSpec((DN_DH, rows, DN_C), lambda c: (0, c, 0)),
            pl.BlockSpec((DN_CPS, 8, 128), lambda c: (c, 0, 0)),
        ],
        out_shape=[
            sds((DN_DH, n, DN_HEAD_DIM), F32),
            sds((DN_DH, 2 * n, DN_HEAD_DIM), BF16),
            sds((DN_DH, nc * DN_HEAD_DIM, DN_C), BF16),
            sds((DN_DH, n, DN_C), BF16),
            sds((nc, 8, 128), F32),
        ],
        compiler_params=_cparams(("parallel",)),
        name="dn_chunk",
    )(qn, kn, vn, gates)


def _dn_scan_kernel(uf_ref, wqf_ref, kgf_ref, aqf_ref, glf_ref, ur_ref, wqr_ref, kgr_ref, aqr_ref, glr_ref,
                    of_ref, or_ref, state):
    @pl.when(pl.program_id(0) == 0)
    def _():
        state[...] = jnp.zeros_like(state)

    C = DN_C
    sets = ((uf_ref, wqf_ref, kgf_ref, aqf_ref, glf_ref, of_ref), (ur_ref, wqr_ref, kgr_ref, aqr_ref, glr_ref, or_ref))
    chains = [(d, h) + sets[d] for d in range(2) for h in range(DN_HEADS)]
    ss = [state[d * DN_HEADS + h] for d, h, *_ in chains]
    for step in range(DN_SPS):
        sub = [step if d == 0 else DN_SPS - 1 - step for d, *_ in chains]
        ts = [jnp.dot(wq_ref[h, c * 2 * C:(c + 1) * 2 * C, :], s.astype(BF16), preferred_element_type=F32)
              for (d, h, u_ref, wq_ref, *_), s, c in zip(chains, ss, sub)]
        vbs = [(u_ref[h, c * C:(c + 1) * C, :] - t[:C]).astype(BF16) for (d, h, u_ref, *_), t, c in zip(chains, ts, sub)]
        os_ = [t[C:] + jnp.dot(aq_ref[h, c * C:(c + 1) * C, :], vb, preferred_element_type=F32)
               for (d, h, u_ref, wq_ref, kg_ref, aq_ref, *_), t, vb, c in zip(chains, ts, vbs, sub)]
        ss = [s * gl_ref[c][0:1, DN_DH + d * DN_HEADS + h:DN_DH + d * DN_HEADS + h + 1]
              + jnp.dot(kg_ref[h, c * DN_HEAD_DIM:(c + 1) * DN_HEAD_DIM, :], vb, preferred_element_type=F32)
              for (d, h, u_ref, wq_ref, kg_ref, aq_ref, gl_ref, o_ref), s, vb, c in zip(chains, ss, vbs, sub)]
        for (d, h, *_, o_ref), o, c in zip(chains, os_, sub):
            o_ref[h, c * C:(c + 1) * C, :] = o.astype(o_ref.dtype)
    for (d, h, *_), s_new in zip(chains, ss):
        state[d * DN_HEADS + h] = s_new


def dn_scan(u, wq, kgt, aqk, gl):
    n = u.shape[1]
    nb = n // (DN_C * DN_SPS)
    cb = CTX_LEN // (DN_C * DN_SPS)
    rows = DN_C * DN_SPS
    fwd = lambda s: s
    rev = lambda s: jnp.where(s < cb, cb - 1 - s, nb + cb - 1 - s)
    specs = []
    for d, cm in enumerate((fwd, rev)):
        specs += [
            pl.BlockSpec((DN_HEADS, rows, DN_HEAD_DIM), lambda s, d=d, cm=cm: (d, cm(s), 0)),
            pl.BlockSpec((DN_HEADS, 2 * rows, DN_HEAD_DIM), lambda s, d=d, cm=cm: (d, cm(s), 0)),
            pl.BlockSpec((DN_HEADS, DN_SPS * DN_HEAD_DIM, DN_C), lambda s, d=d, cm=cm: (d, cm(s), 0)),
            pl.BlockSpec((DN_HEADS, rows, DN_C), lambda s, d=d, cm=cm: (d, cm(s), 0)),
            pl.BlockSpec((DN_SPS, 8, 128), lambda s, cm=cm: (cm(s), 0, 0)),
        ]
    osd = jax.ShapeDtypeStruct((DN_HEADS, n, DN_HEAD_DIM), BF16)
    return pl.pallas_call(
        _dn_scan_kernel,
        grid=(nb,),
        in_specs=specs,
        out_specs=[pl.BlockSpec((DN_HEADS, rows, DN_HEAD_DIM), lambda s: (0, fwd(s), 0)),
                   pl.BlockSpec((DN_HEADS, rows, DN_HEAD_DIM), lambda s: (0, rev(s), 0))],
        out_shape=[osd, osd],
        scratch_shapes=[pltpu.VMEM((DN_DH, DN_HEAD_DIM, DN_HEAD_DIM), F32)],
        compiler_params=_cparams(("arbitrary",)),
        name="dn_scan",
    )(u, wq, kgt, aqk, gl, u, wq, kgt, aqk, gl)


def dn_mixer(z, conv_w, a_log, dt_bias):
    qn, kn, vn, gates = dn_prep(z, conv_w, a_log, dt_bias)
    u, wq, kgt, aqk, gl = dn_chunk(qn, kn, vn, gates)
    return dn_scan(u, wq, kgt, aqk, gl)


S5_SUB = 8
S5_BLK = S5_GROUPS * S5_STATE // S5_SUB
S5_UB = S5_W // S5_SUB
S5_T = 256


def _s5_param_kernel(lr_ref, li_ref, ls_ref, br_ref, bi_ref, ar_ref, ai_ref, bbr_ref, bbi_ref):
    lr = jnp.minimum(lr_ref[...], -1e-4)
    li = li_ref[...]
    dt = jnp.exp(ls_ref[...])
    mag = jnp.exp(lr * dt)
    ar = mag * jnp.cos(li * dt)
    ai = mag * jnp.sin(li * dt)
    den = lr * lr + li * li
    nr, ni = ar - 1.0, ai
    fr = (nr * lr + ni * li) / den
    fi = (ni * lr - nr * li) / den
    ar_ref[...] = ar
    ai_ref[...] = ai
    for h in range(S5_GROUP_CH):
        bbr_ref[h] = fr * br_ref[h] - fi * bi_ref[h]
        bbi_ref[h] = fr * bi_ref[h] + fi * br_ref[h]


def s5_params(lam_re, lam_im, log_step, b_re, b_im):
    r = 2 * S5_GROUPS
    ls = jnp.broadcast_to(log_step.reshape(r, 1), (r, S5_STATE))
    bt = lambda t: jnp.transpose(t.reshape(r, S5_STATE, S5_GROUP_CH), (2, 0, 1))
    sds = jax.ShapeDtypeStruct
    return pl.pallas_call(
        _s5_param_kernel,
        out_shape=[sds((r, S5_STATE), F32), sds((r, S5_STATE), F32),
                   sds((S5_GROUP_CH, r, S5_STATE), F32), sds((S5_GROUP_CH, r, S5_STATE), F32)],
        name="s5_params",
    )(lam_re.reshape(r, S5_STATE), lam_im.reshape(r, S5_STATE), ls, bt(b_re), bt(b_im))


def _s5_scan_kernel(uf_ref, ur_ref, a_ref, rb_ref, cc_ref, yf_ref, yr_ref, bu, hh, st):
    rows = S5_T * S5_SUB

    @pl.when(pl.program_id(0) == 0)
    def _():
        st[...] = jnp.zeros_like(st)

    sub = lax.broadcasted_iota(jnp.int32, (rows, S5_W), 0) % S5_SUB
    blk = lax.broadcasted_iota(jnp.int32, (rows, S5_W), 1) // S5_UB
    own = sub == blk
    for d, u_ref in enumerate((uf_ref, ur_ref)):
        ue = jnp.broadcast_to(u_ref[...].astype(F32)[:, None, :], (S5_T, S5_SUB, S5_W)).reshape(rows, S5_W)
        lhs = jnp.where(own, ue, 0.0).astype(BF16)
        bu[d, 0] = jnp.dot(lhs, rb_ref[d, 0], preferred_element_type=F32)
        bu[d, 1] = jnp.dot(lhs, rb_ref[d, 1], preferred_element_type=F32)

    a = [[a_ref[d, c] for c in range(2)] for d in range(2)]

    def body(t, carry):
        fr, fi, rr, ri = carry
        rf = pl.multiple_of(t * S5_SUB, S5_SUB)
        rv = pl.multiple_of((S5_T - 1 - t) * S5_SUB, S5_SUB)
        nfr = a[0][0] * fr - a[0][1] * fi + bu[0, 0, pl.ds(rf, S5_SUB), :]
        nfi = a[0][0] * fi + a[0][1] * fr + bu[0, 1, pl.ds(rf, S5_SUB), :]
        nrr = a[1][0] * rr - a[1][1] * ri + bu[1, 0, pl.ds(rv, S5_SUB), :]
        nri = a[1][0] * ri + a[1][1] * rr + bu[1, 1, pl.ds(rv, S5_SUB), :]
        hh[0, 0, pl.ds(rf, S5_SUB), :] = nfr
        hh[0, 1, pl.ds(rf, S5_SUB), :] = nfi
        hh[1, 0, pl.ds(rv, S5_SUB), :] = nrr
        hh[1, 1, pl.ds(rv, S5_SUB), :] = nri
        return nfr, nfi, nrr, nri

    fin = lax.fori_loop(0, S5_T, body, (st[0, 0], st[0, 1], st[1, 0], st[1, 1]), unroll=8)
    st[0, 0], st[0, 1], st[1, 0], st[1, 1] = fin

    for d, y_ref in enumerate((yf_ref, yr_ref)):
        ye = (jnp.dot(hh[d, 0].astype(BF16), cc_ref[d, 0], preferred_element_type=F32)
              + jnp.dot(hh[d, 1].astype(BF16), cc_ref[d, 1], preferred_element_type=F32))
        ye = jnp.where(own, ye, 0.0)
        y_ref[...] = jnp.sum(ye.reshape(S5_T, S5_SUB, S5_W), axis=1)


def s5_scan(z, a8, rb, cc):
    rows = S5_T * S5_SUB
    n = z.shape[0]
    nt = n // S5_T
    ct = CTX_LEN // S5_T
    ucol = Z_S5 // S5_W

    def rev_tile(i):
        return jnp.where(i < ct, ct - 1 - i, nt + ct - 1 - i)

    full = lambda shape: pl.BlockSpec(shape, lambda i: (0,) * len(shape))
    sds = jax.ShapeDtypeStruct((n, S5_W), F32)
    return pl.pallas_call(
        _s5_scan_kernel,
        grid=(nt,),
        in_specs=[
            pl.BlockSpec((S5_T, S5_W), lambda i: (i, ucol)),
            pl.BlockSpec((S5_T, S5_W), lambda i: (rev_tile(i), ucol)),
            full((2, 2, S5_SUB, S5_BLK)),
            full((2, 2, S5_W, S5_BLK)),
            full((2, 2, S5_BLK, S5_W)),
        ],
        out_specs=[pl.BlockSpec((S5_T, S5_W), lambda i: (i, 0)),
                   pl.BlockSpec((S5_T, S5_W), lambda i: (rev_tile(i), 0))],
        out_shape=[sds, sds],
        scratch_shapes=[pltpu.VMEM((2, 2, rows, S5_BLK), F32), pltpu.VMEM((2, 2, rows, S5_BLK), F32),
                        pltpu.VMEM((2, 2, S5_SUB, S5_BLK), F32)],
        compiler_params=_cparams(("arbitrary",)),
        name="s5_scan",
    )(z, z, a8, rb, cc)


def s5_mixer(z, lam_re, lam_im, log_step, b_re, b_im, c_re, c_im):
    ar, ai, bbr, bbi = s5_params(lam_re, lam_im, log_step, b_re, b_im)
    g_blk = jax.nn.one_hot(jnp.arange(S5_GROUPS) % (S5_GROUPS // S5_SUB), S5_GROUPS // S5_SUB, dtype=F32)

    def place_b(bb):
        bb = jnp.transpose(bb.reshape(S5_GROUP_CH, 2, S5_GROUPS, S5_STATE), (1, 2, 0, 3))
        return jnp.einsum('dghp,gj->dghjp', bb, g_blk).reshape(2, S5_W, S5_BLK)

    def place_c(cm):
        return jnp.einsum('dghp,gj->djpgh', cm, g_blk).reshape(2, S5_BLK, S5_W)

    rb = jnp.stack([place_b(bbr), place_b(bbi)], axis=1).astype(BF16)
    cc = jnp.stack([place_c(c_re), -place_c(c_im)], axis=1).astype(BF16)
    a8 = jnp.stack([ar.reshape(2, S5_SUB, S5_BLK), ai.reshape(2, S5_SUB, S5_BLK)], axis=1)
    return s5_scan(z, a8, rb, cc)


def _permute_w_in(w):
    n_gate = 4 * DN_HEADS
    attn = w[:, :QKV_W]
    dn = w[:, QKV_W:QKV_W + 4 * DN_W]
    gates = w[:, QKV_W + 4 * DN_W:QKV_W + 4 * DN_W + n_gate]
    s5 = w[:, QKV_W + 4 * DN_W + n_gate:]
    pad = jnp.zeros((w.shape[0], Z_S5 - Z_GATES - n_gate), BF16)
    return jnp.concatenate([dn.astype(BF16), attn.astype(BF16), gates.astype(BF16), pad, s5.astype(BF16)], axis=1)


def kernel(x, c, ctx, c_ctx, w_ada, b_ada, norm1_w, norm2_w, w_in, w_out, attn_q_norm, attn_k_norm, attn_sink,
           dn_conv, dn_a_log, dn_dt_bias, dn_o_norm, s5_lam_re, s5_lam_im, s5_log_step, s5_b_re, s5_b_im,
           s5_c_re, s5_c_im, s5_d, s5_w_glu, moe_w_grp, moe_b_grp, moe_w_rt, moe_b_rt, moe_w1, moe_w3, moe_w2):
    b, n, d = x.shape
    assert b == 1 and ctx.shape[1] == CTX_LEN and d == D_MODEL
    lc = CTX_LEN
    xa = jnp.concatenate([ctx[0], x[0]], axis=0)
    mods = adaln(c_ctx, c[0], w_ada, b_ada)
    cos_t, sin_t = rope_tables(n)
    w1_all = moe_w1.reshape(DEPTH * N_EXPERTS, D_MODEL, EXPERT_HIDDEN)
    w3_all = moe_w3.reshape(DEPTH * N_EXPERTS, D_MODEL, EXPERT_HIDDEN)
    w2_all = moe_w2.reshape(DEPTH * N_EXPERTS, EXPERT_HIDDEN, D_MODEL)
    for layer in range(DEPTH):
        mod = mods[layer, 0:2].reshape(2, 6, D_MODEL)
        mod = jnp.concatenate([mod, jnp.zeros((2, 2, D_MODEL), F32)], axis=1)
        z = in_proj(xa, norm1_w[layer], mod, _permute_w_in(w_in[layer]))
        mix_a = attn_mixer(z, attn_q_norm[layer], attn_k_norm[layer], attn_sink[layer], cos_t, sin_t)
        o_f, o_r = dn_mixer(z, dn_conv[layer], dn_a_log[layer], dn_dt_bias[layer])
        yf, yr = s5_mixer(z, s5_lam_re[layer], s5_lam_im[layer], s5_log_step[layer], s5_b_re[layer],
                          s5_b_im[layer], s5_c_re[layer], s5_c_im[layer])
        w_router = jnp.concatenate([moe_w_grp[layer], moe_w_rt[layer],
                                    jnp.zeros((D_MODEL, 128 - N_GROUPS - N_EXPERTS), F32)], axis=1)
        b_router = jnp.concatenate([moe_b_grp[layer], moe_b_rt[layer],
                                    jnp.zeros((128 - N_GROUPS - N_EXPERTS,), F32)]).reshape(1, 128)
        xa, h2, route = out_proj(mix_a, o_f, o_r, z, yf, yr, xa, mod, w_out[layer].astype(BF16), norm2_w[layer],
                                 w_router, b_router, dn_o_norm[layer], s5_d[layer], s5_w_glu[layer].astype(BF16))
        xa = moe_block(xa, h2, route, mod, w1_all, w3_all, w2_all, layer, skip_ctx=(layer == DEPTH - 1))
    return xa[None]
```

```python
import functools

import jax
import jax.numpy as jnp
import numpy as np
from jax import lax
from jax.experimental import pallas as pl
from jax.experimental.pallas import tpu as pltpu

F32 = jnp.float32
BF16 = jnp.bfloat16
HIGHEST = lax.Precision.HIGHEST

D_MODEL = 2048
SEQ = 8192
DEPTH = 2
GRID_W = 64
CTX_LEN = 256
N_ALL = CTX_LEN + SEQ
EPS = 1e-6
NEG_INF = -1e30

ATTN_HEADS = 12
ATTN_KV_HEADS = 4
ATTN_HEAD_DIM = 64
ATTN_GROUP = ATTN_HEADS // ATTN_KV_HEADS
ATTN_W = ATTN_HEADS * ATTN_HEAD_DIM
ATTN_KV_W = ATTN_KV_HEADS * ATTN_HEAD_DIM
WINDOW = 128
ATTN_BLOCK = 128
ROPE_BASE = 10000.0
DN_HEADS = 6
DN_HEAD_DIM = 128
DN_W = DN_HEADS * DN_HEAD_DIM
DN_CONV = 5
DN_CHUNK = 64
S5_W = D_MODEL - ATTN_W - DN_W
S5_GROUP_CH = 16
S5_GROUPS = S5_W // S5_GROUP_CH
S5_STATE = 64
N_GROUPS = 4
EXPERTS_PER_GROUP = 8
N_EXPERTS = N_GROUPS * EXPERTS_PER_GROUP
TOP_K = 2
EXPERT_HIDDEN = 512

Z_DNQ, Z_DNK, Z_DNV, Z_DNG = 0, 768, 1536, 2304
Z_Q, Z_K, Z_V = 3072, 3840, 4096
Z_GATES = 4352
Z_BETA, Z_ALPHA = Z_GATES, Z_GATES + 2 * DN_HEADS
Z_S5 = 4608
Z_W = 5120

VMEM_LIMIT = 56 * 1024 * 1024


def _cparams(sem, vmem=VMEM_LIMIT):
    return pltpu.CompilerParams(dimension_semantics=sem, vmem_limit_bytes=vmem)


ADALN_TN = 1024


def _adaln_kernel(c_ref, w_ref, b_ref, o_ref):
    reps = ADALN_TN // 128

    def body(kb, acc):
        r0 = pl.multiple_of(kb * 8, 8)
        w = w_ref[0, pl.ds(r0, 8), :]
        out = []
        for r in range(2):
            cv = c_ref[r, pl.ds(r0, 8), :]
            s = cv * jax.nn.sigmoid(cv)
            out.append(acc[r] + w * jnp.concatenate([s] * reps, axis=1))
        return tuple(out)

    zero = jnp.zeros((8, ADALN_TN), F32)
    acc = lax.fori_loop(0, D_MODEL // 8, body, (zero, zero), unroll=4)
    rows = [jnp.sum(a, axis=0, keepdims=True) + b_ref[0] for a in acc]
    o_ref[0] = jnp.concatenate(rows + [jnp.zeros((6, ADALN_TN), F32)], axis=0)


def adaln(c_ctx, c, w_ada, b_ada):
    L = w_ada.shape[0]
    n6 = 6 * D_MODEL
    cb = jnp.broadcast_to(jnp.stack([c_ctx, c])[:, :, None], (2, D_MODEL, 128))
    return pl.pallas_call(
        _adaln_kernel,
        grid=(L, n6 // ADALN_TN),
        in_specs=[
            pl.BlockSpec((2, D_MODEL, 128), lambda l, j: (0, 0, 0)),
            pl.BlockSpec((1, D_MODEL, ADALN_TN), lambda l, j: (l, 0, j)),
            pl.BlockSpec((1, 1, ADALN_TN), lambda l, j: (l, 0, j)),
        ],
        out_specs=pl.BlockSpec((1, 8, ADALN_TN), lambda l, j: (l, 0, j)),
        out_shape=jax.ShapeDtypeStruct((L, 8, n6), F32),
        compiler_params=_cparams(("parallel", "parallel")),
        name="adaln",
    )(cb, w_ada, b_ada.reshape(L, 1, n6))


NORM_ROWS = 64


def _is_ctx_rows(base, rows):
    return base + lax.broadcasted_iota(jnp.int32, (rows, 1), 0) < CTX_LEN


def _row_is_ctx(tm):
    return _is_ctx_rows(pl.program_id(0) * tm, tm)


def _norm_mod(x, nw, mod_ref, shift_i, scale_i, is_ctx):
    ms = jnp.mean(x * x, axis=-1, keepdims=True)
    h = x * lax.rsqrt(ms + EPS) * nw
    sc = jnp.where(is_ctx, mod_ref[0, scale_i:scale_i + 1, :], mod_ref[1, scale_i:scale_i + 1, :])
    sh = jnp.where(is_ctx, mod_ref[0, shift_i:shift_i + 1, :], mod_ref[1, shift_i:shift_i + 1, :])
    return h * (1.0 + sc) + sh


def _in_proj_kernel(x_ref, nw_ref, mod_ref, w_ref, o_ref, h_scr, *, tm):
    @pl.when(pl.program_id(1) == 0)
    def _():
        def chunk(c, carry):
            r0 = pl.multiple_of(c * NORM_ROWS, NORM_ROWS)
            h = _norm_mod(x_ref[pl.ds(r0, NORM_ROWS), :], nw_ref[...], mod_ref, 0, 1,
                          _is_ctx_rows(pl.program_id(0) * tm + r0, NORM_ROWS))
            h_scr[pl.ds(r0, NORM_ROWS), :] = h.astype(BF16)
            return carry
        lax.fori_loop(0, tm // NORM_ROWS, chunk, 0)

    o_ref[...] = jnp.dot(h_scr[...], w_ref[...], preferred_element_type=F32).astype(o_ref.dtype)


def in_proj(xa, norm_w, mod, w_in_p):
    tm, tn = 1024, 1024
    n = xa.shape[0]
    return pl.pallas_call(
        functools.partial(_in_proj_kernel, tm=tm),
        grid=(pl.cdiv(n, tm), Z_W // tn),
        in_specs=[
            pl.BlockSpec((tm, D_MODEL), lambda i, j: (i, 0)),
            pl.BlockSpec((1, D_MODEL), lambda i, j: (0, 0)),
            pl.BlockSpec((2, 8, D_MODEL), lambda i, j: (0, 0, 0)),
            pl.BlockSpec((D_MODEL, tn), lambda i, j: (0, j)),
        ],
        out_specs=pl.BlockSpec((tm, tn), lambda i, j: (i, j)),
        out_shape=jax.ShapeDtypeStruct((n, Z_W), BF16),
        scratch_shapes=[pltpu.VMEM((tm, D_MODEL), BF16)],
        compiler_params=_cparams(("parallel", "arbitrary")),
        name="in_proj",
    )(xa, norm_w.reshape(1, D_MODEL), mod, w_in_p)


def _out_proj_kernel(a_ref, of_ref, or_ref, zg_ref, onw_ref, zu_ref, yf_ref, yr_ref, dsk_ref, wg_ref,
                     x_ref, mod_ref, w_ref, nw_ref, wr_ref, br_ref, xo_ref, h_ref, rt_ref, *, tm):
    is_ctx = _row_is_ctx(tm)
    b_parts = []
    for hd in range(DN_HEADS):
        o = of_ref[hd].astype(F32) + or_ref[hd].astype(F32)
        o = o * lax.rsqrt(jnp.mean(o * o, axis=-1, keepdims=True) + EPS) * onw_ref[...]
        g = zg_ref[:, hd * DN_HEAD_DIM:(hd + 1) * DN_HEAD_DIM].astype(F32)
        b_parts.append((o * (g * jax.nn.sigmoid(g))).astype(BF16))
    b = jnp.concatenate(b_parts, axis=1)
    ys = jax.nn.gelu(zu_ref[...].astype(F32) * dsk_ref[...] + yf_ref[...] + yr_ref[...])
    s = (ys * jax.nn.sigmoid(jnp.dot(ys.astype(BF16), wg_ref[...], preferred_element_type=F32))).astype(BF16)
    y = (jnp.dot(a_ref[...], w_ref[0:ATTN_W, :], preferred_element_type=F32)
         + jnp.dot(b, w_ref[ATTN_W:ATTN_W + DN_W, :], preferred_element_type=F32)
         + jnp.dot(s, w_ref[ATTN_W + DN_W:, :], preferred_element_type=F32))
    gate = jnp.where(is_ctx, mod_ref[0, 2:3, :], mod_ref[1, 2:3, :])
    xn = x_ref[...] + gate * y
    xo_ref[...] = xn
    h = _norm_mod(xn, nw_ref[...], mod_ref, 3, 4, is_ctx)
    h_ref[...] = h
    wr = wr_ref[...]
    h_hi = h.astype(BF16)
    h_lo = (h - h_hi.astype(F32)).astype(BF16)
    w_hi = wr.astype(BF16)
    w_lo = (wr - w_hi.astype(F32)).astype(BF16)
    lg = (jnp.dot(h_hi, w_hi, preferred_element_type=F32) + jnp.dot(h_hi, w_lo, preferred_element_type=F32)
          + jnp.dot(h_lo, w_hi, preferred_element_type=F32) + br_ref[...])
    rt_ref[...] = _route(lg)


def out_proj(mix_a, o_f, o_r, z, yf, yr, xa, mod, w_out_b, norm2_w, w_router, b_router, o_norm_w, d_skip, w_glu_b):
    tm = 256
    n = xa.shape[0]
    row = lambda i: (i, 0)
    const2 = lambda i: (0, 0)
    hm = pl.BlockSpec((DN_HEADS, tm, DN_HEAD_DIM), lambda i: (0, i, 0))
    return pl.pallas_call(
        functools.partial(_out_proj_kernel, tm=tm),
        grid=(n // tm,),
        in_specs=[
            pl.BlockSpec((tm, ATTN_W), row),
            hm, hm,
            pl.BlockSpec((tm, DN_W), lambda i: (i, Z_DNG // DN_W)),
            pl.BlockSpec((1, DN_HEAD_DIM), const2),
            pl.BlockSpec((tm, S5_W), lambda i: (i, Z_S5 // S5_W)),
            pl.BlockSpec((tm, S5_W), row),
            pl.BlockSpec((tm, S5_W), row),
            pl.BlockSpec((1, S5_W), const2),
            pl.BlockSpec((S5_W, S5_W), const2),
            pl.BlockSpec((tm, D_MODEL), row),
            pl.BlockSpec((2, 8, D_MODEL), lambda i: (0, 0, 0)),
            pl.BlockSpec((D_MODEL, D_MODEL), const2),
            pl.BlockSpec((1, D_MODEL), const2),
            pl.BlockSpec((D_MODEL, 128), const2),
            pl.BlockSpec((1, 128), const2),
        ],
        out_specs=[
            pl.BlockSpec((tm, D_MODEL), row),
            pl.BlockSpec((tm, D_MODEL), row),
            pl.BlockSpec((tm, 128), row),
        ],
        out_shape=[
            jax.ShapeDtypeStruct((n, D_MODEL), F32),
            jax.ShapeDtypeStruct((n, D_MODEL), F32),
            jax.ShapeDtypeStruct((n, 128), F32),
        ],
        compiler_params=_cparams(("parallel",)),
        name="out_proj",
    )(mix_a, o_f, o_r, z, o_norm_w.reshape(1, DN_HEAD_DIM), z, yf, yr, d_skip.reshape(1, S5_W), w_glu_b,
      xa, mod, w_out_b, norm2_w.reshape(1, D_MODEL), w_router, b_router)


def _route(lg):
    tm = lg.shape[0]
    lane = lax.broadcasted_iota(jnp.int32, (tm, 128), 1)
    is_g = lane < N_GROUPS
    gl = jnp.where(is_g, lg, NEG_INF)
    gmax = jnp.max(gl, axis=-1, keepdims=True)
    gidx = jnp.min(jnp.where((gl == gmax) & is_g, lane, 128), axis=-1, keepdims=True)
    gsum = jnp.sum(jnp.where(is_g, jnp.exp(gl - gmax), 0.0), axis=-1, keepdims=True)
    g_w = 1.0 / gsum
    e_lane = lane - N_GROUPS
    in_grp = (e_lane >= gidx * EXPERTS_PER_GROUP) & (e_lane < (gidx + 1) * EXPERTS_PER_GROUP)
    el = jnp.where(in_grp, lg, NEG_INF)
    v1 = jnp.max(el, axis=-1, keepdims=True)
    i1 = jnp.min(jnp.where((el == v1) & in_grp, e_lane, 128), axis=-1, keepdims=True)
    el2 = jnp.where(e_lane == i1, NEG_INF, el)
    in2 = in_grp & (e_lane != i1)
    v2 = jnp.max(el2, axis=-1, keepdims=True)
    i2 = jnp.min(jnp.where((el2 == v2) & in2, e_lane, 128), axis=-1, keepdims=True)
    e2 = jnp.exp(v2 - v1)
    w1 = g_w / (1.0 + e2)
    w2 = g_w * e2 / (1.0 + e2)
    out = jnp.where(lane == 0, i1.astype(F32), 0.0)
    out = jnp.where(lane == 1, i2.astype(F32), out)
    out = jnp.where(lane == 2, w1, out)
    out = jnp.where(lane == 3, w2, out)
    return out


MOE_TILE = 256


def _moe_tiles(n):
    return TOP_K * n // MOE_TILE + N_EXPERTS


def _expert_kernel(te_ref, ts_ref, tok_ref, nt_ref, h_hbm, w1_ref, w3_ref, w2_ref, y_ref,
                   xbuf0, xbuf1, sem, w1b, w3b, w2b):
    i = pl.program_id(0)
    n_tiles = nt_ref[0]
    last = tok_ref.shape[0] - 1
    bufs = (xbuf0, xbuf1)

    def gather_start(tile, slot):
        base = ts_ref[tile]
        for r in range(MOE_TILE):
            tok = tok_ref[jnp.minimum(base + r, last)]
            pltpu.make_async_copy(h_hbm.at[pl.ds(tok, 1)], bufs[slot].at[pl.ds(r, 1)], sem.at[slot]).start()

    def gather_wait(slot):
        pltpu.make_async_copy(bufs[slot], bufs[slot], sem.at[slot]).wait()

    @pl.when(i == 0)
    def _():
        gather_start(0, 0)

    prev_e = te_ref[jnp.maximum(i - 1, 0)]
    new_e = (i == 0) | (te_ref[i] != prev_e)

    @pl.when((i < n_tiles) & new_e)
    def _():
        w1b[...] = w1_ref[0].astype(BF16)
        w3b[...] = w3_ref[0].astype(BF16)
        w2b[...] = w2_ref[0].astype(BF16)

    for slot in range(2):
        @pl.when((i < n_tiles) & (i % 2 == slot))
        def _():
            gather_wait(slot)
            gather_start(jnp.minimum(i + 1, n_tiles - 1), 1 - slot)
            xt = bufs[slot][...].astype(BF16)
            a = jnp.dot(xt, w1b[...], preferred_element_type=F32)
            u = jnp.dot(xt, w3b[...], preferred_element_type=F32)
            act = (a * jax.nn.sigmoid(a)) * u
            y_ref[...] = jnp.dot(act.astype(BF16), w2b[...], preferred_element_type=F32)

            @pl.when(i == n_tiles - 1)
            def _():
                gather_wait(1 - slot)

    @pl.when(i >= n_tiles)
    def _():
        y_ref[...] = jnp.zeros_like(y_ref)


def expert_mlp(tile_expert, tile_start, sorted_tok, n_tiles, h2, w1, w3, w2):
    e_map = lambda i, te, ts, tok, nt: (te[i], 0, 0)
    moe_tiles = tile_expert.shape[0]
    grid_spec = pltpu.PrefetchScalarGridSpec(
        num_scalar_prefetch=4,
        grid=(moe_tiles,),
        in_specs=[
            pl.BlockSpec(memory_space=pl.ANY),
            pl.BlockSpec((1, D_MODEL, EXPERT_HIDDEN), e_map),
            pl.BlockSpec((1, D_MODEL, EXPERT_HIDDEN), e_map),
            pl.BlockSpec((1, EXPERT_HIDDEN, D_MODEL), e_map),
        ],
        out_specs=pl.BlockSpec((MOE_TILE, D_MODEL), lambda i, te, ts, tok, nt: (i, 0)),
        scratch_shapes=[
            pltpu.VMEM((MOE_TILE, D_MODEL), F32),
            pltpu.VMEM((MOE_TILE, D_MODEL), F32),
            pltpu.SemaphoreType.DMA((2,)),
            pltpu.VMEM((D_MODEL, EXPERT_HIDDEN), BF16),
            pltpu.VMEM((D_MODEL, EXPERT_HIDDEN), BF16),
            pltpu.VMEM((EXPERT_HIDDEN, D_MODEL), BF16),
        ],
    )
    return pl.pallas_call(
        _expert_kernel,
        grid_spec=grid_spec,
        out_shape=jax.ShapeDtypeStruct((moe_tiles * MOE_TILE, D_MODEL), F32),
        compiler_params=_cparams(("arbitrary",)),
        name="expert_mlp",
    )(tile_expert, tile_start, sorted_tok, n_tiles, h2, w1, w3, w2)


CMB_TILE = 256


def _combine_kernel(pos_ref, y_hbm, x_ref, route_ref, mod_ref, o_ref, ybuf0, ybuf1, sem, *, first_tile):
    i = pl.program_id(0)
    nt = pl.num_programs(0)
    bufs = (ybuf0, ybuf1)

    def gather_start(tile, slot):
        base = (tile + first_tile) * (CMB_TILE * TOP_K)
        for r in range(CMB_TILE):
            for k in range(TOP_K):
                p = pos_ref[base + r * TOP_K + k]
                pltpu.make_async_copy(y_hbm.at[pl.ds(p, 1)], bufs[slot].at[k, pl.ds(r, 1)], sem.at[slot]).start()

    def gather_wait(slot):
        pltpu.make_async_copy(bufs[slot], bufs[slot], sem.at[slot]).wait()

    @pl.when(i == 0)
    def _():
        gather_start(0, 0)

    for slot in range(2):
        @pl.when(i % 2 == slot)
        def _():
            gather_wait(slot)
            gather_start(jnp.minimum(i + 1, nt - 1), 1 - slot)
            is_ctx = _is_ctx_rows((i + first_tile) * CMB_TILE, CMB_TILE)
            gate = jnp.where(is_ctx, mod_ref[0, 5:6, :], mod_ref[1, 5:6, :])
            w0 = route_ref[:, TOP_K:TOP_K + 1]
            w1 = route_ref[:, TOP_K + 1:TOP_K + 2]
            o_ref[...] = x_ref[...] + gate * (w0 * bufs[slot][0] + w1 * bufs[slot][1])

            @pl.when(i == nt - 1)
            def _():
                gather_wait(1 - slot)


def moe_combine(pos, y_sorted, xa, route, mod, skip_ctx):
    first_tile = CTX_LEN // CMB_TILE if skip_ctx else 0
    n = xa.shape[0] - first_tile * CMB_TILE
    grid_spec = pltpu.PrefetchScalarGridSpec(
        num_scalar_prefetch=1,
        grid=(n // CMB_TILE,),
        in_specs=[
            pl.BlockSpec(memory_space=pl.ANY),
            pl.BlockSpec((CMB_TILE, D_MODEL), lambda i, pos: (i + first_tile, 0)),
            pl.BlockSpec((CMB_TILE, 128), lambda i, pos: (i + first_tile, 0)),
            pl.BlockSpec((2, 8, D_MODEL), lambda i, pos: (0, 0, 0)),
        ],
        out_specs=pl.BlockSpec((CMB_TILE, D_MODEL), lambda i, pos: (i, 0)),
        scratch_shapes=[
            pltpu.VMEM((TOP_K, CMB_TILE, D_MODEL), F32),
            pltpu.VMEM((TOP_K, CMB_TILE, D_MODEL), F32),
            pltpu.SemaphoreType.DMA((2,)),
        ],
    )
    return pl.pallas_call(
        functools.partial(_combine_kernel, first_tile=first_tile),
        grid_spec=grid_spec,
        out_shape=jax.ShapeDtypeStruct((n, D_MODEL), F32),
        compiler_params=_cparams(("arbitrary",)),
        name="moe_combine",
    )(pos, y_sorted, xa, route, mod)


def moe_dispatch_plan(route):
    n = route.shape[0]
    eid = route[:, 0:TOP_K].astype(jnp.int32).reshape(-1)
    p_total = eid.shape[0]
    experts = jnp.arange(N_EXPERTS, dtype=jnp.int32)
    counts = jnp.sum((eid[:, None] == experts[None, :]).astype(jnp.int32), axis=0)
    tiles_per = (counts + MOE_TILE - 1) // MOE_TILE
    tile_off = jnp.cumsum(tiles_per) - tiles_per
    off = jnp.cumsum(counts) - counts
    pair_ids = jnp.arange(p_total, dtype=jnp.int32)
    e_sorted, sorted_pair = lax.sort((eid, pair_ids), num_keys=1, is_stable=True)
    sorted_tok = sorted_pair // TOP_K
    s_onehot = (e_sorted[:, None] == experts[None, :]).astype(jnp.int32)
    pos_sorted = pair_ids + jnp.sum(s_onehot * (tile_off * MOE_TILE - off)[None, :], axis=1)
    _, pos = lax.sort((sorted_pair, pos_sorted), num_keys=1)
    moe_tiles = _moe_tiles(n)
    n_tiles = jnp.sum(tiles_per).astype(jnp.int32)
    tile_ids = jnp.arange(moe_tiles, dtype=jnp.int32)
    tile_expert = jnp.sum((tile_ids[:, None] >= (tile_off + tiles_per)[None, :]).astype(jnp.int32), axis=1)
    tile_expert = jnp.minimum(tile_expert, N_EXPERTS - 1)
    t_onehot = (tile_expert[:, None] == experts[None, :]).astype(jnp.int32)
    tile_start = (jnp.sum(t_onehot * off[None, :], axis=1)
                  + (tile_ids - jnp.sum(t_onehot * tile_off[None, :], axis=1)) * MOE_TILE)
    return tile_expert, tile_start, sorted_tok, n_tiles.reshape(1), pos


def moe_block(xa, h2, route, mod, w1, w3, w2, layer, skip_ctx):
    tile_expert, tile_start, sorted_tok, n_tiles, pos = moe_dispatch_plan(route)
    y_sorted = expert_mlp(tile_expert + layer * N_EXPERTS, tile_start, sorted_tok, n_tiles, h2, w1, w3, w2)
    return moe_combine(pos, y_sorted, xa, route, mod, skip_ctx)


QK_W = ATTN_W + ATTN_KV_W
QKV_W = QK_W + ATTN_KV_W
ROPE_F = ATTN_HEAD_DIM // 4


def _norm_rope_heads(x, nw, cos, sin, out_ref, n_heads, scale):
    tm, width = x.shape
    xw = x * nw
    lane = lax.broadcasted_iota(jnp.int32, (tm, width), 1)
    odd = (lane // ROPE_F) % 2 == 1
    xs = jnp.where(odd, pltpu.roll(xw, ROPE_F, 1), pltpu.roll(xw, width - ROPE_F, 1))
    heads = [slice(h * ATTN_HEAD_DIM, (h + 1) * ATTN_HEAD_DIM) for h in range(n_heads)]
    invs = [lax.rsqrt(jnp.mean(x[:, sl] * x[:, sl], axis=-1, keepdims=True) + EPS) * scale for sl in heads]
    for h, (sl, inv) in enumerate(zip(heads, invs)):
        out_ref[h] = ((xw[:, sl] * cos + xs[:, sl] * sin) * inv).astype(out_ref.dtype)


def _attn_prep_kernel(zq_ref, zk_ref, zv_ref, qw_ref, kw_ref, cos_ref, sin_ref, q_ref, k_ref, v_ref):
    cos = cos_ref[...]
    sin = sin_ref[...]
    _norm_rope_heads(zq_ref[...].astype(F32), qw_ref[...], cos, sin, q_ref, ATTN_HEADS, ATTN_HEAD_DIM ** -0.5)
    _norm_rope_heads(zk_ref[...].astype(F32), kw_ref[...], cos, sin, k_ref, ATTN_KV_HEADS, 1.0)
    ones = jnp.ones((zv_ref.shape[0], ATTN_HEAD_DIM), BF16)
    for h in range(ATTN_KV_HEADS):
        v_ref[h] = jnp.concatenate([zv_ref[:, h * ATTN_HEAD_DIM:(h + 1) * ATTN_HEAD_DIM].astype(BF16), ones], axis=1)


def attn_prep(z, q_norm_w, k_norm_w, cos_t, sin_t):
    tm = 256
    n = z.shape[0]
    qw = jnp.tile(q_norm_w, ATTN_HEADS).reshape(1, ATTN_W)
    kw = jnp.tile(k_norm_w, ATTN_KV_HEADS).reshape(1, ATTN_KV_W)
    hm = lambda h, w=ATTN_HEAD_DIM: pl.BlockSpec((h, tm, w), lambda i: (0, i, 0))
    sds = lambda h, w=ATTN_HEAD_DIM: jax.ShapeDtypeStruct((h, n, w), BF16)
    return pl.pallas_call(
        _attn_prep_kernel,
        grid=(n // tm,),
        in_specs=[
            pl.BlockSpec((tm, ATTN_W), lambda i: (i, Z_Q // ATTN_W)),
            pl.BlockSpec((tm, ATTN_KV_W), lambda i: (i, Z_K // ATTN_KV_W)),
            pl.BlockSpec((tm, ATTN_KV_W), lambda i: (i, Z_V // ATTN_KV_W)),
            pl.BlockSpec((1, ATTN_W), lambda i: (0, 0)),
            pl.BlockSpec((1, ATTN_KV_W), lambda i: (0, 0)),
            pl.BlockSpec((tm, ATTN_HEAD_DIM), lambda i: (i, 0)),
            pl.BlockSpec((tm, ATTN_HEAD_DIM), lambda i: (i, 0)),
        ],
        out_specs=[hm(ATTN_HEADS), hm(ATTN_KV_HEADS), hm(ATTN_KV_HEADS, 2 * ATTN_HEAD_DIM)],
        out_shape=[sds(ATTN_HEADS), sds(ATTN_KV_HEADS), sds(ATTN_KV_HEADS, 2 * ATTN_HEAD_DIM)],
        compiler_params=_cparams(("parallel",)),
        name="attn_prep",
    )(z, z, z, qw, kw, cos_t, sin_t)


def _attn_kernel(sink_ref, q_ref, kp_ref, ko_ref, kn_ref, kc_ref, vp_ref, vo_ref, vn_ref, vc_ref, o_ref, *, n_blocks):
    i = pl.program_id(0)
    ctx_blocks = CTX_LEN // ATTN_BLOCK
    B = ATTN_BLOCK
    rows = ATTN_GROUP * B
    ncol = 3 * B + CTX_LEN
    r = lax.broadcasted_iota(jnp.int32, (rows, ncol), 0) % B
    c = lax.broadcasted_iota(jnp.int32, (rows, ncol), 1)
    lo = jnp.where(i > ctx_blocks, 0, B)
    hi = jnp.where(i < n_blocks - 1, 3 * B, 2 * B)
    hi = jnp.where(i < ctx_blocks, 0, hi)
    band = (c >= r) & (c <= r + 2 * WINDOW) & (c >= lo) & (c < hi)
    mask = band | (c >= 3 * B)
    grp = lax.broadcasted_iota(jnp.int32, (rows, 1), 0) // B
    for kh in range(ATTN_KV_HEADS):
        q3 = jnp.concatenate([q_ref[kh * ATTN_GROUP + g] for g in range(ATTN_GROUP)], axis=0)
        kcat = jnp.concatenate([kp_ref[kh], ko_ref[kh], kn_ref[kh], kc_ref[kh]], axis=0)
        vcat = jnp.concatenate([vp_ref[kh], vo_ref[kh], vn_ref[kh], vc_ref[kh]], axis=0)
        s = lax.dot_general(q3, kcat, (((1,), (1,)), ((), ())), preferred_element_type=F32)
        s = jnp.where(mask, s, NEG_INF)
        sink = jnp.zeros((rows, 1), F32)
        for g in range(ATTN_GROUP):
            sink = jnp.where(grp == g, sink_ref[kh * ATTN_GROUP + g], sink)
        m = jnp.maximum(jnp.max(s, axis=-1, keepdims=True), sink)
        p = jnp.exp(s - m)
        pv = jnp.dot(p.astype(BF16), vcat, preferred_element_type=F32)
        den = pv[:, ATTN_HEAD_DIM:ATTN_HEAD_DIM + 1] + jnp.exp(sink - m)
        o = pv[:, :ATTN_HEAD_DIM] / den
        for g in range(ATTN_GROUP):
            h = kh * ATTN_GROUP + g
            o_ref[:, h * ATTN_HEAD_DIM:(h + 1) * ATTN_HEAD_DIM] = o[g * B:(g + 1) * B].astype(o_ref.dtype)


def attention(qh, kh, vh, sink):
    n = qh.shape[1]
    B = ATTN_BLOCK
    nblk = n // B
    cb = CTX_LEN // B
    prev = lambda i: (0, jnp.clip(i - 1, cb, nblk - 1), 0)
    own = lambda i: (0, i, 0)
    nxt = lambda i: (0, jnp.clip(i + 1, cb, nblk - 1), 0)
    ctx = lambda i: (0, 0, 0)
    kv = lambda m, w=ATTN_HEAD_DIM: pl.BlockSpec((ATTN_KV_HEADS, B, w), m)
    kvc = lambda w=ATTN_HEAD_DIM: pl.BlockSpec((ATTN_KV_HEADS, CTX_LEN, w), ctx)
    vw = 2 * ATTN_HEAD_DIM
    return pl.pallas_call(
        functools.partial(_attn_kernel, n_blocks=nblk),
        grid=(nblk,),
        in_specs=[
            pl.BlockSpec(memory_space=pltpu.SMEM),
            pl.BlockSpec((ATTN_HEADS, B, ATTN_HEAD_DIM), own),
            kv(prev), kv(own), kv(nxt), kvc(),
            kv(prev, vw), kv(own, vw), kv(nxt, vw), kvc(vw),
        ],
        out_specs=pl.BlockSpec((B, ATTN_W), lambda i: (i, 0)),
        out_shape=jax.ShapeDtypeStruct((n, ATTN_W), BF16),
        compiler_params=_cparams(("parallel",)),
        name="attention",
    )(sink, qh, kh, kh, kh, kh, vh, vh, vh, vh)


def rope_tables(n_lat):
    rows = n_lat // GRID_W
    row = np.repeat(np.arange(rows), GRID_W).astype(np.float32)
    col = np.tile(np.arange(GRID_W), rows).astype(np.float32)
    inv = (ROPE_BASE ** (-np.arange(ROPE_F, dtype=np.float32) / ROPE_F)).astype(np.float32)
    ar, ac = row[:, None] * inv, col[:, None] * inv
    cos = np.concatenate([np.cos(ar), np.cos(ar), np.cos(ac), np.cos(ac)], axis=1)
    sin = np.concatenate([-np.sin(ar), np.sin(ar), -np.sin(ac), np.sin(ac)], axis=1)
    cos = np.concatenate([np.ones((CTX_LEN, ATTN_HEAD_DIM), np.float32), cos], axis=0)
    sin = np.concatenate([np.zeros((CTX_LEN, ATTN_HEAD_DIM), np.float32), sin], axis=0)
    return jnp.asarray(cos, F32), jnp.asarray(sin, F32)


def attn_mixer(z, q_norm_w, k_norm_w, sink, cos_t, sin_t):
    qh, kh, vh = attn_prep(z, q_norm_w, k_norm_w, cos_t, sin_t)
    return attention(qh, kh, vh, sink)


DN_QKV = 3 * DN_W
DN_HALO = 16
DN_DH = 2 * DN_HEADS
DN_C = DN_CHUNK
DN_SPS = 4


def _dn_prep_kernel(zm_ref, zp_ref, zn_ref, zg_ref, cw_ref, al_ref, dtb_ref, q_ref, k_ref, v_ref, g_ref, *, tm, n_tiles):
    i = pl.program_id(0)
    ctx_tiles = CTX_LEN // tm
    has_prev = (i != 0) & (i != ctx_tiles)
    has_next = (i != ctx_tiles - 1) & (i != n_tiles - 1)
    prev = jnp.where(has_prev, zp_ref[...].astype(F32), 0.0)
    nxt = jnp.where(has_next, zn_ref[...].astype(F32), 0.0)
    xcat = jnp.concatenate([prev, zm_ref[...].astype(F32), nxt], axis=0)
    half = DN_CONV // 2
    acc = None
    for t in range(DN_CONV):
        off = DN_HALO - half + t
        term = xcat[off:off + tm, :] * cw_ref[t:t + 1, :]
        acc = term if acc is None else acc + term
    y = acc * jax.nn.sigmoid(acc)
    for h in range(DN_HEADS):
        qh = y[:, h * DN_HEAD_DIM:(h + 1) * DN_HEAD_DIM]
        kh = y[:, DN_W + h * DN_HEAD_DIM:DN_W + (h + 1) * DN_HEAD_DIM]
        qn = qh * (lax.rsqrt(jnp.sum(qh * qh, axis=-1, keepdims=True) + EPS) * (DN_HEAD_DIM ** -0.5))
        q_ref[h] = qn.astype(q_ref.dtype)
        k_ref[h] = (kh * lax.rsqrt(jnp.sum(kh * kh, axis=-1, keepdims=True) + EPS)).astype(k_ref.dtype)
        v_ref[h] = y[:, 2 * DN_W + h * DN_HEAD_DIM:2 * DN_W + (h + 1) * DN_HEAD_DIM].astype(v_ref.dtype)
    zg = zg_ref[...].astype(F32)
    lane = lax.broadcasted_iota(jnp.int32, zg.shape, 1)
    beta = jax.nn.sigmoid(zg)
    alpha = pltpu.roll(zg, 128 - DN_DH, 1)
    gl = -jnp.exp(al_ref[...]) * jax.nn.softplus(alpha + dtb_ref[...])
    g_ref[...] = jnp.where(lane < DN_DH, beta, pltpu.roll(gl, DN_DH, 1))


def dn_prep(z, conv_w, a_log, dt_bias):
    tm = 256
    n = z.shape[0]
    n_tiles = n // tm
    hb = tm // DN_HALO
    cw = jnp.zeros((8, DN_QKV), F32).at[0:DN_CONV].set(conv_w)
    pad_row = lambda t: jnp.zeros((1, 128), F32).at[0, 0:DN_DH].set(t.reshape(-1))
    hm = pl.BlockSpec((DN_HEADS, tm, DN_HEAD_DIM), lambda i: (0, i, 0))
    sds = jax.ShapeDtypeStruct((DN_HEADS, n, DN_HEAD_DIM), BF16)
    return pl.pallas_call(
        functools.partial(_dn_prep_kernel, tm=tm, n_tiles=n_tiles),
        grid=(n_tiles,),
        in_specs=[
            pl.BlockSpec((tm, DN_QKV), lambda i: (i, 0)),
            pl.BlockSpec((DN_HALO, DN_QKV), lambda i: (jnp.maximum(i * hb - 1, 0), 0)),
            pl.BlockSpec((DN_HALO, DN_QKV), lambda i: (jnp.minimum((i + 1) * hb, n // DN_HALO - 1), 0)),
            pl.BlockSpec((tm, 128), lambda i: (i, Z_GATES // 128)),
            pl.BlockSpec((8, DN_QKV), lambda i: (0, 0)),
            pl.BlockSpec((1, 128), lambda i: (0, 0)),
            pl.BlockSpec((1, 128), lambda i: (0, 0)),
        ],
        out_specs=[hm, hm, hm, pl.BlockSpec((tm, 128), lambda i: (i, 0))],
        out_shape=[sds, sds, sds, jax.ShapeDtypeStruct((n, 128), F32)],
        compiler_params=_cparams(("parallel",)),
        name="dn_prep",
    )(z, z, z, z, cw, pad_row(a_log), pad_row(dt_bias))


def _bdot(a, b):
    return jnp.dot(a.astype(BF16), b.astype(BF16), preferred_element_type=F32)


def _unit_tri_inverse_many(nmats, row, col):
    eye = (row == col).astype(F32)
    same = lambda b: (row // b) == (col // b)
    d1 = [jnp.where(same(8), m, 0.0) for m in nmats]
    d2 = [_bdot(a, a) for a in d1]
    d3 = [_bdot(a, b) for a, b in zip(d1, d2)]
    d4 = [_bdot(b, b) for b in d2]
    x = [eye + a + b + c for a, b, c in zip(d1, d2, d3)]
    t = [a + _bdot(a, b) for a, b in zip(x, d4)]
    for b in (8, 16, 32):
        sel = same(2 * b) & jnp.logical_not(same(b))
        tmp = [_bdot(jnp.where(sel, m, 0.0), a) for m, a in zip(nmats, t)]
        t = [a + _bdot(a, c) for a, c in zip(t, tmp)]
    return t


def _dn_fused_kernel(qf_ref, kf_ref, vf_ref, gf_ref, qr_ref, kr_ref, vr_ref, gr_ref, of_ref, or_ref, state):
    @pl.when(pl.program_id(0) == 0)
    def _():
        state[...] = jnp.zeros_like(state)

    C = DN_C
    row = lax.broadcasted_iota(jnp.int32, (C, C), 0)
    col = lax.broadcasted_iota(jnp.int32, (C, C), 1)
    lane = lax.broadcasted_iota(jnp.int32, (C, 128), 1)
    tri = ((row >= col).astype(F32), (row <= col).astype(F32))
    src = ((qf_ref, kf_ref, vf_ref, gf_ref), (qr_ref, kr_ref, vr_ref, gr_ref))
    outs = (of_ref, or_ref)
    inst = [(c, d, h) for c in range(DN_SPS) for d in range(2) for h in range(DN_HEADS)]
    rows = lambda c: slice(c * C, (c + 1) * C)
    g_all, gc_all, gc_t, tot = {}, {}, {}, {}
    for c in range(DN_SPS):
        for d in range(2):
            g = src[d][3][rows(c), :]
            gpart = jnp.where((lane >= DN_DH) & (lane < 2 * DN_DH), g, 0.0)
            gc = jnp.dot(tri[d], gpart, preferred_element_type=F32, precision=HIGHEST)
            g_all[c, d], gc_all[c, d], gc_t[c, d] = g, gc, jnp.transpose(gc)
            tot[c, d] = jnp.sum(gpart, axis=0, keepdims=True)
    qs = {(c, d, h): src[d][0][h, rows(c), :].astype(F32) for c, d, h in inst}
    ks = {(c, d, h): src[d][1][h, rows(c), :].astype(F32) for c, d, h in inst}
    kbs, egcs, decays, kks = [], [], [], []
    for c, d, h in inst:
        j = d * DN_HEADS + h
        incl = (row >= col) if d == 0 else (row <= col)
        gc_col = gc_all[c, d][:, DN_DH + j:DN_DH + j + 1]
        gc_row = gc_t[c, d][DN_DH + j:DN_DH + j + 1, :]
        decays.append(jnp.where(incl, jnp.exp(jnp.where(incl, gc_col - gc_row, 0.0)), 0.0))
        egcs.append(jnp.exp(gc_col))
        kbs.append(ks[c, d, h] * g_all[c, d][:, j:j + 1])
    for i, kb in zip(inst, kbs):
        kks.append(lax.dot_general(jnp.concatenate([kb, qs[i]], axis=0).astype(BF16), ks[i].astype(BF16),
                                   (((1,), (1,)), ((), ())), preferred_element_type=F32))
    nmats = []
    for (c, d, h), kk, decay in zip(inst, kks, decays):
        strict = (row > col) if d == 0 else (row < col)
        nmats.append(jnp.where(strict, -(kk[:C] * decay), 0.0))
    tinv = _unit_tri_inverse_many(nmats, row, col)
    wy = {}
    for (c, d, h), t, kb, egc, kk, decay in zip(inst, tinv, kbs, egcs, kks, decays):
        j = d * DN_HEADS + h
        incl = (row >= col) if d == 0 else (row <= col)
        rhs = jnp.concatenate([src[d][2][h, rows(c), :].astype(F32) * g_all[c, d][:, j:j + 1], kb * egc], axis=1)
        sol = _bdot(t, rhs)
        gc_col = gc_all[c, d][:, DN_DH + j:DN_DH + j + 1]
        tot_j = tot[c, d][:, DN_DH + j:DN_DH + j + 1]
        wy[c, d, h] = (sol[:, :DN_HEAD_DIM],
                       jnp.concatenate([sol[:, DN_HEAD_DIM:], qs[c, d, h] * egc], axis=0).astype(BF16),
                       jnp.transpose(ks[c, d, h] * jnp.exp(tot_j - gc_col)).astype(BF16),
                       jnp.where(incl, kk[C:] * decay, 0.0).astype(BF16),
                       jnp.exp(tot_j))
    chains = [(d, h) for d in range(2) for h in range(DN_HEADS)]
    ss = [state[d * DN_HEADS + h] for d, h in chains]
    for step in range(DN_SPS):
        sub = [step if d == 0 else DN_SPS - 1 - step for d, h in chains]
        parts = [wy[c, d, h] for (d, h), c in zip(chains, sub)]
        ts = [jnp.dot(p[1], s.astype(BF16), preferred_element_type=F32) for p, s in zip(parts, ss)]
        vbs = [(p[0] - t[:C]).astype(BF16) for p, t in zip(parts, ts)]
        os_ = [t[C:] + jnp.dot(p[3], vb, preferred_element_type=F32) for p, t, vb in zip(parts, ts, vbs)]
        ss = [s * p[4] + jnp.dot(p[2], vb, preferred_element_type=F32) for p, s, vb in zip(parts, ss, vbs)]
        for (d, h), o, c in zip(chains, os_, sub):
            outs[d][h, rows(c), :] = o.astype(outs[d].dtype)
    for (d, h), s_new in zip(chains, ss):
        state[d * DN_HEADS + h] = s_new


def dn_fused(qn, kn, vn, gates):
    n = qn.shape[1]
    rows = DN_C * DN_SPS
    nb = n // rows
    cb = CTX_LEN // rows
    fwd = lambda s: s
    rev = lambda s: jnp.where(s < cb, cb - 1 - s, nb + cb - 1 - s)
    specs = []
    for cm in (fwd, rev):
        hm = pl.BlockSpec((DN_HEADS, rows, DN_HEAD_DIM), lambda s, cm=cm: (0, cm(s), 0))
        specs += [hm, hm, hm, pl.BlockSpec((rows, 128), lambda s, cm=cm: (cm(s), 0))]
    osd = jax.ShapeDtypeStruct((DN_HEADS, n, DN_HEAD_DIM), BF16)
    return pl.pallas_call(
        _dn_fused_kernel,
        grid=(nb,),
        in_specs=specs,
        out_specs=[pl.BlockSpec((DN_HEADS, rows, DN_HEAD_DIM), lambda s: (0, fwd(s), 0)),
                   pl.BlockSpec((DN_HEADS, rows, DN_HEAD_DIM), lambda s: (0, rev(s), 0))],
        out_shape=[osd, osd],
        scratch_shapes=[pltpu.VMEM((DN_DH, DN_HEAD_DIM, DN_HEAD_DIM), F32)],
        compiler_params=_cparams(("arbitrary",)),
        name="dn_fused",
    )(qn, kn, vn, gates, qn, kn, vn, gates)


def dn_mixer(z, conv_w, a_log, dt_bias):
    qn, kn, vn, gates = dn_prep(z, conv_w, a_log, dt_bias)
    return dn_fused(qn, kn, vn, gates)


S5_SUB = 8
S5_BLK = S5_GROUPS * S5_STATE // S5_SUB
S5_UB = S5_W // S5_SUB
S5_T = 256


def _s5_param_kernel(lr_ref, li_ref, ls_ref, br_ref, bi_ref, ar_ref, ai_ref, bbr_ref, bbi_ref):
    lr = jnp.minimum(lr_ref[...], -1e-4)
    li = li_ref[...]
    dt = jnp.exp(ls_ref[...])
    mag = jnp.exp(lr * dt)
    ar = mag * jnp.cos(li * dt)
    ai = mag * jnp.sin(li * dt)
    den = lr * lr + li * li
    nr, ni = ar - 1.0, ai
    fr = (nr * lr + ni * li) / den
    fi = (ni * lr - nr * li) / den
    ar_ref[...] = ar
    ai_ref[...] = ai
    for h in range(S5_GROUP_CH):
        bbr_ref[h] = fr * br_ref[h] - fi * bi_ref[h]
        bbi_ref[h] = fr * bi_ref[h] + fi * br_ref[h]


def s5_params(lam_re, lam_im, log_step, b_re, b_im):
    r = 2 * S5_GROUPS
    ls = jnp.broadcast_to(log_step.reshape(r, 1), (r, S5_STATE))
    bt = lambda t: jnp.transpose(t.reshape(r, S5_STATE, S5_GROUP_CH), (2, 0, 1))
    sds = jax.ShapeDtypeStruct
    return pl.pallas_call(
        _s5_param_kernel,
        out_shape=[sds((r, S5_STATE), F32), sds((r, S5_STATE), F32),
                   sds((S5_GROUP_CH, r, S5_STATE), F32), sds((S5_GROUP_CH, r, S5_STATE), F32)],
        name="s5_params",
    )(lam_re.reshape(r, S5_STATE), lam_im.reshape(r, S5_STATE), ls, bt(b_re), bt(b_im))


def _s5_scan_kernel(uf_ref, ur_ref, a_ref, rb_ref, cc_ref, yf_ref, yr_ref, bu, hh, st):
    rows = S5_T * S5_SUB

    @pl.when(pl.program_id(0) == 0)
    def _():
        st[...] = jnp.zeros_like(st)

    sub = lax.broadcasted_iota(jnp.int32, (rows, S5_W), 0) % S5_SUB
    blk = lax.broadcasted_iota(jnp.int32, (rows, S5_W), 1) // S5_UB
    own = sub == blk
    for d, u_ref in enumerate((uf_ref, ur_ref)):
        ue = jnp.broadcast_to(u_ref[...].astype(F32)[:, None, :], (S5_T, S5_SUB, S5_W)).reshape(rows, S5_W)
        lhs = jnp.where(own, ue, 0.0).astype(BF16)
        bu[d, 0] = jnp.dot(lhs, rb_ref[d, 0], preferred_element_type=F32)
        bu[d, 1] = jnp.dot(lhs, rb_ref[d, 1], preferred_element_type=F32)

    a = [[a_ref[d, c] for c in range(2)] for d in range(2)]

    def body(t, carry):
        fr, fi, rr, ri = carry
        rf = pl.multiple_of(t * S5_SUB, S5_SUB)
        rv = pl.multiple_of((S5_T - 1 - t) * S5_SUB, S5_SUB)
        nfr = a[0][0] * fr - a[0][1] * fi + bu[0, 0, pl.ds(rf, S5_SUB), :]
        nfi = a[0][0] * fi + a[0][1] * fr + bu[0, 1, pl.ds(rf, S5_SUB), :]
        nrr = a[1][0] * rr - a[1][1] * ri + bu[1, 0, pl.ds(rv, S5_SUB), :]
        nri = a[1][0] * ri + a[1][1] * rr + bu[1, 1, pl.ds(rv, S5_SUB), :]
        hh[0, 0, pl.ds(rf, S5_SUB), :] = nfr
        hh[0, 1, pl.ds(rf, S5_SUB), :] = nfi
        hh[1, 0, pl.ds(rv, S5_SUB), :] = nrr
        hh[1, 1, pl.ds(rv, S5_SUB), :] = nri
        return nfr, nfi, nrr, nri

    fin = lax.fori_loop(0, S5_T, body, (st[0, 0], st[0, 1], st[1, 0], st[1, 1]), unroll=8)
    st[0, 0], st[0, 1], st[1, 0], st[1, 1] = fin

    for d, y_ref in enumerate((yf_ref, yr_ref)):
        ye = (jnp.dot(hh[d, 0].astype(BF16), cc_ref[d, 0], preferred_element_type=F32)
              + jnp.dot(hh[d, 1].astype(BF16), cc_ref[d, 1], preferred_element_type=F32))
        ye = jnp.where(own, ye, 0.0)
        y_ref[...] = jnp.sum(ye.reshape(S5_T, S5_SUB, S5_W), axis=1)


def s5_scan(z, a8, rb, cc):
    rows = S5_T * S5_SUB
    n = z.shape[0]
    nt = n // S5_T
    ct = CTX_LEN // S5_T
    ucol = Z_S5 // S5_W

    def rev_tile(i):
        return jnp.where(i < ct, ct - 1 - i, nt + ct - 1 - i)

    full = lambda shape: pl.BlockSpec(shape, lambda i: (0,) * len(shape))
    sds = jax.ShapeDtypeStruct((n, S5_W), F32)
    return pl.pallas_call(
        _s5_scan_kernel,
        grid=(nt,),
        in_specs=[
            pl.BlockSpec((S5_T, S5_W), lambda i: (i, ucol)),
            pl.BlockSpec((S5_T, S5_W), lambda i: (rev_tile(i), ucol)),
            full((2, 2, S5_SUB, S5_BLK)),
            full((2, 2, S5_W, S5_BLK)),
            full((2, 2, S5_BLK, S5_W)),
        ],
        out_specs=[pl.BlockSpec((S5_T, S5_W), lambda i: (i, 0)),
                   pl.BlockSpec((S5_T, S5_W), lambda i: (rev_tile(i), 0))],
        out_shape=[sds, sds],
        scratch_shapes=[pltpu.VMEM((2, 2, rows, S5_BLK), F32), pltpu.VMEM((2, 2, rows, S5_BLK), F32),
                        pltpu.VMEM((2, 2, S5_SUB, S5_BLK), F32)],
        compiler_params=_cparams(("arbitrary",)),
        name="s5_scan",
    )(z, z, a8, rb, cc)


def s5_mixer(z, lam_re, lam_im, log_step, b_re, b_im, c_re, c_im):
    ar, ai, bbr, bbi = s5_params(lam_re, lam_im, log_step, b_re, b_im)
    g_blk = jax.nn.one_hot(jnp.arange(S5_GROUPS) % (S5_GROUPS // S5_SUB), S5_GROUPS // S5_SUB, dtype=F32)

    def place_b(bb):
        bb = jnp.transpose(bb.reshape(S5_GROUP_CH, 2, S5_GROUPS, S5_STATE), (1, 2, 0, 3))
        return jnp.einsum('dghp,gj->dghjp', bb, g_blk).reshape(2, S5_W, S5_BLK)

    def place_c(cm):
        return jnp.einsum('dghp,gj->djpgh', cm, g_blk).reshape(2, S5_BLK, S5_W)

    rb = jnp.stack([place_b(bbr), place_b(bbi)], axis=1).astype(BF16)
    cc = jnp.stack([place_c(c_re), -place_c(c_im)], axis=1).astype(BF16)
    a8 = jnp.stack([ar.reshape(2, S5_SUB, S5_BLK), ai.reshape(2, S5_SUB, S5_BLK)], axis=1)
    return s5_scan(z, a8, rb, cc)


def _permute_w_in(w):
    n_gate = 4 * DN_HEADS
    attn = w[:, :QKV_W]
    dn = w[:, QKV_W:QKV_W + 4 * DN_W]
    gates = w[:, QKV_W + 4 * DN_W:QKV_W + 4 * DN_W + n_gate]
    s5 = w[:, QKV_W + 4 * DN_W + n_gate:]
    pad = jnp.zeros((w.shape[0], Z_S5 - Z_GATES - n_gate), BF16)
    return jnp.concatenate([dn.astype(BF16), attn.astype(BF16), gates.astype(BF16), pad, s5.astype(BF16)], axis=1)


def kernel(x, c, ctx, c_ctx, w_ada, b_ada, norm1_w, norm2_w, w_in, w_out, attn_q_norm, attn_k_norm, attn_sink,
           dn_conv, dn_a_log, dn_dt_bias, dn_o_norm, s5_lam_re, s5_lam_im, s5_log_step, s5_b_re, s5_b_im,
           s5_c_re, s5_c_im, s5_d, s5_w_glu, moe_w_grp, moe_b_grp, moe_w_rt, moe_b_rt, moe_w1, moe_w3, moe_w2):
    b, n, d = x.shape
    assert b == 1 and ctx.shape[1] == CTX_LEN and d == D_MODEL
    lc = CTX_LEN
    xa = jnp.concatenate([ctx[0], x[0]], axis=0)
    mods = adaln(c_ctx, c[0], w_ada, b_ada)
    cos_t, sin_t = rope_tables(n)
    w1_all = moe_w1.reshape(DEPTH * N_EXPERTS, D_MODEL, EXPERT_HIDDEN)
    w3_all = moe_w3.reshape(DEPTH * N_EXPERTS, D_MODEL, EXPERT_HIDDEN)
    w2_all = moe_w2.reshape(DEPTH * N_EXPERTS, EXPERT_HIDDEN, D_MODEL)
    for layer in range(DEPTH):
        mod = mods[layer, 0:2].reshape(2, 6, D_MODEL)
        mod = jnp.concatenate([mod, jnp.zeros((2, 2, D_MODEL), F32)], axis=1)
        z = in_proj(xa, norm1_w[layer], mod, _permute_w_in(w_in[layer]))
        mix_a = attn_mixer(z, attn_q_norm[layer], attn_k_norm[layer], attn_sink[layer], cos_t, sin_t)
        o_f, o_r = dn_mixer(z, dn_conv[layer], dn_a_log[layer], dn_dt_bias[layer])
        yf, yr = s5_mixer(z, s5_lam_re[layer], s5_lam_im[layer], s5_log_step[layer], s5_b_re[layer],
                          s5_b_im[layer], s5_c_re[layer], s5_c_im[layer])
        w_router = jnp.concatenate([moe_w_grp[layer], moe_w_rt[layer],
                                    jnp.zeros((D_MODEL, 128 - N_GROUPS - N_EXPERTS), F32)], axis=1)
        b_router = jnp.concatenate([moe_b_grp[layer], moe_b_rt[layer],
                                    jnp.zeros((128 - N_GROUPS - N_EXPERTS,), F32)]).reshape(1, 128)
        xa, h2, route = out_proj(mix_a, o_f, o_r, z, yf, yr, xa, mod, w_out[layer].astype(BF16), norm2_w[layer],
                                 w_router, b_router, dn_o_norm[layer], s5_d[layer], s5_w_glu[layer].astype(BF16))
        xa = moe_block(xa, h2, route, mod, w1_all, w3_all, w2_all, layer, skip_ctx=(layer == DEPTH - 1))
    return xa[None]
```
